```python
import math
import jax, jax.numpy as jnp
from jax import lax
import numpy as np

D_MODEL = 2048
BATCH = 4
SEQ = 4096
DEPTH = 2

POOL_WIDTH = D_MODEL // 2
POOL_WINDOWS = (2, 4, 8, 16)
N_POOL_GROUPS = len(POOL_WINDOWS)
POOL_GROUP = POOL_WIDTH // N_POOL_GROUPS
DIFF_HEADS = 8
DIFF_HEAD_DIM = 64
DIFF_WIDTH = DIFF_HEADS * 2 * DIFF_HEAD_DIM
Q_BLOCK = 128
MIX_IN_EVEN = POOL_WIDTH + 3 * DIFF_WIDTH
MIX_OUT_EVEN = POOL_WIDTH + DIFF_WIDTH
CONV_WIDTH = D_MODEL
CONV_K = 3
D_FF = 5632
N_EXPERTS = 8
TOP_K = 2
EXPERT_FF = 7168
LN_EPS = 1e-5
RMS_EPS = 1e-5
ALPHA = (2 * DEPTH) ** 0.25
BETA = (8 * DEPTH) ** -0.25
N_EVEN = (DEPTH + 1) // 2
N_ODD = DEPTH // 2

kernel_name = 'hybrid_pool_diffattn_shortconv_moe_deepnorm'


def layer_norm(x, g, b):
    xf = x.astype(jnp.float32)
    mu = jnp.mean(xf, axis=-1, keepdims=True)
    var = jnp.mean(jnp.square(xf - mu), axis=-1, keepdims=True)
    return ((xf - mu) * lax.rsqrt(var + LN_EPS) * g + b).astype(x.dtype)


def lambda_init_fn(layer_idx):
    return 0.8 - 0.6 * math.exp(-0.3 * layer_idx)


def causal_pool_minus_identity(u, window):
    s = u.shape[1]
    c = jnp.cumsum(u.astype(jnp.float32), axis=1)
    c_prev = jnp.pad(c, ((0, 0), (window, 0), (0, 0)))[:, :s]
    count = jnp.minimum(jnp.arange(1, s + 1), window).astype(jnp.float32)
    mean = (c - c_prev) / count[None, :, None]
    return (mean - u.astype(jnp.float32)).astype(u.dtype)


def pool_mixer(u, w_group, scale):
    groups = jnp.split(u, N_POOL_GROUPS, axis=-1)
    pooled = jnp.stack([causal_pool_minus_identity(g_, w) for g_, w in zip(groups, POOL_WINDOWS)], axis=2)
    mixed = jnp.einsum('bsgc,gcd->bsgd', pooled, w_group)
    return mixed.reshape(u.shape) * scale


def diff_attention(q, k, v, lam):
    b, h, _, s, d = q.shape
    n_blk = s // Q_BLOCK
    scale = d ** -0.5
    kf = k.astype(jnp.float32)
    vf = v.astype(jnp.float32)
    qb = q.reshape(b, h, 2, n_blk, Q_BLOCK, d).transpose(3, 0, 1, 2, 4, 5)
    k_pos = jnp.arange(s)

    def one_block(args):
        q_blk, start = args
        sc = jnp.einsum('bhcqd,bhckd->bhcqk', q_blk.astype(jnp.float32), kf) * scale
        q_pos = start + jnp.arange(Q_BLOCK)
        mask = k_pos[None, :] <= q_pos[:, None]
        sc = jnp.where(mask, sc, -jnp.inf)
        p = jax.nn.softmax(sc, axis=-1)
        a = p[:, :, 0] - lam * p[:, :, 1]
        return jnp.einsum('bhqk,bhkv->bhqv', a, vf)

    starts = jnp.arange(n_blk) * Q_BLOCK
    out = lax.map(one_block, (qb, starts))
    return out.transpose(1, 2, 0, 3, 4).reshape(b, h, s, 2 * d)


def even_mixer(x, w_in, w_pool, pool_scale, lam_q1, lam_k1, lam_q2, lam_k2, subln_g, w_out, lam_init):
    b, s, _ = x.shape
    hcat = x @ w_in
    u_pool, q, k, v = jnp.split(hcat, [POOL_WIDTH, POOL_WIDTH + DIFF_WIDTH, POOL_WIDTH + 2 * DIFF_WIDTH], axis=-1)
    q = q.reshape(b, s, DIFF_HEADS, 2, DIFF_HEAD_DIM).transpose(0, 2, 3, 1, 4)
    k = k.reshape(b, s, DIFF_HEADS, 2, DIFF_HEAD_DIM).transpose(0, 2, 3, 1, 4)
    v = v.reshape(b, s, DIFF_HEADS, 2 * DIFF_HEAD_DIM).transpose(0, 2, 1, 3)
    lam = (jnp.exp(jnp.sum(lam_q1.astype(jnp.float32) * lam_k1.astype(jnp.float32)))
           - jnp.exp(jnp.sum(lam_q2.astype(jnp.float32) * lam_k2.astype(jnp.float32))) + lam_init)
    o = diff_attention(q, k, v, lam)
    o = o * lax.rsqrt(jnp.mean(jnp.square(o), axis=-1, keepdims=True) + RMS_EPS) * subln_g * (1.0 - lam_init)
    o = o.transpose(0, 2, 1, 3).reshape(b, s, DIFF_WIDTH).astype(x.dtype)
    a = pool_mixer(u_pool, w_pool, pool_scale)
    return jnp.concatenate([a, o], axis=-1) @ w_out


def odd_mixer(x, w_in, conv_w, w_out):
    hcat = x @ w_in
    b_gate, c_gate, hx = jnp.split(hcat, 3, axis=-1)
    z = c_gate * hx
    y = lax.conv_general_dilated(z, conv_w, window_strides=(1,), padding=[(CONV_K - 1, 0)],
                                 dimension_numbers=('NWC', 'WIO', 'NWC'), feature_group_count=CONV_WIDTH)
    return (b_gate * y) @ w_out


def swiglu(x, wg, wu, wd):
    return (jax.nn.silu(x @ wg) * (x @ wu)) @ wd


def moe_swiglu(x, w_router, wg, wu, wd):
    xt = x.reshape(-1, x.shape[-1])
    logits = (xt @ w_router).astype(jnp.float32)
    top_v, top_i = lax.top_k(logits, TOP_K)
    gates = jax.nn.softmax(top_v, axis=-1)
    combine = jnp.sum(jax.nn.one_hot(top_i, N_EXPERTS, dtype=jnp.float32) * gates[..., None], axis=1)
    out = jnp.zeros(xt.shape, jnp.float32)
    for e in range(N_EXPERTS):
        out = out + combine[:, e:e + 1] * swiglu(xt, wg[e], wu[e], wd[e])
    return out.astype(x.dtype).reshape(x.shape)


def setup_inputs(seed: int = 0) -> dict:
    key = jax.random.key(seed)
    ks = iter(jax.random.split(key, 40))
    f32 = jnp.float32

    def w(shape, fan_in, mult=1.0):
        return jax.random.normal(next(ks), shape, f32) * (fan_in ** -0.5) * mult

    def gain(shape):
        return 1.0 + 0.05 * jax.random.normal(next(ks), shape, f32)

    def bias(shape):
        return 0.02 * jax.random.normal(next(ks), shape, f32)

    x = jax.random.normal(next(ks), (BATCH, SEQ, D_MODEL), f32)
    return {
        'x': x,
        'ev_w_in': w((N_EVEN, D_MODEL, MIX_IN_EVEN), D_MODEL),
        'ev_w_pool': w((N_EVEN, N_POOL_GROUPS, POOL_GROUP, POOL_GROUP), POOL_GROUP),
        'ev_pool_scale': gain((N_EVEN, POOL_WIDTH)),
        'ev_lam_q1': 0.1 * jax.random.normal(next(ks), (N_EVEN, DIFF_HEAD_DIM), f32),
        'ev_lam_k1': 0.1 * jax.random.normal(next(ks), (N_EVEN, DIFF_HEAD_DIM), f32),
        'ev_lam_q2': 0.1 * jax.random.normal(next(ks), (N_EVEN, DIFF_HEAD_DIM), f32),
        'ev_lam_k2': 0.1 * jax.random.normal(next(ks), (N_EVEN, DIFF_HEAD_DIM), f32),
        'ev_subln_g': gain((N_EVEN, 2 * DIFF_HEAD_DIM)),
        'ev_w_out': w((N_EVEN, MIX_OUT_EVEN, D_MODEL), MIX_OUT_EVEN, BETA),
        'ev_ln1_g': gain((N_EVEN, D_MODEL)),
        'ev_ln1_b': bias((N_EVEN, D_MODEL)),
        'ev_ffn_wg': w((N_EVEN, D_MODEL, D_FF), D_MODEL),
        'ev_ffn_wu': w((N_EVEN, D_MODEL, D_FF), D_MODEL),
        'ev_ffn_wd': w((N_EVEN, D_FF, D_MODEL), D_FF, BETA),
        'ev_ln2_g': gain((N_EVEN, D_MODEL)),
        'ev_ln2_b': bias((N_EVEN, D_MODEL)),
        'od_w_in': w((N_ODD, D_MODEL, 3 * CONV_WIDTH), D_MODEL),
        'od_conv_w': w((N_ODD, CONV_K, 1, CONV_WIDTH), CONV_K),
        'od_w_out': w((N_ODD, CONV_WIDTH, D_MODEL), CONV_WIDTH, BETA),
        'od_ln1_g': gain((N_ODD, D_MODEL)),
        'od_ln1_b': bias((N_ODD, D_MODEL)),
        'od_router': w((N_ODD, D_MODEL, N_EXPERTS), D_MODEL),
        'od_exp_wg': w((N_ODD, N_EXPERTS, D_MODEL, EXPERT_FF), D_MODEL),
        'od_exp_wu': w((N_ODD, N_EXPERTS, D_MODEL, EXPERT_FF), D_MODEL),
        'od_exp_wd': w((N_ODD, N_EXPERTS, EXPERT_FF, D_MODEL), EXPERT_FF, BETA),
        'od_ln2_g': gain((N_ODD, D_MODEL)),
        'od_ln2_b': bias((N_ODD, D_MODEL)),
    }


def reference(x, ev_w_in, ev_w_pool, ev_pool_scale, ev_lam_q1, ev_lam_k1, ev_lam_q2, ev_lam_k2,
              ev_subln_g, ev_w_out, ev_ln1_g, ev_ln1_b, ev_ffn_wg, ev_ffn_wu, ev_ffn_wd, ev_ln2_g, ev_ln2_b,
              od_w_in, od_conv_w, od_w_out, od_ln1_g, od_ln1_b, od_router, od_exp_wg, od_exp_wu, od_exp_wd,
              od_ln2_g, od_ln2_b):
    for layer in range(DEPTH):
        i = layer // 2
        if layer % 2 == 0:
            lam_init = lambda_init_fn(layer)
            m = even_mixer(x, ev_w_in[i], ev_w_pool[i], ev_pool_scale[i], ev_lam_q1[i], ev_lam_k1[i],
                           ev_lam_q2[i], ev_lam_k2[i], ev_subln_g[i], ev_w_out[i], lam_init)
            x = layer_norm(ALPHA * x + m, ev_ln1_g[i], ev_ln1_b[i])
            f = swiglu(x, ev_ffn_wg[i], ev_ffn_wu[i], ev_ffn_wd[i])
            x = layer_norm(ALPHA * x + f, ev_ln2_g[i], ev_ln2_b[i])
        else:
            m = odd_mixer(x, od_w_in[i], od_conv_w[i], od_w_out[i])
            x = layer_norm(ALPHA * x + m, od_ln1_g[i], od_ln1_b[i])
            f = moe_swiglu(x, od_router[i], od_exp_wg[i], od_exp_wu[i], od_exp_wd[i])
            x = layer_norm(ALPHA * x + f, od_ln2_g[i], od_ln2_b[i])
    return x
```

```python
import functools
import math

import jax
import jax.numpy as jnp
from jax import lax
from jax.experimental import pallas as pl
from jax.experimental.pallas import tpu as pltpu

F32 = jnp.float32
BF16 = jnp.bfloat16

DEPTH = 2
ALPHA = (2 * DEPTH) ** 0.25
LN_EPS = 1e-5
RMS_EPS = 1e-5
POOL_WINDOWS = (2, 4, 8, 16)
POOL_HALO = 16
DIFF_HEAD_DIM = 64
HEAD_WIDTH = 2 * DIFF_HEAD_DIM
LAM_INIT_LAYER0 = 0.8 - 0.6 * math.exp(-0.3 * 0)
CONV_K = 3
CONV_HALO = 8
TOP_K = 2
ROUTER_LANES = 128
CUMSUM_CHUNK = 256

VMEM_LIMIT_BYTES = 56 * 1024 * 1024


def _tile(n, pref):
    t = min(n, pref)
    assert n % t == 0, (n, t)
    return t


def _params(*sem):
    return pltpu.CompilerParams(dimension_semantics=sem, vmem_limit_bytes=VMEM_LIMIT_BYTES)


def _layer_norm(v, g, b):
    mu = jnp.mean(v, axis=-1, keepdims=True)
    d = v - mu
    var = jnp.mean(d * d, axis=-1, keepdims=True)
    return d * lax.rsqrt(var + LN_EPS) * g + b


def _even_inproj_kernel(x_ref, w_ref, pool_ref, qkv_ref, xb_ref):
    j = pl.program_id(1)

    @pl.when(j == 0)
    def _():
        xb_ref[...] = x_ref[...].astype(BF16)

    acc = jnp.dot(xb_ref[...], w_ref[...], preferred_element_type=F32)

    @pl.when(j == 0)
    def _():
        pool_ref[...] = acc

    @pl.when(j > 0)
    def _():
        qkv_ref[...] = acc.astype(BF16)


def _even_inproj(x2d, w_bf16, pool_width):
    n, d = x2d.shape
    width = w_bf16.shape[1]
    tn = pool_width
    assert (width - pool_width) % tn == 0
    tm = _tile(n, 1024)
    return pl.pallas_call(
        _even_inproj_kernel,
        grid=(n // tm, width // tn),
        in_specs=[pl.BlockSpec((tm, d), lambda i, j: (i, 0)),
                  pl.BlockSpec((d, tn), lambda i, j: (0, j))],
        out_specs=[pl.BlockSpec((tm, tn), lambda i, j: (i, 0)),
                   pl.BlockSpec((tm, tn), lambda i, j: (i, jnp.maximum(j - 1, 0)))],
        out_shape=[jax.ShapeDtypeStruct((n, pool_width), F32),
                   jax.ShapeDtypeStruct((n, width - pool_width), BF16)],
        scratch_shapes=[pltpu.VMEM((tm, d), BF16)],
        compiler_params=_params("arbitrary", "arbitrary"),
        name="even_inproj",
    )(x2d, w_bf16)


def _pool_kernel(u_ref, halo_ref, w_ref, scale_ref, o_ref, buf_ref, *, seq, tp, cg):
    i = pl.program_id(0)
    first = (i * tp) % seq == 0
    pos = (i * tp) % seq + lax.broadcasted_iota(jnp.int32, (tp, 1), 0)
    for g, window in enumerate(POOL_WINDOWS):
        cols = slice(g * cg, (g + 1) * cg)
        u = u_ref[:, cols]
        buf_ref[0:POOL_HALO, :] = jnp.where(first, 0.0, halo_ref[:, cols])
        buf_ref[POOL_HALO:, :] = u
        total = u
        for back in range(1, window):
            total = total + buf_ref[pl.ds(POOL_HALO - back, tp), :]
        count = jnp.minimum(pos + 1, window).astype(F32)
        pooled = total / count - u
        mixed = jnp.dot(pooled.astype(BF16), w_ref[g], preferred_element_type=F32)
        o_ref[:, cols] = (mixed * scale_ref[:, cols]).astype(BF16)


def _pool_mixer(u, w_pool_bf16, scale, seq):
    n, pw = u.shape
    groups, cg, _ = w_pool_bf16.shape
    assert groups == len(POOL_WINDOWS) and groups * cg == pw
    tp = _tile(seq, 512)
    halo_blocks = tp // POOL_HALO
    kern = functools.partial(_pool_kernel, seq=seq, tp=tp, cg=cg)
    return pl.pallas_call(
        kern,
        grid=(n // tp,),
        in_specs=[pl.BlockSpec((tp, pw), lambda i: (i, 0)),
                  pl.BlockSpec((POOL_HALO, pw), lambda i: (jnp.maximum(i * halo_blocks - 1, 0), 0)),
                  pl.BlockSpec((groups, cg, cg), lambda i: (0, 0, 0)),
                  pl.BlockSpec((1, pw), lambda i: (0, 0))],
        out_specs=pl.BlockSpec((tp, pw), lambda i: (i, 0)),
        out_shape=jax.ShapeDtypeStruct((n, pw), BF16),
        scratch_shapes=[pltpu.VMEM((POOL_HALO + tp, cg), F32)],
        compiler_params=_params("arbitrary"),
        name="pool_mixer",
    )(u, u, w_pool_bf16, scale)


def _diff_attn_kernel(q_ref, k_ref, v_ref, lq1_ref, lk1_ref, lq2_ref, lk2_ref, g_ref, o_ref,
                      m_ref, l_ref, acc_ref, *, tq):
    qi = pl.program_id(2)
    rows = 2 * tq
    q = q_ref[...]
    lane = lax.broadcasted_iota(jnp.int32, q.shape, 1)
    zero = jnp.zeros_like(q)
    qq = jnp.concatenate([jnp.where(lane < DIFF_HEAD_DIM, q, zero),
                          jnp.where(lane >= DIFF_HEAD_DIM, q, zero)], axis=0)
    qq = qq * jnp.asarray(DIFF_HEAD_DIM ** -0.5, BF16)

    m_ref[...] = jnp.full((rows, 1), -jnp.inf, F32)
    l_ref[...] = jnp.zeros((rows, 1), F32)
    acc_ref[...] = jnp.zeros((rows, HEAD_WIDTH), F32)

    def step(kb, masked):
        start = pl.multiple_of(kb * tq, tq)
        k = k_ref[pl.ds(start, tq), :]
        v = v_ref[pl.ds(start, tq), :]
        s = lax.dot_general(qq, k, (((1,), (1,)), ((), ())), preferred_element_type=F32)
        if masked:
            r = lax.broadcasted_iota(jnp.int32, (rows, tq), 0)
            r = jnp.where(r >= tq, r - tq, r)
            c = lax.broadcasted_iota(jnp.int32, (rows, tq), 1)
            s = jnp.where(c <= r, s, -jnp.inf)
        m_old = m_ref[...]
        m_new = jnp.maximum(m_old, jnp.max(s, axis=-1, keepdims=True))
        p = jnp.exp(s - m_new)
        scale_old = jnp.exp(m_old - m_new)
        l_ref[...] = scale_old * l_ref[...] + jnp.sum(p, axis=-1, keepdims=True)
        acc_ref[...] = scale_old * acc_ref[...] + jnp.dot(p.astype(BF16), v, preferred_element_type=F32)
        m_ref[...] = m_new

    def body(kb, carry):
        step(kb, False)
        return carry

    lax.fori_loop(0, qi, body, 0)
    step(qi, True)

    lam = (jnp.exp(jnp.sum(lq1_ref[...] * lk1_ref[...], keepdims=True))
           - jnp.exp(jnp.sum(lq2_ref[...] * lk2_ref[...], keepdims=True)) + LAM_INIT_LAYER0)
    o = acc_ref[0:tq, :] / l_ref[0:tq, :] - lam * (acc_ref[tq:, :] / l_ref[tq:, :])
    o = o * lax.rsqrt(jnp.mean(o * o, axis=-1, keepdims=True) + RMS_EPS)
    o_ref[...] = (o * g_ref[...] * (1.0 - LAM_INIT_LAYER0)).astype(BF16)


def _diff_attention(qkv, lq1, lk1, lq2, lk2, subln_g, batch, seq):
    n, w3 = qkv.shape
    width = w3 // 3
    heads = width // HEAD_WIDTH
    tq = _tile(seq, 512)
    nq = seq // tq
    kern = functools.partial(_diff_attn_kernel, tq=tq)
    vec = lambda: pl.BlockSpec((1, DIFF_HEAD_DIM), lambda b, h, i: (0, 0))
    return pl.pallas_call(
        kern,
        grid=(batch, heads, nq),
        in_specs=[pl.BlockSpec((tq, HEAD_WIDTH), lambda b, h, i: (b * nq + i, h)),
                  pl.BlockSpec((seq, HEAD_WIDTH), lambda b, h, i: (b, heads + h)),
                  pl.BlockSpec((seq, HEAD_WIDTH), lambda b, h, i: (b, 2 * heads + h)),
                  vec(), vec(), vec(), vec(),
                  pl.BlockSpec((1, HEAD_WIDTH), lambda b, h, i: (0, 0))],
        out_specs=pl.BlockSpec((tq, HEAD_WIDTH), lambda b, h, i: (b * nq + i, h)),
        out_shape=jax.ShapeDtypeStruct((n, width), BF16),
        scratch_shapes=[pltpu.VMEM((2 * tq, 1), F32), pltpu.VMEM((2 * tq, 1), F32),
                        pltpu.VMEM((2 * tq, HEAD_WIDTH), F32)],
        compiler_params=_params("arbitrary", "arbitrary", "arbitrary"),
        name="diff_attention",
    )(qkv, qkv, qkv, lq1, lk1, lq2, lk2, subln_g)


def _outproj_ln_kernel(*refs, n_lhs, with_router):
    lhs_refs = refs[:n_lhs]
    w_ref, x_ref, g_ref, b_ref = refs[n_lhs:n_lhs + 4]
    rest = refs[n_lhs + 4:]
    y = None
    off = 0
    for a_ref in lhs_refs:
        kp = a_ref.shape[1]
        part = jnp.dot(a_ref[...], w_ref[off:off + kp, :], preferred_element_type=F32)
        y = part if y is None else y + part
        off += kp
    out = _layer_norm(ALPHA * x_ref[...] + y, g_ref[...], b_ref[...])
    if with_router:
        wr_ref, o_ref, logit_ref = rest
        o_ref[...] = out
        logit_ref[...] = jnp.dot(out, wr_ref[...], preferred_element_type=F32,
                                 precision=lax.Precision.HIGHEST)
    else:
        (o_ref,) = rest
        o_ref[...] = out


def _outproj_ln(lhs_list, w_bf16, x2d, g, b, w_router_padded=None):
    n, d = x2d.shape
    tm = _tile(n, 512)
    with_router = w_router_padded is not None
    in_specs = [pl.BlockSpec((tm, a.shape[1]), lambda i: (i, 0)) for a in lhs_list]
    in_specs += [pl.BlockSpec(w_bf16.shape, lambda i: (0, 0)),
                 pl.BlockSpec((tm, d), lambda i: (i, 0)),
                 pl.BlockSpec((1, d), lambda i: (0, 0)),
                 pl.BlockSpec((1, d), lambda i: (0, 0))]
    out_specs = [pl.BlockSpec((tm, d), lambda i: (i, 0))]
    out_shape = [jax.ShapeDtypeStruct((n, d), F32)]
    args = list(lhs_list) + [w_bf16, x2d, g, b]
    if with_router:
        in_specs.append(pl.BlockSpec(w_router_padded.shape, lambda i: (0, 0)))
        out_specs.append(pl.BlockSpec((tm, ROUTER_LANES), lambda i: (i, 0)))
        out_shape.append(jax.ShapeDtypeStruct((n, ROUTER_LANES), F32))
        args.append(w_router_padded)
    kern = functools.partial(_outproj_ln_kernel, n_lhs=len(lhs_list), with_router=with_router)
    res = pl.pallas_call(
        kern,
        grid=(n // tm,),
        in_specs=in_specs,
        out_specs=out_specs,
        out_shape=out_shape,
        compiler_params=_params("arbitrary"),
        name="outproj_ln_router" if with_router else "outproj_ln",
    )(*args)
    return res if with_router else res[0]


def _swiglu_block(xb, wg, wu, wd):
    gate = jnp.dot(xb, wg, preferred_element_type=F32)
    up = jnp.dot(xb, wu, preferred_element_type=F32)
    h = gate / (1.0 + jnp.exp(-gate)) * up
    return jnp.dot(h.astype(BF16), wd, preferred_element_type=F32)


def _ffn_ln_kernel(x_ref, wg_ref, wu_ref, wd_ref, g_ref, b_ref, o_ref, xb_ref, acc_ref):
    j = pl.program_id(1)

    @pl.when(j == 0)
    def _():
        xb_ref[...] = x_ref[...].astype(BF16)
        acc_ref[...] = jnp.zeros_like(acc_ref)

    acc_ref[...] += _swiglu_block(xb_ref[...], wg_ref[...], wu_ref[...], wd_ref[...])

    @pl.when(j == pl.num_programs(1) - 1)
    def _():
        o_ref[...] = _layer_norm(ALPHA * x_ref[...] + acc_ref[...], g_ref[...], b_ref[...])


def _ffn_ln(x2d, wg, wu, wd, g, b):
    n, d = x2d.shape
    ff = wg.shape[1]
    tm = _tile(n, 512)
    tf = _tile(ff, 512)
    return pl.pallas_call(
        _ffn_ln_kernel,
        grid=(n // tm, ff // tf),
        in_specs=[pl.BlockSpec((tm, d), lambda i, j: (i, 0)),
                  pl.BlockSpec((d, tf), lambda i, j: (0, j)),
                  pl.BlockSpec((d, tf), lambda i, j: (0, j)),
                  pl.BlockSpec((tf, d), lambda i, j: (j, 0)),
                  pl.BlockSpec((1, d), lambda i, j: (0, 0)),
                  pl.BlockSpec((1, d), lambda i, j: (0, 0))],
        out_specs=pl.BlockSpec((tm, d), lambda i, j: (i, 0)),
        out_shape=jax.ShapeDtypeStruct((n, d), F32),
        scratch_shapes=[pltpu.VMEM((tm, d), BF16), pltpu.VMEM((tm, d), F32)],
        compiler_params=_params("arbitrary", "arbitrary"),
        name="ffn_ln",
    )(x2d, wg, wu, wd, g, b)


def _odd_inproj_kernel(x_ref, wb_ref, wc_ref, wh_ref, cw_ref, o_ref, xb_ref, zs_ref, carry_ref,
                       *, seq, tm):
    i = pl.program_id(0)
    j = pl.program_id(1)

    @pl.when(j == 0)
    def _():
        xb_ref[...] = x_ref[...].astype(BF16)

    xb = xb_ref[...]
    b_gate = jnp.dot(xb, wb_ref[...], preferred_element_type=F32)
    c_gate = jnp.dot(xb, wc_ref[...], preferred_element_type=F32)
    hx = jnp.dot(xb, wh_ref[...], preferred_element_type=F32)
    z = c_gate * hx
    first = (i * tm) % seq == 0
    zs_ref[0:CONV_HALO, :] = jnp.where(first, 0.0, carry_ref[j])
    zs_ref[CONV_HALO:, :] = z
    carry_ref[j] = z[tm - CONV_HALO:, :]
    y = (cw_ref[0:1, :] * zs_ref[pl.ds(CONV_HALO - 2, tm), :]
         + cw_ref[1:2, :] * zs_ref[pl.ds(CONV_HALO - 1, tm), :]
         + cw_ref[2:3, :] * z)
    o_ref[...] = (b_gate * y).astype(BF16)


def _odd_inproj(x2d, w_bf16, conv_w, seq):
    n, d = x2d.shape
    cw = w_bf16.shape[1] // 3
    tm = _tile(seq, 512)
    tn = _tile(cw, 512)
    nj = cw // tn
    kern = functools.partial(_odd_inproj_kernel, seq=seq, tm=tm)
    return pl.pallas_call(
        kern,
        grid=(n // tm, nj),
        in_specs=[pl.BlockSpec((tm, d), lambda i, j: (i, 0)),
                  pl.BlockSpec((d, tn), lambda i, j: (0, j)),
                  pl.BlockSpec((d, tn), lambda i, j: (0, nj + j)),
                  pl.BlockSpec((d, tn), lambda i, j: (0, 2 * nj + j)),
                  pl.BlockSpec((CONV_K, tn), lambda i, j: (0, j))],
        out_specs=pl.BlockSpec((tm, tn), lambda i, j: (i, j)),
        out_shape=jax.ShapeDtypeStruct((n, cw), BF16),
        scratch_shapes=[pltpu.VMEM((tm, d), BF16),
                        pltpu.VMEM((CONV_HALO + tm, tn), F32),
                        pltpu.VMEM((nj, CONV_HALO, tn), F32)],
        compiler_params=_params("arbitrary", "arbitrary"),
        name="odd_inproj",
    )(x2d, w_bf16, w_bf16, w_bf16, conv_w)


def _route_kernel(lt_ref, d1_ref, d2_ref, g1_ref, g2_ref, te_ref, nact_ref, rank_ref, *, tm):
    e, n = lt_ref.shape
    lt = lt_ref[...]
    eidx = lax.broadcasted_iota(jnp.int32, (e, n), 0)
    v1 = jnp.max(lt, axis=0, keepdims=True)
    i1 = jnp.min(jnp.where(lt == v1, eidx, e), axis=0, keepdims=True)
    oh1 = eidx == i1
    lt2 = jnp.where(oh1, -jnp.inf, lt)
    v2 = jnp.max(lt2, axis=0, keepdims=True)
    i2 = jnp.min(jnp.where(lt2 == v2, eidx, e), axis=0, keepdims=True)
    oh2 = eidx == i2
    ex = jnp.exp(v2 - v1)
    g1_ref[...] = 1.0 / (1.0 + ex)
    g2_ref[...] = ex / (1.0 + ex)

    sel = jnp.where(oh1 | oh2, 1.0, 0.0)
    a = lax.broadcasted_iota(jnp.int32, (CUMSUM_CHUNK, CUMSUM_CHUNK), 0)
    b = lax.broadcasted_iota(jnp.int32, (CUMSUM_CHUNK, CUMSUM_CHUNK), 1)
    upper = jnp.where(a <= b, 1.0, 0.0).astype(BF16)
    running = jnp.zeros((e, 1), F32)
    for c in range(n // CUMSUM_CHUNK):
        cols = slice(c * CUMSUM_CHUNK, (c + 1) * CUMSUM_CHUNK)
        blk = sel[:, cols]
        incl = jnp.dot(blk.astype(BF16), upper, preferred_element_type=F32)
        rank_ref[:, cols] = incl - blk + running
        running = running + incl[:, CUMSUM_CHUNK - 1:CUMSUM_CHUNK]

    log2_tm = tm.bit_length() - 1
    assert tm == 1 << log2_tm
    counts = running.astype(jnp.int32)
    padded = ((counts + (tm - 1)) >> log2_tm) << log2_tm
    e8 = lax.broadcasted_iota(jnp.int32, (e, 1), 0)
    offs = jnp.zeros((e, 1), jnp.int32)
    total = jnp.zeros((1, 1), jnp.int32)
    for k in range(e):
        offs = jnp.where(e8 == k, total, offs)
        total = total + padded[k:k + 1, :]
    ends = offs + padded

    dest = rank_ref[...].astype(jnp.int32) + offs
    d1_ref[...] = jnp.sum(jnp.where(oh1, dest, 0), axis=0, keepdims=True)
    d2_ref[...] = jnp.sum(jnp.where(oh2, dest, 0), axis=0, keepdims=True)

    t = te_ref.shape[1]
    starts = lax.broadcasted_iota(jnp.int32, (e, t), 1) * tm
    te = jnp.sum(jnp.where(ends <= starts, 1, 0), axis=0, keepdims=True)
    te_ref[...] = jnp.minimum(te, e - 1)
    nact_ref[...] = total >> log2_tm


def _route(logits_t, tm, n_tiles):
    e, n = logits_t.shape
    assert n % CUMSUM_CHUNK == 0
    kern = functools.partial(_route_kernel, tm=tm)
    row_i = jax.ShapeDtypeStruct((1, n), jnp.int32)
    row_f = jax.ShapeDtypeStruct((1, n), F32)
    return pl.pallas_call(
        kern,
        out_shape=[row_i, row_i, row_f, row_f,
                   jax.ShapeDtypeStruct((1, n_tiles), jnp.int32),
                   jax.ShapeDtypeStruct((1, 1), jnp.int32)],
        scratch_shapes=[pltpu.VMEM((e, n), F32)],
        compiler_params=pltpu.CompilerParams(vmem_limit_bytes=VMEM_LIMIT_BYTES),
        name="route",
    )(logits_t)


def _dispatch_kernel(d1_ref, d2_ref, x_ref, xs_in_ref, xs_ref, sem, *, td):
    del xs_in_ref

    def row_copy(r, dst_row):
        return pltpu.make_async_copy(x_ref.at[pl.ds(r, 1), :], xs_ref.at[pl.ds(dst_row, 1), :], sem)

    def issue(r, carry):
        row_copy(r, d1_ref[0, r]).start()
        row_copy(r, d2_ref[0, r]).start()
        return carry

    lax.fori_loop(0, td, issue, 0)

    def drain(r, carry):
        row_copy(r, 0).wait()
        row_copy(r, 0).wait()
        return carry

    lax.fori_loop(0, td, drain, 0)


def _dispatch(x2d, d1, d2, rows_padded):
    n, d = x2d.shape
    td = _tile(n, 256)
    nb = n // td
    xs0 = jnp.zeros((rows_padded, d), F32)
    kern = functools.partial(_dispatch_kernel, td=td)
    smem_idx = lambda: pl.BlockSpec((None, 1, td), lambda i: (i, 0, 0), memory_space=pltpu.SMEM)
    return pl.pallas_call(
        kern,
        grid=(nb,),
        in_specs=[smem_idx(), smem_idx(),
                  pl.BlockSpec((td, d), lambda i: (i, 0)),
                  pl.BlockSpec(memory_space=pl.ANY)],
        out_specs=pl.BlockSpec(memory_space=pl.ANY),
        out_shape=jax.ShapeDtypeStruct((rows_padded, d), F32),
        scratch_shapes=[pltpu.SemaphoreType.DMA],
        input_output_aliases={3: 0},
        compiler_params=_params("arbitrary"),
        name="dispatch",
    )(d1.reshape(nb, 1, td), d2.reshape(nb, 1, td), x2d, xs0)


def _moe_kernel(te_ref, nact_ref, xs_ref, wg_ref, wu_ref, wd_ref, y_ref, xb_ref):
    i = pl.program_id(0)
    j = pl.program_id(1)
    active = i < nact_ref[0]

    @pl.when(j == 0)
    def _():
        y_ref[...] = jnp.zeros_like(y_ref)

    @pl.when(active & (j == 0))
    def _():
        xb_ref[...] = xs_ref[...].astype(BF16)

    @pl.when(active)
    def _():
        y_ref[...] += _swiglu_block(xb_ref[...], wg_ref[...], wu_ref[...], wd_ref[...])


def _moe_ffn(xs, te, nact, wg, wu, wd, tm):
    rows, d = xs.shape
    n_exp, _, ff = wg.shape
    tf = _tile(ff, 512)
    nj = ff // tf
    n_tiles = rows // tm

    def row_blk(i, j, te_ref, nact_ref):
        return jnp.minimum(i, nact_ref[0] - 1)

    def ff_blk(i, j, te_ref, nact_ref):
        return jnp.where(i < nact_ref[0], j, nj - 1)

    def expert(i, j, te_ref, nact_ref):
        return te_ref[row_blk(i, j, te_ref, nact_ref)]

    grid_spec = pltpu.PrefetchScalarGridSpec(
        num_scalar_prefetch=2,
        grid=(n_tiles, nj),
        in_specs=[pl.BlockSpec((tm, d), lambda i, j, t, a: (row_blk(i, j, t, a), 0)),
                  pl.BlockSpec((None, d, tf), lambda i, j, t, a: (expert(i, j, t, a), 0, ff_blk(i, j, t, a))),
                  pl.BlockSpec((None, d, tf), lambda i, j, t, a: (expert(i, j, t, a), 0, ff_blk(i, j, t, a))),
                  pl.BlockSpec((None, tf, d), lambda i, j, t, a: (expert(i, j, t, a), ff_blk(i, j, t, a), 0))],
        out_specs=pl.BlockSpec((tm, d), lambda i, j, t, a: (i, 0)),
        scratch_shapes=[pltpu.VMEM((tm, d), BF16)],
    )
    return pl.pallas_call(
        _moe_kernel,
        grid_spec=grid_spec,
        out_shape=jax.ShapeDtypeStruct((rows, d), F32),
        compiler_params=_params("arbitrary", "arbitrary"),
        name="moe_ffn",
    )(te, nact, xs, wg, wu, wd)


def _combine_kernel(d1_ref, d2_ref, x_ref, g1_ref, g2_ref, lg_ref, lb_ref, y_ref, o_ref,
                    buf1_ref, buf2_ref, sem, *, tc):
    def row_copy(src_row, buf_ref, r):
        return pltpu.make_async_copy(y_ref.at[pl.ds(src_row, 1), :], buf_ref.at[pl.ds(r, 1), :], sem)

    def issue(r, carry):
        row_copy(d1_ref[0, r], buf1_ref, r).start()
        row_copy(d2_ref[0, r], buf2_ref, r).start()
        return carry

    lax.fori_loop(0, tc, issue, 0)

    def drain(r, carry):
        row_copy(0, buf1_ref, r).wait()
        row_copy(0, buf2_ref, r).wait()
        return carry

    lax.fori_loop(0, tc, drain, 0)

    f = g1_ref[...] * buf1_ref[...] + g2_ref[...] * buf2_ref[...]
    o_ref[...] = _layer_norm(ALPHA * x_ref[...] + f, lg_ref[...], lb_ref[...])


def _combine_ln(x2d, y, d1, d2, g1, g2, lg, lb):
    n, d = x2d.shape
    tc = _tile(n, 256)
    nb = n // tc
    kern = functools.partial(_combine_kernel, tc=tc)
    smem_idx = lambda: pl.BlockSpec((None, 1, tc), lambda i: (i, 0, 0), memory_space=pltpu.SMEM)
    return pl.pallas_call(
        kern,
        grid=(nb,),
        in_specs=[smem_idx(), smem_idx(),
                  pl.BlockSpec((tc, d), lambda i: (i, 0)),
                  pl.BlockSpec((tc, 1), lambda i: (i, 0)),
                  pl.BlockSpec((tc, 1), lambda i: (i, 0)),
                  pl.BlockSpec((1, d), lambda i: (0, 0)),
                  pl.BlockSpec((1, d), lambda i: (0, 0)),
                  pl.BlockSpec(memory_space=pl.ANY)],
        out_specs=pl.BlockSpec((tc, d), lambda i: (i, 0)),
        out_shape=jax.ShapeDtypeStruct((n, d), F32),
        scratch_shapes=[pltpu.VMEM((tc, d), F32), pltpu.VMEM((tc, d), F32), pltpu.SemaphoreType.DMA],
        compiler_params=_params("arbitrary"),
        name="combine_ln",
    )(d1.reshape(nb, 1, tc), d2.reshape(nb, 1, tc), x2d, g1.reshape(n, 1), g2.reshape(n, 1), lg, lb, y)


def kernel(x, ev_w_in, ev_w_pool, ev_pool_scale, ev_lam_q1, ev_lam_k1, ev_lam_q2, ev_lam_k2, ev_subln_g, ev_w_out, ev_ln1_g, ev_ln1_b, ev_ffn_wg, ev_ffn_wu, ev_ffn_wd, ev_ln2_g, ev_ln2_b, od_w_in, od_conv_w, od_w_out, od_ln1_g, od_ln1_b, od_router, od_exp_wg, od_exp_wu, od_exp_wd, od_ln2_g, od_ln2_b):
    batch, seq, d = x.shape
    n = batch * seq
    assert ev_w_in.shape[0] == 1 and od_w_in.shape[0] == 1, "DEPTH == 2: one even and one odd layer"
    h = x.reshape(n, d)
    bf = lambda w: w.astype(BF16)
    row = lambda v: v.reshape(1, -1)

    pool_width = ev_w_pool.shape[1] * ev_w_pool.shape[2]
    u_pool, qkv = _even_inproj(h, bf(ev_w_in[0]), pool_width)
    a = _pool_mixer(u_pool, bf(ev_w_pool[0]), row(ev_pool_scale[0]), seq)
    o = _diff_attention(qkv, row(ev_lam_q1[0]), row(ev_lam_k1[0]), row(ev_lam_q2[0]), row(ev_lam_k2[0]),
                        row(ev_subln_g[0]), batch, seq)
    h = _outproj_ln([a, o], bf(ev_w_out[0]), h, row(ev_ln1_g[0]), row(ev_ln1_b[0]))
    h = _ffn_ln(h, bf(ev_ffn_wg[0]), bf(ev_ffn_wu[0]), bf(ev_ffn_wd[0]), row(ev_ln2_g[0]), row(ev_ln2_b[0]))

    gated = _odd_inproj(h, bf(od_w_in[0]), od_conv_w[0].reshape(CONV_K, -1), seq)
    n_exp = od_router.shape[-1]
    w_router = jnp.pad(od_router[0], ((0, 0), (0, ROUTER_LANES - n_exp)))
    h, logits = _outproj_ln([gated], bf(od_w_out[0]), h, row(od_ln1_g[0]), row(od_ln1_b[0]), w_router)

    tm = _tile(n, 512)
    n_tiles = (TOP_K * n) // tm + n_exp
    d1, d2, g1, g2, te, nact = _route(logits[:, :n_exp].T, tm, n_tiles)
    xs = _dispatch(h, d1, d2, n_tiles * tm)
    y = _moe_ffn(xs, te.reshape(-1), nact.reshape(-1), bf(od_exp_wg[0]), bf(od_exp_wu[0]), bf(od_exp_wd[0]), tm)
    h = _combine_ln(h, y, d1, d2, g1, g2, row(od_ln2_g[0]), row(od_ln2_b[0]))
    return h.reshape(batch, seq, d)
```

```python
import functools
import math

import jax
import jax.numpy as jnp
from jax import lax
from jax.experimental import pallas as pl
from jax.experimental.pallas import tpu as pltpu

F32 = jnp.float32
BF16 = jnp.bfloat16

DEPTH = 2
ALPHA = (2 * DEPTH) ** 0.25
LN_EPS = 1e-5
RMS_EPS = 1e-5
POOL_WINDOWS = (2, 4, 8, 16)
POOL_HALO = 16
DIFF_HEAD_DIM = 64
HEAD_WIDTH = 2 * DIFF_HEAD_DIM
LAM_INIT_LAYER0 = 0.8 - 0.6 * math.exp(-0.3 * 0)
CONV_K = 3
CONV_HALO = 8
TOP_K = 2
ROUTER_LANES = 128
CUMSUM_CHUNK = 256

VMEM_LIMIT_BYTES = 56 * 1024 * 1024


def _tile(n, pref):
    t = min(n, pref)
    assert n % t == 0, (n, t)
    return t


def _params(*sem):
    return pltpu.CompilerParams(dimension_semantics=sem, vmem_limit_bytes=VMEM_LIMIT_BYTES)


def _layer_norm(v, g, b):
    mu = jnp.mean(v, axis=-1, keepdims=True)
    d = v - mu
    var = jnp.mean(d * d, axis=-1, keepdims=True)
    return d * lax.rsqrt(var + LN_EPS) * g + b


def _even_inproj_kernel(x_ref, w_ref, pool_ref, qkv_ref, xb_ref):
    j = pl.program_id(1)

    @pl.when(j == 0)
    def _():
        xb_ref[...] = x_ref[...].astype(BF16)

    acc = jnp.dot(xb_ref[...], w_ref[...], preferred_element_type=F32)

    @pl.when(j == 0)
    def _():
        pool_ref[...] = acc

    @pl.when(j > 0)
    def _():
        qkv_ref[...] = acc.astype(BF16)


def _even_inproj(x2d, w_bf16, pool_width):
    n, d = x2d.shape
    width = w_bf16.shape[1]
    tn = pool_width
    assert (width - pool_width) % tn == 0
    tm = _tile(n, 1024)
    return pl.pallas_call(
        _even_inproj_kernel,
        grid=(n // tm, width // tn),
        in_specs=[pl.BlockSpec((tm, d), lambda i, j: (i, 0)),
                  pl.BlockSpec((d, tn), lambda i, j: (0, j))],
        out_specs=[pl.BlockSpec((tm, tn), lambda i, j: (i, 0)),
                   pl.BlockSpec((tm, tn), lambda i, j: (i, jnp.maximum(j - 1, 0)))],
        out_shape=[jax.ShapeDtypeStruct((n, pool_width), F32),
                   jax.ShapeDtypeStruct((n, width - pool_width), BF16)],
        scratch_shapes=[pltpu.VMEM((tm, d), BF16)],
        compiler_params=_params("arbitrary", "arbitrary"),
        name="even_inproj",
    )(x2d, w_bf16)


def _pool_kernel(u_ref, halo_ref, w_ref, scale_ref, o_ref, buf_ref, *, seq, tp, cg):
    i = pl.program_id(0)
    first = (i * tp) % seq == 0
    pos = (i * tp) % seq + lax.broadcasted_iota(jnp.int32, (tp, 1), 0)
    for g, window in enumerate(POOL_WINDOWS):
        cols = slice(g * cg, (g + 1) * cg)
        u = u_ref[:, cols]
        buf_ref[0:POOL_HALO, :] = jnp.where(first, 0.0, halo_ref[:, cols])
        buf_ref[POOL_HALO:, :] = u
        total = u
        for back in range(1, window):
            total = total + buf_ref[pl.ds(POOL_HALO - back, tp), :]
        count = jnp.minimum(pos + 1, window).astype(F32)
        pooled = total / count - u
        mixed = jnp.dot(pooled.astype(BF16), w_ref[g], preferred_element_type=F32)
        o_ref[:, cols] = (mixed * scale_ref[:, cols]).astype(BF16)


def _pool_mixer(u, w_pool_bf16, scale, seq):
    n, pw = u.shape
    groups, cg, _ = w_pool_bf16.shape
    assert groups == len(POOL_WINDOWS) and groups * cg == pw
    tp = _tile(seq, 512)
    halo_blocks = tp // POOL_HALO
    kern = functools.partial(_pool_kernel, seq=seq, tp=tp, cg=cg)
    return pl.pallas_call(
        kern,
        grid=(n // tp,),
        in_specs=[pl.BlockSpec((tp, pw), lambda i: (i, 0)),
                  pl.BlockSpec((POOL_HALO, pw), lambda i: (jnp.maximum(i * halo_blocks - 1, 0), 0)),
                  pl.BlockSpec((groups, cg, cg), lambda i: (0, 0, 0)),
                  pl.BlockSpec((1, pw), lambda i: (0, 0))],
        out_specs=pl.BlockSpec((tp, pw), lambda i: (i, 0)),
        out_shape=jax.ShapeDtypeStruct((n, pw), BF16),
        scratch_shapes=[pltpu.VMEM((POOL_HALO + tp, cg), F32)],
        compiler_params=_params("arbitrary"),
        name="pool_mixer",
    )(u, u, w_pool_bf16, scale)


def _diff_attn_kernel(q_ref, k_ref, v_ref, lq1_ref, lk1_ref, lq2_ref, lk2_ref, g_ref, o_ref,
                      qs_ref, m_ref, acc_ref, *, tq):
    qi = pl.program_id(2)
    q = q_ref[...]
    lane = lax.broadcasted_iota(jnp.int32, q.shape, 1)
    zero = jnp.zeros_like(q)
    scale = jnp.asarray(DIFF_HEAD_DIM ** -0.5, BF16)
    qs_ref[0] = jnp.where(lane < DIFF_HEAD_DIM, q, zero) * scale
    qs_ref[1] = jnp.where(lane >= DIFF_HEAD_DIM, q, zero) * scale
    m_ref[...] = jnp.full(m_ref.shape, -jnp.inf, F32)
    acc_ref[...] = jnp.zeros(acc_ref.shape, F32)
    ones = jnp.ones((tq, HEAD_WIDTH), BF16)
    n_chunks = tq // HEAD_WIDTH

    def step(kb, masked):
        start = pl.multiple_of(kb * tq, tq)
        k = k_ref[pl.ds(start, tq), :]
        vx = jnp.concatenate([v_ref[pl.ds(start, tq), :], ones], axis=1)
        for h in range(2):
            s = lax.dot_general(qs_ref[h], k, (((1,), (1,)), ((), ())), preferred_element_type=F32)
            if masked:
                r = lax.broadcasted_iota(jnp.int32, (tq, tq), 0)
                c = lax.broadcasted_iota(jnp.int32, (tq, tq), 1)
                s = jnp.where(c <= r, s, -jnp.inf)
            chunks = [s[:, i * HEAD_WIDTH:(i + 1) * HEAD_WIDTH] for i in range(n_chunks)]
            m_blk = functools.reduce(jnp.maximum, chunks)
            m_old = m_ref[h]
            m_new = jnp.maximum(m_old, jnp.max(m_blk, axis=-1, keepdims=True))
            scale_old = jnp.exp(m_old - m_new)
            p = jnp.concatenate([jnp.exp(ch - m_new) for ch in chunks], axis=1).astype(BF16)
            pv = jnp.dot(p, vx, preferred_element_type=F32)
            acc_ref[h] = jnp.concatenate([scale_old, scale_old], axis=1) * acc_ref[h] + pv
            m_ref[h] = m_new

    def body(kb, carry):
        step(kb, False)
        return carry

    lax.fori_loop(0, qi, body, 0)
    step(qi, True)

    lam = (jnp.exp(jnp.sum(lq1_ref[...] * lk1_ref[...], keepdims=True))
           - jnp.exp(jnp.sum(lq2_ref[...] * lk2_ref[...], keepdims=True)) + LAM_INIT_LAYER0)
    o = (acc_ref[0, :, 0:HEAD_WIDTH] / acc_ref[0, :, HEAD_WIDTH:]
         - lam * (acc_ref[1, :, 0:HEAD_WIDTH] / acc_ref[1, :, HEAD_WIDTH:]))
    o = o * lax.rsqrt(jnp.mean(o * o, axis=-1, keepdims=True) + RMS_EPS)
    o_ref[...] = (o * g_ref[...] * (1.0 - LAM_INIT_LAYER0)).astype(BF16)


def _diff_attention(qkv, lq1, lk1, lq2, lk2, subln_g, batch, seq):
    n, w3 = qkv.shape
    width = w3 // 3
    heads = width // HEAD_WIDTH
    tq = _tile(seq, 512)
    nq = seq // tq
    kern = functools.partial(_diff_attn_kernel, tq=tq)
    vec = lambda: pl.BlockSpec((1, DIFF_HEAD_DIM), lambda b, h, i: (0, 0))
    return pl.pallas_call(
        kern,
        grid=(batch, heads, nq),
        in_specs=[pl.BlockSpec((tq, HEAD_WIDTH), lambda b, h, i: (b * nq + i, h)),
                  pl.BlockSpec((seq, HEAD_WIDTH), lambda b, h, i: (b, heads + h)),
                  pl.BlockSpec((seq, HEAD_WIDTH), lambda b, h, i: (b, 2 * heads + h)),
                  vec(), vec(), vec(), vec(),
                  pl.BlockSpec((1, HEAD_WIDTH), lambda b, h, i: (0, 0))],
        out_specs=pl.BlockSpec((tq, HEAD_WIDTH), lambda b, h, i: (b * nq + i, h)),
        out_shape=jax.ShapeDtypeStruct((n, width), BF16),
        scratch_shapes=[pltpu.VMEM((2, tq, HEAD_WIDTH), BF16),
                        pltpu.VMEM((2, tq, HEAD_WIDTH), F32),
                        pltpu.VMEM((2, tq, 2 * HEAD_WIDTH), F32)],
        compiler_params=_params("arbitrary", "arbitrary", "arbitrary"),
        name="diff_attention",
    )(qkv, qkv, qkv, lq1, lk1, lq2, lk2, subln_g)


def _outproj_ln_kernel(*refs, n_lhs, with_router):
    lhs_refs = refs[:n_lhs]
    w_ref, x_ref, g_ref, b_ref = refs[n_lhs:n_lhs + 4]
    rest = refs[n_lhs + 4:]
    y = None
    off = 0
    for a_ref in lhs_refs:
        kp = a_ref.shape[1]
        part = jnp.dot(a_ref[...], w_ref[off:off + kp, :], preferred_element_type=F32)
        y = part if y is None else y + part
        off += kp
    out = _layer_norm(ALPHA * x_ref[...] + y, g_ref[...], b_ref[...])
    if with_router:
        wr_hi_ref, wr_lo_ref, o_ref, logit_ref = rest
        o_ref[...] = out
        out_hi = out.astype(BF16)
        out_lo = (out - out_hi.astype(F32)).astype(BF16)
        logit_ref[...] = (jnp.dot(out_hi, wr_hi_ref[...], preferred_element_type=F32)
                          + (jnp.dot(out_lo, wr_hi_ref[...], preferred_element_type=F32)
                             + jnp.dot(out_hi, wr_lo_ref[...], preferred_element_type=F32)))
    else:
        (o_ref,) = rest
        o_ref[...] = out


def _outproj_ln(lhs_list, w_bf16, x2d, g, b, w_router_padded=None):
    n, d = x2d.shape
    tm = _tile(n, 512)
    with_router = w_router_padded is not None
    in_specs = [pl.BlockSpec((tm, a.shape[1]), lambda i: (i, 0)) for a in lhs_list]
    in_specs += [pl.BlockSpec(w_bf16.shape, lambda i: (0, 0)),
                 pl.BlockSpec((tm, d), lambda i: (i, 0)),
                 pl.BlockSpec((1, d), lambda i: (0, 0)),
                 pl.BlockSpec((1, d), lambda i: (0, 0))]
    out_specs = [pl.BlockSpec((tm, d), lambda i: (i, 0))]
    out_shape = [jax.ShapeDtypeStruct((n, d), F32)]
    args = list(lhs_list) + [w_bf16, x2d, g, b]
    if with_router:
        wr_hi = w_router_padded.astype(BF16)
        wr_lo = (w_router_padded - wr_hi.astype(F32)).astype(BF16)
        in_specs += [pl.BlockSpec(wr_hi.shape, lambda i: (0, 0))] * 2
        out_specs.append(pl.BlockSpec((tm, ROUTER_LANES), lambda i: (i, 0)))
        out_shape.append(jax.ShapeDtypeStruct((n, ROUTER_LANES), F32))
        args += [wr_hi, wr_lo]
    kern = functools.partial(_outproj_ln_kernel, n_lhs=len(lhs_list), with_router=with_router)
    res = pl.pallas_call(
        kern,
        grid=(n // tm,),
        in_specs=in_specs,
        out_specs=out_specs,
        out_shape=out_shape,
        compiler_params=_params("arbitrary"),
        name="outproj_ln_router" if with_router else "outproj_ln",
    )(*args)
    return res if with_router else res[0]


def _swiglu_block(xb, wg, wu, wd):
    gate = jnp.dot(xb, wg, preferred_element_type=F32)
    up = jnp.dot(xb, wu, preferred_element_type=F32)
    h = gate / (1.0 + jnp.exp(-gate)) * up
    return jnp.dot(h.astype(BF16), wd, preferred_element_type=F32)


def _ffn_ln_kernel(x_ref, wg_ref, wu_ref, wd_ref, g_ref, b_ref, o_ref, xb_ref, acc_ref):
    j = pl.program_id(1)

    @pl.when(j == 0)
    def _():
        xb_ref[...] = x_ref[...].astype(BF16)
        acc_ref[...] = jnp.zeros_like(acc_ref)

    acc_ref[...] += _swiglu_block(xb_ref[...], wg_ref[...], wu_ref[...], wd_ref[...])

    @pl.when(j == pl.num_programs(1) - 1)
    def _():
        o_ref[...] = _layer_norm(ALPHA * x_ref[...] + acc_ref[...], g_ref[...], b_ref[...])


def _ffn_ln(x2d, wg, wu, wd, g, b):
    n, d = x2d.shape
    ff = wg.shape[1]
    tm = _tile(n, 512)
    tf = _tile(ff, 512)
    return pl.pallas_call(
        _ffn_ln_kernel,
        grid=(n // tm, ff // tf),
        in_specs=[pl.BlockSpec((tm, d), lambda i, j: (i, 0)),
                  pl.BlockSpec((d, tf), lambda i, j: (0, j)),
                  pl.BlockSpec((d, tf), lambda i, j: (0, j)),
                  pl.BlockSpec((tf, d), lambda i, j: (j, 0)),
                  pl.BlockSpec((1, d), lambda i, j: (0, 0)),
                  pl.BlockSpec((1, d), lambda i, j: (0, 0))],
        out_specs=pl.BlockSpec((tm, d), lambda i, j: (i, 0)),
        out_shape=jax.ShapeDtypeStruct((n, d), F32),
        scratch_shapes=[pltpu.VMEM((tm, d), BF16), pltpu.VMEM((tm, d), F32)],
        compiler_params=_params("arbitrary", "arbitrary"),
        name="ffn_ln",
    )(x2d, wg, wu, wd, g, b)


def _odd_inproj_kernel(x_ref, wb_ref, wc_ref, wh_ref, cw_ref, o_ref, xb_ref, zs_ref, carry_ref,
                       *, seq, tm):
    i = pl.program_id(0)
    j = pl.program_id(1)

    @pl.when(j == 0)
    def _():
        xb_ref[...] = x_ref[...].astype(BF16)

    xb = xb_ref[...]
    b_gate = jnp.dot(xb, wb_ref[...], preferred_element_type=F32)
    c_gate = jnp.dot(xb, wc_ref[...], preferred_element_type=F32)
    hx = jnp.dot(xb, wh_ref[...], preferred_element_type=F32)
    z = c_gate * hx
    first = (i * tm) % seq == 0
    zs_ref[0:CONV_HALO, :] = jnp.where(first, 0.0, carry_ref[j])
    zs_ref[CONV_HALO:, :] = z
    carry_ref[j] = z[tm - CONV_HALO:, :]
    y = (cw_ref[0:1, :] * zs_ref[pl.ds(CONV_HALO - 2, tm), :]
         + cw_ref[1:2, :] * zs_ref[pl.ds(CONV_HALO - 1, tm), :]
         + cw_ref[2:3, :] * z)
    o_ref[...] = (b_gate * y).astype(BF16)


def _odd_inproj(x2d, w_bf16, conv_w, seq):
    n, d = x2d.shape
    cw = w_bf16.shape[1] // 3
    tm = _tile(seq, 512)
    tn = _tile(cw, 512)
    nj = cw // tn
    kern = functools.partial(_odd_inproj_kernel, seq=seq, tm=tm)
    return pl.pallas_call(
        kern,
        grid=(n // tm, nj),
        in_specs=[pl.BlockSpec((tm, d), lambda i, j: (i, 0)),
                  pl.BlockSpec((d, tn), lambda i, j: (0, j)),
                  pl.BlockSpec((d, tn), lambda i, j: (0, nj + j)),
                  pl.BlockSpec((d, tn), lambda i, j: (0, 2 * nj + j)),
                  pl.BlockSpec((CONV_K, tn), lambda i, j: (0, j))],
        out_specs=pl.BlockSpec((tm, tn), lambda i, j: (i, j)),
        out_shape=jax.ShapeDtypeStruct((n, cw), BF16),
        scratch_shapes=[pltpu.VMEM((tm, d), BF16),
                        pltpu.VMEM((CONV_HALO + tm, tn), F32),
                        pltpu.VMEM((nj, CONV_HALO, tn), F32)],
        compiler_params=_params("arbitrary", "arbitrary"),
        name="odd_inproj",
    )(x2d, w_bf16, w_bf16, w_bf16, conv_w)


def _route_kernel(lt_ref, d1_ref, d2_ref, g1_ref, g2_ref, te_ref, nact_ref, rank_ref, *, tm):
    e, n = lt_ref.shape
    lt = lt_ref[...]
    eidx = lax.broadcasted_iota(jnp.int32, (e, n), 0)
    v1 = jnp.max(lt, axis=0, keepdims=True)
    i1 = jnp.min(jnp.where(lt == v1, eidx, e), axis=0, keepdims=True)
    oh1 = eidx == i1
    lt2 = jnp.where(oh1, -jnp.inf, lt)
    v2 = jnp.max(lt2, axis=0, keepdims=True)
    i2 = jnp.min(jnp.where(lt2 == v2, eidx, e), axis=0, keepdims=True)
    oh2 = eidx == i2
    ex = jnp.exp(v2 - v1)
    g1_ref[...] = 1.0 / (1.0 + ex)
    g2_ref[...] = ex / (1.0 + ex)

    sel = jnp.where(oh1 | oh2, 1.0, 0.0)
    a = lax.broadcasted_iota(jnp.int32, (CUMSUM_CHUNK, CUMSUM_CHUNK), 0)
    b = lax.broadcasted_iota(jnp.int32, (CUMSUM_CHUNK, CUMSUM_CHUNK), 1)
    upper = jnp.where(a <= b, 1.0, 0.0).astype(BF16)
    running = jnp.zeros((e, 1), F32)
    for c in range(n // CUMSUM_CHUNK):
        cols = slice(c * CUMSUM_CHUNK, (c + 1) * CUMSUM_CHUNK)
        blk = sel[:, cols]
        incl = jnp.dot(blk.astype(BF16), upper, preferred_element_type=F32)
        rank_ref[:, cols] = incl - blk + running
        running = running + incl[:, CUMSUM_CHUNK - 1:CUMSUM_CHUNK]

    log2_tm = tm.bit_length() - 1
    assert tm == 1 << log2_tm
    counts = running.astype(jnp.int32)
    padded = ((counts + (tm - 1)) >> log2_tm) << log2_tm
    e8 = lax.broadcasted_iota(jnp.int32, (e, 1), 0)
    offs = jnp.zeros((e, 1), jnp.int32)
    total = jnp.zeros((1, 1), jnp.int32)
    for k in range(e):
        offs = jnp.where(e8 == k, total, offs)
        total = total + padded[k:k + 1, :]
    ends = offs + padded

    dest = rank_ref[...].astype(jnp.int32) + offs
    d1_ref[...] = jnp.sum(jnp.where(oh1, dest, 0), axis=0, keepdims=True)
    d2_ref[...] = jnp.sum(jnp.where(oh2, dest, 0), axis=0, keepdims=True)

    t = te_ref.shape[1]
    starts = lax.broadcasted_iota(jnp.int32, (e, t), 1) * tm
    te = jnp.sum(jnp.where(ends <= starts, 1, 0), axis=0, keepdims=True)
    te_ref[...] = jnp.minimum(te, e - 1)
    nact_ref[...] = total >> log2_tm


def _route(logits_t, tm, n_tiles):
    e, n = logits_t.shape
    assert n % CUMSUM_CHUNK == 0
    kern = functools.partial(_route_kernel, tm=tm)
    row_i = jax.ShapeDtypeStruct((1, n), jnp.int32)
    row_f = jax.ShapeDtypeStruct((1, n), F32)
    return pl.pallas_call(
        kern,
        out_shape=[row_i, row_i, row_f, row_f,
                   jax.ShapeDtypeStruct((1, n_tiles), jnp.int32),
                   jax.ShapeDtypeStruct((1, 1), jnp.int32)],
        scratch_shapes=[pltpu.VMEM((e, n), F32)],
        compiler_params=pltpu.CompilerParams(vmem_limit_bytes=VMEM_LIMIT_BYTES),
        name="route",
    )(logits_t)


def _dispatch_kernel(d1_ref, d2_ref, x_ref, xs_in_ref, xs_ref, sem, *, td):
    del xs_in_ref

    def row_copy(r, dst_row):
        return pltpu.make_async_copy(x_ref.at[pl.ds(r, 1), :], xs_ref.at[pl.ds(dst_row, 1), :], sem)

    def issue(r, carry):
        row_copy(r, d1_ref[0, r]).start()
        row_copy(r, d2_ref[0, r]).start()
        return carry

    lax.fori_loop(0, td, issue, 0)
    for _ in range(TOP_K):
        pltpu.make_async_copy(x_ref, xs_ref.at[pl.ds(0, td), :], sem).wait()


def _dispatch(x2d, d1, d2, rows_padded):
    n, d = x2d.shape
    td = _tile(n, 256)
    nb = n // td
    xs0 = jnp.zeros((rows_padded, d), F32)
    kern = functools.partial(_dispatch_kernel, td=td)
    smem_idx = lambda: pl.BlockSpec((None, 1, td), lambda i: (i, 0, 0), memory_space=pltpu.SMEM)
    return pl.pallas_call(
        kern,
        grid=(nb,),
        in_specs=[smem_idx(), smem_idx(),
                  pl.BlockSpec((td, d), lambda i: (i, 0)),
                  pl.BlockSpec(memory_space=pl.ANY)],
        out_specs=pl.BlockSpec(memory_space=pl.ANY),
        out_shape=jax.ShapeDtypeStruct((rows_padded, d), F32),
        scratch_shapes=[pltpu.SemaphoreType.DMA],
        input_output_aliases={3: 0},
        compiler_params=_params("arbitrary"),
        name="dispatch",
    )(d1.reshape(nb, 1, td), d2.reshape(nb, 1, td), x2d, xs0)


def _moe_kernel(te_ref, nact_ref, xs_ref, wg_ref, wu_ref, wd_ref, y_ref, xb_ref):
    i = pl.program_id(0)
    j = pl.program_id(1)
    active = i < nact_ref[0]

    @pl.when(j == 0)
    def _():
        y_ref[...] = jnp.zeros_like(y_ref)

    @pl.when(active & (j == 0))
    def _():
        xb_ref[...] = xs_ref[...].astype(BF16)

    @pl.when(active)
    def _():
        y_ref[...] += _swiglu_block(xb_ref[...], wg_ref[...], wu_ref[...], wd_ref[...])


def _moe_ffn(xs, te, nact, wg, wu, wd, tm):
    rows, d = xs.shape
    n_exp, _, ff = wg.shape
    tf = _tile(ff, 512)
    nj = ff // tf
    n_tiles = rows // tm

    def row_blk(i, j, te_ref, nact_ref):
        return jnp.minimum(i, nact_ref[0] - 1)

    def ff_blk(i, j, te_ref, nact_ref):
        return jnp.where(i < nact_ref[0], j, nj - 1)

    def expert(i, j, te_ref, nact_ref):
        return te_ref[row_blk(i, j, te_ref, nact_ref)]

    grid_spec = pltpu.PrefetchScalarGridSpec(
        num_scalar_prefetch=2,
        grid=(n_tiles, nj),
        in_specs=[pl.BlockSpec((tm, d), lambda i, j, t, a: (row_blk(i, j, t, a), 0)),
                  pl.BlockSpec((None, d, tf), lambda i, j, t, a: (expert(i, j, t, a), 0, ff_blk(i, j, t, a))),
                  pl.BlockSpec((None, d, tf), lambda i, j, t, a: (expert(i, j, t, a), 0, ff_blk(i, j, t, a))),
                  pl.BlockSpec((None, tf, d), lambda i, j, t, a: (expert(i, j, t, a), ff_blk(i, j, t, a), 0))],
        out_specs=pl.BlockSpec((tm, d), lambda i, j, t, a: (i, 0)),
        scratch_shapes=[pltpu.VMEM((tm, d), BF16)],
    )
    return pl.pallas_call(
        _moe_kernel,
        grid_spec=grid_spec,
        out_shape=jax.ShapeDtypeStruct((rows, d), F32),
        compiler_params=_params("arbitrary", "arbitrary"),
        name="moe_ffn",
    )(te, nact, xs, wg, wu, wd)


def _combine_kernel(d1_ref, d2_ref, x_ref, g1_ref, g2_ref, lg_ref, lb_ref, y_ref, o_ref,
                    buf1_ref, buf2_ref, sem, *, tc):
    def row_copy(src_row, buf_ref, r):
        return pltpu.make_async_copy(y_ref.at[pl.ds(src_row, 1), :], buf_ref.at[pl.ds(r, 1), :], sem)

    def issue(r, carry):
        row_copy(d1_ref[0, r], buf1_ref, r).start()
        row_copy(d2_ref[0, r], buf2_ref, r).start()
        return carry

    lax.fori_loop(0, tc, issue, 0)
    for buf_ref in (buf1_ref, buf2_ref):
        pltpu.make_async_copy(y_ref.at[pl.ds(0, tc), :], buf_ref, sem).wait()

    f = g1_ref[...] * buf1_ref[...] + g2_ref[...] * buf2_ref[...]
    o_ref[...] = _layer_norm(ALPHA * x_ref[...] + f, lg_ref[...], lb_ref[...])


def _combine_ln(x2d, y, d1, d2, g1, g2, lg, lb):
    n, d = x2d.shape
    tc = _tile(n, 256)
    nb = n // tc
    kern = functools.partial(_combine_kernel, tc=tc)
    smem_idx = lambda: pl.BlockSpec((None, 1, tc), lambda i: (i, 0, 0), memory_space=pltpu.SMEM)
    return pl.pallas_call(
        kern,
        grid=(nb,),
        in_specs=[smem_idx(), smem_idx(),
                  pl.BlockSpec((tc, d), lambda i: (i, 0)),
                  pl.BlockSpec((tc, 1), lambda i: (i, 0)),
                  pl.BlockSpec((tc, 1), lambda i: (i, 0)),
                  pl.BlockSpec((1, d), lambda i: (0, 0)),
                  pl.BlockSpec((1, d), lambda i: (0, 0)),
                  pl.BlockSpec(memory_space=pl.ANY)],
        out_specs=pl.BlockSpec((tc, d), lambda i: (i, 0)),
        out_shape=jax.ShapeDtypeStruct((n, d), F32),
        scratch_shapes=[pltpu.VMEM((tc, d), F32), pltpu.VMEM((tc, d), F32), pltpu.SemaphoreType.DMA],
        compiler_params=_params("arbitrary"),
        name="combine_ln",
    )(d1.reshape(nb, 1, tc), d2.reshape(nb, 1, tc), x2d, g1.reshape(n, 1), g2.reshape(n, 1), lg, lb, y)


def kernel(x, ev_w_in, ev_w_pool, ev_pool_scale, ev_lam_q1, ev_lam_k1, ev_lam_q2, ev_lam_k2, ev_subln_g, ev_w_out, ev_ln1_g, ev_ln1_b, ev_ffn_wg, ev_ffn_wu, ev_ffn_wd, ev_ln2_g, ev_ln2_b, od_w_in, od_conv_w, od_w_out, od_ln1_g, od_ln1_b, od_router, od_exp_wg, od_exp_wu, od_exp_wd, od_ln2_g, od_ln2_b):
    batch, seq, d = x.shape
    n = batch * seq
    assert ev_w_in.shape[0] == 1 and od_w_in.shape[0] == 1, "DEPTH == 2: one even and one odd layer"
    h = x.reshape(n, d)
    bf = lambda w: w.astype(BF16)
    row = lambda v: v.reshape(1, -1)

    pool_width = ev_w_pool.shape[1] * ev_w_pool.shape[2]
    u_pool, qkv = _even_inproj(h, bf(ev_w_in[0]), pool_width)
    a = _pool_mixer(u_pool, bf(ev_w_pool[0]), row(ev_pool_scale[0]), seq)
    o = _diff_attention(qkv, row(ev_lam_q1[0]), row(ev_lam_k1[0]), row(ev_lam_q2[0]), row(ev_lam_k2[0]),
                        row(ev_subln_g[0]), batch, seq)
    h = _outproj_ln([a, o], bf(ev_w_out[0]), h, row(ev_ln1_g[0]), row(ev_ln1_b[0]))
    h = _ffn_ln(h, bf(ev_ffn_wg[0]), bf(ev_ffn_wu[0]), bf(ev_ffn_wd[0]), row(ev_ln2_g[0]), row(ev_ln2_b[0]))

    gated = _odd_inproj(h, bf(od_w_in[0]), od_conv_w[0].reshape(CONV_K, -1), seq)
    n_exp = od_router.shape[-1]
    w_router = jnp.pad(od_router[0], ((0, 0), (0, ROUTER_LANES - n_exp)))
    h, logits = _outproj_ln([gated], bf(od_w_out[0]), h, row(od_ln1_g[0]), row(od_ln1_b[0]), w_router)

    tm = _tile(n, 512)
    n_tiles = (TOP_K * n) // tm + n_exp
    d1, d2, g1, g2, te, nact = _route(logits[:, :n_exp].T, tm, n_tiles)
    xs = _dispatch(h, d1, d2, n_tiles * tm)
    y = _moe_ffn(xs, te.reshape(-1), nact.reshape(-1), bf(od_exp_wg[0]), bf(od_exp_wu[0]), bf(od_exp_wd[0]), tm)
    h = _combine_ln(h, y, d1, d2, g1, g2, row(od_ln2_g[0]), row(od_ln2_b[0]))
    return h.reshape(batch, seq, d)
```

```python
import functools
import math

import jax
import jax.numpy as jnp
from jax import lax
from jax.experimental import pallas as pl
from jax.experimental.pallas import tpu as pltpu

F32 = jnp.float32
BF16 = jnp.bfloat16

DEPTH = 2
ALPHA = (2 * DEPTH) ** 0.25
LN_EPS = 1e-5
RMS_EPS = 1e-5
POOL_WINDOWS = (2, 4, 8, 16)
POOL_HALO = 16
DIFF_HEAD_DIM = 64
HEAD_WIDTH = 2 * DIFF_HEAD_DIM
LAM_INIT_LAYER0 = 0.8 - 0.6 * math.exp(-0.3 * 0)
CONV_K = 3
CONV_HALO = 8
TOP_K = 2
ROUTER_LANES = 128
CUMSUM_CHUNK = 256

VMEM_LIMIT_BYTES = 56 * 1024 * 1024


def _tile(n, pref):
    t = min(n, pref)
    assert n % t == 0, (n, t)
    return t


def _params(*sem):
    return pltpu.CompilerParams(dimension_semantics=sem, vmem_limit_bytes=VMEM_LIMIT_BYTES)


def _layer_norm(v, g, b):
    mu = jnp.mean(v, axis=-1, keepdims=True)
    d = v - mu
    var = jnp.mean(d * d, axis=-1, keepdims=True)
    return d * lax.rsqrt(var + LN_EPS) * g + b


def _even_inproj_kernel(x_ref, w_ref, pool_ref, qkv_ref, xb_ref):
    j = pl.program_id(1)

    @pl.when(j == 0)
    def _():
        xb_ref[...] = x_ref[...].astype(BF16)

    acc = jnp.dot(xb_ref[...], w_ref[...], preferred_element_type=F32)

    @pl.when(j == 0)
    def _():
        pool_ref[...] = acc

    @pl.when(j > 0)
    def _():
        qkv_ref[...] = acc.astype(BF16)


def _even_inproj(x2d, w_bf16, pool_width):
    n, d = x2d.shape
    width = w_bf16.shape[1]
    tn = pool_width
    assert (width - pool_width) % tn == 0
    tm = _tile(n, 1024)
    return pl.pallas_call(
        _even_inproj_kernel,
        grid=(n // tm, width // tn),
        in_specs=[pl.BlockSpec((tm, d), lambda i, j: (i, 0)),
                  pl.BlockSpec((d, tn), lambda i, j: (0, j))],
        out_specs=[pl.BlockSpec((tm, tn), lambda i, j: (i, 0)),
                   pl.BlockSpec((tm, tn), lambda i, j: (i, jnp.maximum(j - 1, 0)))],
        out_shape=[jax.ShapeDtypeStruct((n, pool_width), F32),
                   jax.ShapeDtypeStruct((n, width - pool_width), BF16)],
        scratch_shapes=[pltpu.VMEM((tm, d), BF16)],
        compiler_params=_params("arbitrary", "arbitrary"),
        name="even_inproj",
    )(x2d, w_bf16)


def _pool_kernel(u_ref, halo_ref, w_ref, scale_ref, o_ref, buf_ref, *, seq, tp, cg):
    i = pl.program_id(0)
    first = (i * tp) % seq == 0
    pos = (i * tp) % seq + lax.broadcasted_iota(jnp.int32, (tp, 1), 0)
    for g, window in enumerate(POOL_WINDOWS):
        cols = slice(g * cg, (g + 1) * cg)
        u = u_ref[:, cols]
        buf_ref[0:POOL_HALO, :] = jnp.where(first, 0.0, halo_ref[:, cols])
        buf_ref[POOL_HALO:, :] = u
        total = u
        for back in range(1, window):
            total = total + buf_ref[pl.ds(POOL_HALO - back, tp), :]
        count = jnp.minimum(pos + 1, window).astype(F32)
        pooled = total / count - u
        mixed = jnp.dot(pooled.astype(BF16), w_ref[g], preferred_element_type=F32)
        o_ref[:, cols] = (mixed * scale_ref[:, cols]).astype(BF16)


def _pool_mixer(u, w_pool_bf16, scale, seq):
    n, pw = u.shape
    groups, cg, _ = w_pool_bf16.shape
    assert groups == len(POOL_WINDOWS) and groups * cg == pw
    tp = _tile(seq, 512)
    halo_blocks = tp // POOL_HALO
    kern = functools.partial(_pool_kernel, seq=seq, tp=tp, cg=cg)
    return pl.pallas_call(
        kern,
        grid=(n // tp,),
        in_specs=[pl.BlockSpec((tp, pw), lambda i: (i, 0)),
                  pl.BlockSpec((POOL_HALO, pw), lambda i: (jnp.maximum(i * halo_blocks - 1, 0), 0)),
                  pl.BlockSpec((groups, cg, cg), lambda i: (0, 0, 0)),
                  pl.BlockSpec((1, pw), lambda i: (0, 0))],
        out_specs=pl.BlockSpec((tp, pw), lambda i: (i, 0)),
        out_shape=jax.ShapeDtypeStruct((n, pw), BF16),
        scratch_shapes=[pltpu.VMEM((POOL_HALO + tp, cg), F32)],
        compiler_params=_params("arbitrary"),
        name="pool_mixer",
    )(u, u, w_pool_bf16, scale)


def _diff_attn_kernel(q_ref, k_ref, v_ref, lq1_ref, lk1_ref, lq2_ref, lk2_ref, g_ref, o_ref,
                      qs_ref, m_ref, acc_ref, *, tq):
    qi = pl.program_id(2)
    q = q_ref[...]
    lane = lax.broadcasted_iota(jnp.int32, q.shape, 1)
    zero = jnp.zeros_like(q)
    scale = jnp.asarray(DIFF_HEAD_DIM ** -0.5, BF16)
    qs_ref[0] = jnp.where(lane < DIFF_HEAD_DIM, q, zero) * scale
    qs_ref[1] = jnp.where(lane >= DIFF_HEAD_DIM, q, zero) * scale
    m_ref[...] = jnp.full(m_ref.shape, -jnp.inf, F32)
    acc_ref[...] = jnp.zeros(acc_ref.shape, F32)
    ones = jnp.ones((tq, HEAD_WIDTH), BF16)
    n_chunks = tq // HEAD_WIDTH

    def step(kb, masked):
        start = pl.multiple_of(kb * tq, tq)
        k = k_ref[pl.ds(start, tq), :]
        vx = jnp.concatenate([v_ref[pl.ds(start, tq), :], ones], axis=1)
        for h in range(2):
            s = lax.dot_general(qs_ref[h], k, (((1,), (1,)), ((), ())), preferred_element_type=F32)
            if masked:
                r = lax.broadcasted_iota(jnp.int32, (tq, tq), 0)
                c = lax.broadcasted_iota(jnp.int32, (tq, tq), 1)
                s = jnp.where(c <= r, s, -jnp.inf)
            chunks = [s[:, i * HEAD_WIDTH:(i + 1) * HEAD_WIDTH] for i in range(n_chunks)]
            m_blk = functools.reduce(jnp.maximum, chunks)
            m_old = m_ref[h]
            m_new = jnp.maximum(m_old, jnp.max(m_blk, axis=-1, keepdims=True))
            scale_old = jnp.exp(m_old - m_new)
            p = jnp.concatenate([jnp.exp(ch - m_new) for ch in chunks], axis=1).astype(BF16)
            pv = jnp.dot(p, vx, preferred_element_type=F32)
            acc_ref[h] = jnp.concatenate([scale_old, scale_old], axis=1) * acc_ref[h] + pv
            m_ref[h] = m_new

    def body(kb, carry):
        step(kb, False)
        return carry

    lax.fori_loop(0, qi, body, 0)
    step(qi, True)

    lam = (jnp.exp(jnp.sum(lq1_ref[...] * lk1_ref[...], keepdims=True))
           - jnp.exp(jnp.sum(lq2_ref[...] * lk2_ref[...], keepdims=True)) + LAM_INIT_LAYER0)
    o = (acc_ref[0, :, 0:HEAD_WIDTH] / acc_ref[0, :, HEAD_WIDTH:]
         - lam * (acc_ref[1, :, 0:HEAD_WIDTH] / acc_ref[1, :, HEAD_WIDTH:]))
    o = o * lax.rsqrt(jnp.mean(o * o, axis=-1, keepdims=True) + RMS_EPS)
    o_ref[...] = (o * g_ref[...] * (1.0 - LAM_INIT_LAYER0)).astype(BF16)


def _diff_attention(qkv, lq1, lk1, lq2, lk2, subln_g, batch, seq):
    n, w3 = qkv.shape
    width = w3 // 3
    heads = width // HEAD_WIDTH
    tq = _tile(seq, 512)
    nq = seq // tq
    kern = functools.partial(_diff_attn_kernel, tq=tq)
    vec = lambda: pl.BlockSpec((1, DIFF_HEAD_DIM), lambda b, h, i: (0, 0))
    return pl.pallas_call(
        kern,
        grid=(batch, heads, nq),
        in_specs=[pl.BlockSpec((tq, HEAD_WIDTH), lambda b, h, i: (b * nq + i, h)),
                  pl.BlockSpec((seq, HEAD_WIDTH), lambda b, h, i: (b, heads + h)),
                  pl.BlockSpec((seq, HEAD_WIDTH), lambda b, h, i: (b, 2 * heads + h)),
                  vec(), vec(), vec(), vec(),
                  pl.BlockSpec((1, HEAD_WIDTH), lambda b, h, i: (0, 0))],
        out_specs=pl.BlockSpec((tq, HEAD_WIDTH), lambda b, h, i: (b * nq + i, h)),
        out_shape=jax.ShapeDtypeStruct((n, width), BF16),
        scratch_shapes=[pltpu.VMEM((2, tq, HEAD_WIDTH), BF16),
                        pltpu.VMEM((2, tq, HEAD_WIDTH), F32),
                        pltpu.VMEM((2, tq, 2 * HEAD_WIDTH), F32)],
        compiler_params=_params("arbitrary", "arbitrary", "arbitrary"),
        name="diff_attention",
    )(qkv, qkv, qkv, lq1, lk1, lq2, lk2, subln_g)


def _outproj_ln_kernel(*refs, n_lhs, with_router):
    lhs_refs = refs[:n_lhs]
    w_ref, x_ref, g_ref, b_ref = refs[n_lhs:n_lhs + 4]
    rest = refs[n_lhs + 4:]
    y = None
    off = 0
    for a_ref in lhs_refs:
        kp = a_ref.shape[1]
        part = jnp.dot(a_ref[...], w_ref[off:off + kp, :], preferred_element_type=F32)
        y = part if y is None else y + part
        off += kp
    out = _layer_norm(ALPHA * x_ref[...] + y, g_ref[...], b_ref[...])
    if with_router:
        wr_ref, o_ref, logit_ref = rest
        o_ref[...] = out
        out_hi = out.astype(BF16)
        out_lo = (out - out_hi.astype(F32)).astype(BF16)
        hi_both = jnp.dot(out_hi, wr_ref[...], preferred_element_type=F32)
        lo_hi = jnp.dot(out_lo, wr_ref[:, 0:ROUTER_LANES], preferred_element_type=F32)
        logit_ref[...] = hi_both[:, 0:ROUTER_LANES] + (hi_both[:, ROUTER_LANES:] + lo_hi)
    else:
        (o_ref,) = rest
        o_ref[...] = out


def _outproj_ln(lhs_list, w_bf16, x2d, g, b, w_router_padded=None):
    n, d = x2d.shape
    tm = _tile(n, 512)
    with_router = w_router_padded is not None
    in_specs = [pl.BlockSpec((tm, a.shape[1]), lambda i: (i, 0)) for a in lhs_list]
    in_specs += [pl.BlockSpec(w_bf16.shape, lambda i: (0, 0)),
                 pl.BlockSpec((tm, d), lambda i: (i, 0)),
                 pl.BlockSpec((1, d), lambda i: (0, 0)),
                 pl.BlockSpec((1, d), lambda i: (0, 0))]
    out_specs = [pl.BlockSpec((tm, d), lambda i: (i, 0))]
    out_shape = [jax.ShapeDtypeStruct((n, d), F32)]
    args = list(lhs_list) + [w_bf16, x2d, g, b]
    if with_router:
        wr_hi = w_router_padded.astype(BF16)
        wr_lo = (w_router_padded - wr_hi.astype(F32)).astype(BF16)
        wr = jnp.concatenate([wr_hi, wr_lo], axis=1)
        in_specs.append(pl.BlockSpec(wr.shape, lambda i: (0, 0)))
        out_specs.append(pl.BlockSpec((tm, ROUTER_LANES), lambda i: (i, 0)))
        out_shape.append(jax.ShapeDtypeStruct((n, ROUTER_LANES), F32))
        args.append(wr)
    kern = functools.partial(_outproj_ln_kernel, n_lhs=len(lhs_list), with_router=with_router)
    res = pl.pallas_call(
        kern,
        grid=(n // tm,),
        in_specs=in_specs,
        out_specs=out_specs,
        out_shape=out_shape,
        compiler_params=_params("arbitrary"),
        name="outproj_ln_router" if with_router else "outproj_ln",
    )(*args)
    return res if with_router else res[0]


def _swiglu_block(xb, wg, wu, wd):
    gate = jnp.dot(xb, wg, preferred_element_type=F32)
    up = jnp.dot(xb, wu, preferred_element_type=F32)
    h = gate / (1.0 + jnp.exp(-gate)) * up
    return jnp.dot(h.astype(BF16), wd, preferred_element_type=F32)


def _ffn_ln_kernel(x_ref, wg_ref, wu_ref, wd_ref, g_ref, b_ref, o_ref, xb_ref, acc_ref):
    j = pl.program_id(1)

    @pl.when(j == 0)
    def _():
        xb_ref[...] = x_ref[...].astype(BF16)
        acc_ref[...] = jnp.zeros_like(acc_ref)

    acc_ref[...] += _swiglu_block(xb_ref[...], wg_ref[...], wu_ref[...], wd_ref[...])

    @pl.when(j == pl.num_programs(1) - 1)
    def _():
        o_ref[...] = _layer_norm(ALPHA * x_ref[...] + acc_ref[...], g_ref[...], b_ref[...])


def _ffn_ln(x2d, wg, wu, wd, g, b):
    n, d = x2d.shape
    ff = wg.shape[1]
    tm = _tile(n, 512)
    tf = _tile(ff, 512)
    return pl.pallas_call(
        _ffn_ln_kernel,
        grid=(n // tm, ff // tf),
        in_specs=[pl.BlockSpec((tm, d), lambda i, j: (i, 0)),
                  pl.BlockSpec((d, tf), lambda i, j: (0, j)),
                  pl.BlockSpec((d, tf), lambda i, j: (0, j)),
                  pl.BlockSpec((tf, d), lambda i, j: (j, 0)),
                  pl.BlockSpec((1, d), lambda i, j: (0, 0)),
                  pl.BlockSpec((1, d), lambda i, j: (0, 0))],
        out_specs=pl.BlockSpec((tm, d), lambda i, j: (i, 0)),
        out_shape=jax.ShapeDtypeStruct((n, d), F32),
        scratch_shapes=[pltpu.VMEM((tm, d), BF16), pltpu.VMEM((tm, d), F32)],
        compiler_params=_params("arbitrary", "arbitrary"),
        name="ffn_ln",
    )(x2d, wg, wu, wd, g, b)


def _odd_inproj_kernel(x_ref, wb_ref, wc_ref, wh_ref, cw_ref, o_ref, xb_ref, zs_ref, carry_ref,
                       *, seq, tm):
    i = pl.program_id(0)
    j = pl.program_id(1)

    @pl.when(j == 0)
    def _():
        xb_ref[...] = x_ref[...].astype(BF16)

    xb = xb_ref[...]
    b_gate = jnp.dot(xb, wb_ref[...], preferred_element_type=F32)
    c_gate = jnp.dot(xb, wc_ref[...], preferred_element_type=F32)
    hx = jnp.dot(xb, wh_ref[...], preferred_element_type=F32)
    z = c_gate * hx
    first = (i * tm) % seq == 0
    zs_ref[0:CONV_HALO, :] = jnp.where(first, 0.0, carry_ref[j])
    zs_ref[CONV_HALO:, :] = z
    carry_ref[j] = z[tm - CONV_HALO:, :]
    y = (cw_ref[0:1, :] * zs_ref[pl.ds(CONV_HALO - 2, tm), :]
         + cw_ref[1:2, :] * zs_ref[pl.ds(CONV_HALO - 1, tm), :]
         + cw_ref[2:3, :] * z)
    o_ref[...] = (b_gate * y).astype(BF16)


def _odd_inproj(x2d, w_bf16, conv_w, seq):
    n, d = x2d.shape
    cw = w_bf16.shape[1] // 3
    tm = _tile(seq, 512)
    tn = _tile(cw, 512)
    nj = cw // tn
    kern = functools.partial(_odd_inproj_kernel, seq=seq, tm=tm)
    return pl.pallas_call(
        kern,
        grid=(n // tm, nj),
        in_specs=[pl.BlockSpec((tm, d), lambda i, j: (i, 0)),
                  pl.BlockSpec((d, tn), lambda i, j: (0, j)),
                  pl.BlockSpec((d, tn), lambda i, j: (0, nj + j)),
                  pl.BlockSpec((d, tn), lambda i, j: (0, 2 * nj + j)),
                  pl.BlockSpec((CONV_K, tn), lambda i, j: (0, j))],
        out_specs=pl.BlockSpec((tm, tn), lambda i, j: (i, j)),
        out_shape=jax.ShapeDtypeStruct((n, cw), BF16),
        scratch_shapes=[pltpu.VMEM((tm, d), BF16),
                        pltpu.VMEM((CONV_HALO + tm, tn), F32),
                        pltpu.VMEM((nj, CONV_HALO, tn), F32)],
        compiler_params=_params("arbitrary", "arbitrary"),
        name="odd_inproj",
    )(x2d, w_bf16, w_bf16, w_bf16, conv_w)


def _route_kernel(lt_ref, d1_ref, d2_ref, g1_ref, g2_ref, te_ref, nact_ref, zfill_ref, rank_ref, *, tm):
    e, n = lt_ref.shape
    lt = lt_ref[...]
    eidx = lax.broadcasted_iota(jnp.int32, (e, n), 0)
    v1 = jnp.max(lt, axis=0, keepdims=True)
    i1 = jnp.min(jnp.where(lt == v1, eidx, e), axis=0, keepdims=True)
    oh1 = eidx == i1
    lt2 = jnp.where(oh1, -jnp.inf, lt)
    v2 = jnp.max(lt2, axis=0, keepdims=True)
    i2 = jnp.min(jnp.where(lt2 == v2, eidx, e), axis=0, keepdims=True)
    oh2 = eidx == i2
    ex = jnp.exp(v2 - v1)
    g1_ref[...] = 1.0 / (1.0 + ex)
    g2_ref[...] = ex / (1.0 + ex)

    sel = jnp.where(oh1 | oh2, 1.0, 0.0)
    a = lax.broadcasted_iota(jnp.int32, (CUMSUM_CHUNK, CUMSUM_CHUNK), 0)
    b = lax.broadcasted_iota(jnp.int32, (CUMSUM_CHUNK, CUMSUM_CHUNK), 1)
    upper = jnp.where(a <= b, 1.0, 0.0).astype(BF16)
    running = jnp.zeros((e, 1), F32)
    for c in range(n // CUMSUM_CHUNK):
        cols = slice(c * CUMSUM_CHUNK, (c + 1) * CUMSUM_CHUNK)
        blk = sel[:, cols]
        incl = jnp.dot(blk.astype(BF16), upper, preferred_element_type=F32)
        rank_ref[:, cols] = incl - blk + running
        running = running + incl[:, CUMSUM_CHUNK - 1:CUMSUM_CHUNK]

    log2_tm = tm.bit_length() - 1
    assert tm == 1 << log2_tm
    counts = running.astype(jnp.int32)
    padded = ((counts + (tm - 1)) >> log2_tm) << log2_tm
    e8 = lax.broadcasted_iota(jnp.int32, (e, 1), 0)
    offs = jnp.zeros((e, 1), jnp.int32)
    total = jnp.zeros((1, 1), jnp.int32)
    for k in range(e):
        offs = jnp.where(e8 == k, total, offs)
        total = total + padded[k:k + 1, :]
    ends = offs + padded
    zfill_ref[...] = jnp.where(padded > 0, ends - tm, -1)

    dest = rank_ref[...].astype(jnp.int32) + offs
    d1_ref[...] = jnp.sum(jnp.where(oh1, dest, 0), axis=0, keepdims=True)
    d2_ref[...] = jnp.sum(jnp.where(oh2, dest, 0), axis=0, keepdims=True)

    t = te_ref.shape[1]
    starts = lax.broadcasted_iota(jnp.int32, (e, t), 1) * tm
    te = jnp.sum(jnp.where(ends <= starts, 1, 0), axis=0, keepdims=True)
    te_ref[...] = jnp.minimum(te, e - 1)
    nact_ref[...] = total >> log2_tm


def _route(logits_t, tm, n_tiles):
    e, n = logits_t.shape
    assert n % CUMSUM_CHUNK == 0
    kern = functools.partial(_route_kernel, tm=tm)
    row_i = jax.ShapeDtypeStruct((1, n), jnp.int32)
    row_f = jax.ShapeDtypeStruct((1, n), F32)
    return pl.pallas_call(
        kern,
        out_shape=[row_i, row_i, row_f, row_f,
                   jax.ShapeDtypeStruct((1, n_tiles), jnp.int32),
                   jax.ShapeDtypeStruct((1, 1), jnp.int32),
                   jax.ShapeDtypeStruct((e, 1), jnp.int32)],
        scratch_shapes=[pltpu.VMEM((e, n), F32)],
        compiler_params=pltpu.CompilerParams(vmem_limit_bytes=VMEM_LIMIT_BYTES),
        name="route",
    )(logits_t)


def _dispatch_kernel(fill_ref, d1_ref, d2_ref, x_ref, xs_ref, zero_ref, sem, zsem, *, td, tm, n_exp, n_tiles):
    @pl.when(pl.program_id(0) == 0)
    def _():
        zero_ref[...] = jnp.zeros_like(zero_ref)

        def fill(row0):
            cp = pltpu.make_async_copy(zero_ref, xs_ref.at[pl.ds(pl.multiple_of(row0, tm), tm), :], zsem)
            cp.start()
            cp.wait()

        for e in range(n_exp):
            pl.when(fill_ref[e] >= 0)(functools.partial(fill, fill_ref[e]))
        for t in range(n_tiles - n_exp, n_tiles):
            pl.when(t >= fill_ref[n_exp])(functools.partial(fill, t * tm))

    def row_copy(r, dst_row):
        return pltpu.make_async_copy(x_ref.at[pl.ds(r, 1), :], xs_ref.at[pl.ds(dst_row, 1), :], sem)

    def issue(r, carry):
        row_copy(r, d1_ref[0, r]).start()
        row_copy(r, d2_ref[0, r]).start()
        return carry

    lax.fori_loop(0, td, issue, 0)
    for _ in range(TOP_K):
        pltpu.make_async_copy(x_ref, xs_ref.at[pl.ds(0, td), :], sem).wait()


def _dispatch(x2d, d1, d2, fill, tm, n_tiles):
    n, d = x2d.shape
    n_exp = fill.shape[0] - 1
    td = _tile(n, 256)
    nb = n // td
    kern = functools.partial(_dispatch_kernel, td=td, tm=tm, n_exp=n_exp, n_tiles=n_tiles)
    smem_idx = lambda: pl.BlockSpec((None, 1, td), lambda i: (i, 0, 0), memory_space=pltpu.SMEM)
    return pl.pallas_call(
        kern,
        grid=(nb,),
        in_specs=[pl.BlockSpec(memory_space=pltpu.SMEM), smem_idx(), smem_idx(),
                  pl.BlockSpec((td, d), lambda i: (i, 0))],
        out_specs=pl.BlockSpec(memory_space=pl.ANY),
        out_shape=jax.ShapeDtypeStruct((n_tiles * tm, d), F32),
        scratch_shapes=[pltpu.VMEM((tm, d), F32), pltpu.SemaphoreType.DMA, pltpu.SemaphoreType.DMA],
        compiler_params=_params("arbitrary"),
        name="dispatch",
    )(fill, d1.reshape(nb, 1, td), d2.reshape(nb, 1, td), x2d)


def _moe_kernel(te_ref, nact_ref, xs_ref, wg_ref, wu_ref, wd_ref, y_ref, xb_ref):
    i = pl.program_id(0)
    j = pl.program_id(1)
    active = i < nact_ref[0]

    @pl.when(j == 0)
    def _():
        y_ref[...] = jnp.zeros_like(y_ref)

    @pl.when(active & (j == 0))
    def _():
        xb_ref[...] = xs_ref[...].astype(BF16)

    @pl.when(active)
    def _():
        y_ref[...] += _swiglu_block(xb_ref[...], wg_ref[...].astype(BF16), wu_ref[...].astype(BF16),
                                    wd_ref[...].astype(BF16))


def _moe_ffn(xs, te, nact, wg, wu, wd, tm):
    rows, d = xs.shape
    n_exp, _, ff = wg.shape
    tf = _tile(ff, 512)
    nj = ff // tf
    n_tiles = rows // tm

    def row_blk(i, j, te_ref, nact_ref):
        return jnp.minimum(i, nact_ref[0] - 1)

    def ff_blk(i, j, te_ref, nact_ref):
        return jnp.where(i < nact_ref[0], j, nj - 1)

    def expert(i, j, te_ref, nact_ref):
        return te_ref[row_blk(i, j, te_ref, nact_ref)]

    grid_spec = pltpu.PrefetchScalarGridSpec(
        num_scalar_prefetch=2,
        grid=(n_tiles, nj),
        in_specs=[pl.BlockSpec((tm, d), lambda i, j, t, a: (row_blk(i, j, t, a), 0)),
                  pl.BlockSpec((None, d, tf), lambda i, j, t, a: (expert(i, j, t, a), 0, ff_blk(i, j, t, a))),
                  pl.BlockSpec((None, d, tf), lambda i, j, t, a: (expert(i, j, t, a), 0, ff_blk(i, j, t, a))),
                  pl.BlockSpec((None, tf, d), lambda i, j, t, a: (expert(i, j, t, a), ff_blk(i, j, t, a), 0))],
        out_specs=pl.BlockSpec((tm, d), lambda i, j, t, a: (i, 0)),
        scratch_shapes=[pltpu.VMEM((tm, d), BF16)],
    )
    return pl.pallas_call(
        _moe_kernel,
        grid_spec=grid_spec,
        out_shape=jax.ShapeDtypeStruct((rows, d), F32),
        compiler_params=_params("arbitrary", "arbitrary"),
        name="moe_ffn",
    )(te, nact, xs, wg, wu, wd)


def _combine_kernel(d1_ref, d2_ref, d1n_ref, d2n_ref, x_ref, g1_ref, g2_ref, lg_ref, lb_ref, y_ref, o_ref,
                    buf_ref, sems, *, tc):
    i = pl.program_id(0)
    slot = i % 2

    def gather(da_ref, db_ref, s):
        def issue(r, carry):
            for k, dk_ref in enumerate((da_ref, db_ref)):
                pltpu.make_async_copy(y_ref.at[pl.ds(dk_ref[0, r], 1), :],
                                      buf_ref.at[s, k, pl.ds(r, 1), :], sems.at[s]).start()
            return carry

        lax.fori_loop(0, tc, issue, 0)

    @pl.when(i == 0)
    def _():
        gather(d1_ref, d2_ref, 0)

    @pl.when(i + 1 < pl.num_programs(0))
    def _():
        gather(d1n_ref, d2n_ref, 1 - slot)

    for k in range(TOP_K):
        pltpu.make_async_copy(y_ref.at[pl.ds(0, tc), :], buf_ref.at[slot, k], sems.at[slot]).wait()

    f = g1_ref[...] * buf_ref[slot, 0] + g2_ref[...] * buf_ref[slot, 1]
    o_ref[...] = _layer_norm(ALPHA * x_ref[...] + f, lg_ref[...], lb_ref[...])


def _combine_ln(x2d, y, d1, d2, g1, g2, lg, lb):
    n, d = x2d.shape
    tc = _tile(n, 256)
    nb = n // tc
    kern = functools.partial(_combine_kernel, tc=tc)
    cur = lambda: pl.BlockSpec((None, 1, tc), lambda i: (i, 0, 0), memory_space=pltpu.SMEM)
    nxt = lambda: pl.BlockSpec((None, 1, tc), lambda i: (jnp.minimum(i + 1, nb - 1), 0, 0),
                               memory_space=pltpu.SMEM)
    d1b, d2b = d1.reshape(nb, 1, tc), d2.reshape(nb, 1, tc)
    return pl.pallas_call(
        kern,
        grid=(nb,),
        in_specs=[cur(), cur(), nxt(), nxt(),
                  pl.BlockSpec((tc, d), lambda i: (i, 0)),
                  pl.BlockSpec((tc, 1), lambda i: (i, 0)),
                  pl.BlockSpec((tc, 1), lambda i: (i, 0)),
                  pl.BlockSpec((1, d), lambda i: (0, 0)),
                  pl.BlockSpec((1, d), lambda i: (0, 0)),
                  pl.BlockSpec(memory_space=pl.ANY)],
        out_specs=pl.BlockSpec((tc, d), lambda i: (i, 0)),
        out_shape=jax.ShapeDtypeStruct((n, d), F32),
        scratch_shapes=[pltpu.VMEM((2, TOP_K, tc, d), F32), pltpu.SemaphoreType.DMA((2,))],
        compiler_params=_params("arbitrary"),
        name="combine_ln",
    )(d1b, d2b, d1b, d2b, x2d, g1.reshape(n, 1), g2.reshape(n, 1), lg, lb, y)


def kernel(x, ev_w_in, ev_w_pool, ev_pool_scale, ev_lam_q1, ev_lam_k1, ev_lam_q2, ev_lam_k2, ev_subln_g, ev_w_out, ev_ln1_g, ev_ln1_b, ev_ffn_wg, ev_ffn_wu, ev_ffn_wd, ev_ln2_g, ev_ln2_b, od_w_in, od_conv_w, od_w_out, od_ln1_g, od_ln1_b, od_router, od_exp_wg, od_exp_wu, od_exp_wd, od_ln2_g, od_ln2_b):
    batch, seq, d = x.shape
    n = batch * seq
    assert ev_w_in.shape[0] == 1 and od_w_in.shape[0] == 1, "DEPTH == 2: one even and one odd layer"
    h = x.reshape(n, d)
    bf = lambda w: w.astype(BF16)
    row = lambda v: v.reshape(1, -1)

    pool_width = ev_w_pool.shape[1] * ev_w_pool.shape[2]
    u_pool, qkv = _even_inproj(h, bf(ev_w_in[0]), pool_width)
    a = _pool_mixer(u_pool, bf(ev_w_pool[0]), row(ev_pool_scale[0]), seq)
    o = _diff_attention(qkv, row(ev_lam_q1[0]), row(ev_lam_k1[0]), row(ev_lam_q2[0]), row(ev_lam_k2[0]),
                        row(ev_subln_g[0]), batch, seq)
    h = _outproj_ln([a, o], bf(ev_w_out[0]), h, row(ev_ln1_g[0]), row(ev_ln1_b[0]))
    h = _ffn_ln(h, bf(ev_ffn_wg[0]), bf(ev_ffn_wu[0]), bf(ev_ffn_wd[0]), row(ev_ln2_g[0]), row(ev_ln2_b[0]))

    gated = _odd_inproj(h, bf(od_w_in[0]), od_conv_w[0].reshape(CONV_K, -1), seq)
    n_exp = od_router.shape[-1]
    w_router = jnp.pad(od_router[0], ((0, 0), (0, ROUTER_LANES - n_exp)))
    h, logits = _outproj_ln([gated], bf(od_w_out[0]), h, row(od_ln1_g[0]), row(od_ln1_b[0]), w_router)

    tm = _tile(n, 512)
    n_tiles = (TOP_K * n) // tm + n_exp
    d1, d2, g1, g2, te, nact, zfill = _route(logits[:, :n_exp].T, tm, n_tiles)
    fill = jnp.concatenate([zfill.reshape(-1), nact.reshape(-1)])
    xs = _dispatch(h, d1, d2, fill, tm, n_tiles)
    y = _moe_ffn(xs, te.reshape(-1), nact.reshape(-1), od_exp_wg[0], od_exp_wu[0], od_exp_wd[0], tm)
    h = _combine_ln(h, y, d1, d2, g1, g2, row(od_ln2_g[0]), row(od_ln2_b[0]))
    return h.reshape(batch, seq, d)
```

```python
import functools
import math

import jax
import jax.numpy as jnp
from jax import lax
from jax.experimental import pallas as pl
from jax.experimental.pallas import tpu as pltpu

F32 = jnp.float32
BF16 = jnp.bfloat16

DEPTH = 2
ALPHA = (2 * DEPTH) ** 0.25
LN_EPS = 1e-5
RMS_EPS = 1e-5
POOL_WINDOWS = (2, 4, 8, 16)
POOL_HALO = 16
DIFF_HEAD_DIM = 64
HEAD_WIDTH = 2 * DIFF_HEAD_DIM
LAM_INIT_LAYER0 = 0.8 - 0.6 * math.exp(-0.3 * 0)
CONV_K = 3
CONV_HALO = 8
TOP_K = 2
ROUTER_LANES = 128
CUMSUM_CHUNK = 256

VMEM_LIMIT_BYTES = 56 * 1024 * 1024


def _tile(n, pref):
    t = min(n, pref)
    assert n % t == 0, (n, t)
    return t


def _params(*sem):
    return pltpu.CompilerParams(dimension_semantics=sem, vmem_limit_bytes=VMEM_LIMIT_BYTES)


def _layer_norm(v, g, b):
    mu = jnp.mean(v, axis=-1, keepdims=True)
    d = v - mu
    var = jnp.mean(d * d, axis=-1, keepdims=True)
    return d * lax.rsqrt(var + LN_EPS) * g + b


def _even_inproj_kernel(x_ref, w_ref, pool_ref, qkv_ref, xb_ref):
    j = pl.program_id(1)

    @pl.when(j == 0)
    def _():
        xb_ref[...] = x_ref[...].astype(BF16)

    acc = jnp.dot(xb_ref[...], w_ref[...], preferred_element_type=F32)

    @pl.when(j == 0)
    def _():
        pool_ref[...] = acc

    @pl.when(j > 0)
    def _():
        qkv_ref[...] = acc.astype(BF16)


def _even_inproj(x2d, w_bf16, pool_width):
    n, d = x2d.shape
    width = w_bf16.shape[1]
    tn = pool_width
    assert (width - pool_width) % tn == 0
    tm = _tile(n, 1024)
    return pl.pallas_call(
        _even_inproj_kernel,
        grid=(n // tm, width // tn),
        in_specs=[pl.BlockSpec((tm, d), lambda i, j: (i, 0)),
                  pl.BlockSpec((d, tn), lambda i, j: (0, j))],
        out_specs=[pl.BlockSpec((tm, tn), lambda i, j: (i, 0)),
                   pl.BlockSpec((tm, tn), lambda i, j: (i, jnp.maximum(j - 1, 0)))],
        out_shape=[jax.ShapeDtypeStruct((n, pool_width), F32),
                   jax.ShapeDtypeStruct((n, width - pool_width), BF16)],
        scratch_shapes=[pltpu.VMEM((tm, d), BF16)],
        compiler_params=_params("arbitrary", "arbitrary"),
        name="even_inproj",
    )(x2d, w_bf16)


def _pool_kernel(u_ref, halo_ref, w_ref, scale_ref, o_ref, buf_ref, *, seq, tp, cg):
    i = pl.program_id(0)
    first = (i * tp) % seq == 0
    pos = (i * tp) % seq + lax.broadcasted_iota(jnp.int32, (tp, 1), 0)
    for g, window in enumerate(POOL_WINDOWS):
        cols = slice(g * cg, (g + 1) * cg)
        u = u_ref[:, cols]
        buf_ref[0:POOL_HALO, :] = jnp.where(first, 0.0, halo_ref[:, cols])
        buf_ref[POOL_HALO:, :] = u
        total = u
        for back in range(1, window):
            total = total + buf_ref[pl.ds(POOL_HALO - back, tp), :]
        count = jnp.minimum(pos + 1, window).astype(F32)
        pooled = total / count - u
        mixed = jnp.dot(pooled.astype(BF16), w_ref[g], preferred_element_type=F32)
        o_ref[:, cols] = (mixed * scale_ref[:, cols]).astype(BF16)


def _pool_mixer(u, w_pool_bf16, scale, seq):
    n, pw = u.shape
    groups, cg, _ = w_pool_bf16.shape
    assert groups == len(POOL_WINDOWS) and groups * cg == pw
    tp = _tile(seq, 512)
    halo_blocks = tp // POOL_HALO
    kern = functools.partial(_pool_kernel, seq=seq, tp=tp, cg=cg)
    return pl.pallas_call(
        kern,
        grid=(n // tp,),
        in_specs=[pl.BlockSpec((tp, pw), lambda i: (i, 0)),
                  pl.BlockSpec((POOL_HALO, pw), lambda i: (jnp.maximum(i * halo_blocks - 1, 0), 0)),
                  pl.BlockSpec((groups, cg, cg), lambda i: (0, 0, 0)),
                  pl.BlockSpec((1, pw), lambda i: (0, 0))],
        out_specs=pl.BlockSpec((tp, pw), lambda i: (i, 0)),
        out_shape=jax.ShapeDtypeStruct((n, pw), BF16),
        scratch_shapes=[pltpu.VMEM((POOL_HALO + tp, cg), F32)],
        compiler_params=_params("arbitrary"),
        name="pool_mixer",
    )(u, u, w_pool_bf16, scale)


def _diff_attn_kernel(q_ref, k_ref, v_ref, lq1_ref, lk1_ref, lq2_ref, lk2_ref, g_ref, o_ref,
                      qs_ref, m_ref, acc_ref, sa_ref, sb_ref, *, tq):
    qi = pl.program_id(2)
    q = q_ref[...]
    lane = lax.broadcasted_iota(jnp.int32, q.shape, 1)
    zero = jnp.zeros_like(q)
    scale = jnp.asarray(DIFF_HEAD_DIM ** -0.5, BF16)
    qs_ref[0] = jnp.where(lane < DIFF_HEAD_DIM, q, zero) * scale
    qs_ref[1] = jnp.where(lane >= DIFF_HEAD_DIM, q, zero) * scale
    m_ref[...] = jnp.full(m_ref.shape, -jnp.inf, F32)
    acc_ref[...] = jnp.zeros(acc_ref.shape, F32)
    ones = jnp.ones((tq, HEAD_WIDTH), BF16)
    n_chunks = tq // HEAD_WIDTH

    def scores(kb, s_ref):
        k = k_ref[pl.ds(pl.multiple_of(kb * tq, tq), tq), :]
        for h in range(2):
            s_ref[h] = lax.dot_general(qs_ref[h], k, (((1,), (1,)), ((), ())), preferred_element_type=F32)

    def step(kb, s_ref, masked):
        vx = jnp.concatenate([v_ref[pl.ds(pl.multiple_of(kb * tq, tq), tq), :], ones], axis=1)
        for h in range(2):
            s = s_ref[h]
            if masked:
                r = lax.broadcasted_iota(jnp.int32, (tq, tq), 0)
                c = lax.broadcasted_iota(jnp.int32, (tq, tq), 1)
                s = jnp.where(c <= r, s, -jnp.inf)
            chunks = [s[:, i * HEAD_WIDTH:(i + 1) * HEAD_WIDTH] for i in range(n_chunks)]
            m_blk = functools.reduce(jnp.maximum, chunks)
            m_old = m_ref[h]
            m_new = jnp.maximum(m_old, jnp.max(m_blk, axis=-1, keepdims=True))
            scale_old = jnp.exp(m_old - m_new)
            p = jnp.concatenate([jnp.exp(ch - m_new) for ch in chunks], axis=1).astype(BF16)
            pv = jnp.dot(p, vx, preferred_element_type=F32)
            acc_ref[h] = jnp.concatenate([scale_old, scale_old], axis=1) * acc_ref[h] + pv
            m_ref[h] = m_new

    def pair(t, carry):
        scores(2 * t + 1, sb_ref)
        step(2 * t, sa_ref, False)
        scores(2 * t + 2, sa_ref)
        step(2 * t + 1, sb_ref, False)
        return carry

    scores(0, sa_ref)
    lax.fori_loop(0, qi // 2, pair, 0)

    @pl.when(qi % 2 == 0)
    def _():
        step(qi, sa_ref, True)

    @pl.when(qi % 2 == 1)
    def _():
        scores(qi, sb_ref)
        step(qi - 1, sa_ref, False)
        step(qi, sb_ref, True)

    lam = (jnp.exp(jnp.sum(lq1_ref[...] * lk1_ref[...], keepdims=True))
           - jnp.exp(jnp.sum(lq2_ref[...] * lk2_ref[...], keepdims=True)) + LAM_INIT_LAYER0)
    o = (acc_ref[0, :, 0:HEAD_WIDTH] / acc_ref[0, :, HEAD_WIDTH:]
         - lam * (acc_ref[1, :, 0:HEAD_WIDTH] / acc_ref[1, :, HEAD_WIDTH:]))
    o = o * lax.rsqrt(jnp.mean(o * o, axis=-1, keepdims=True) + RMS_EPS)
    o_ref[...] = (o * g_ref[...] * (1.0 - LAM_INIT_LAYER0)).astype(BF16)


def _diff_attention(qkv, lq1, lk1, lq2, lk2, subln_g, batch, seq):
    n, w3 = qkv.shape
    width = w3 // 3
    heads = width // HEAD_WIDTH
    tq = _tile(seq, 512)
    nq = seq // tq
    kern = functools.partial(_diff_attn_kernel, tq=tq)
    vec = lambda: pl.BlockSpec((1, DIFF_HEAD_DIM), lambda b, h, i: (0, 0))
    return pl.pallas_call(
        kern,
        grid=(batch, heads, nq),
        in_specs=[pl.BlockSpec((tq, HEAD_WIDTH), lambda b, h, i: (b * nq + i, h)),
                  pl.BlockSpec((seq, HEAD_WIDTH), lambda b, h, i: (b, heads + h)),
                  pl.BlockSpec((seq, HEAD_WIDTH), lambda b, h, i: (b, 2 * heads + h)),
                  vec(), vec(), vec(), vec(),
                  pl.BlockSpec((1, HEAD_WIDTH), lambda b, h, i: (0, 0))],
        out_specs=pl.BlockSpec((tq, HEAD_WIDTH), lambda b, h, i: (b * nq + i, h)),
        out_shape=jax.ShapeDtypeStruct((n, width), BF16),
        scratch_shapes=[pltpu.VMEM((2, tq, HEAD_WIDTH), BF16),
                        pltpu.VMEM((2, tq, HEAD_WIDTH), F32),
                        pltpu.VMEM((2, tq, 2 * HEAD_WIDTH), F32),
                        pltpu.VMEM((2, tq, tq), F32), pltpu.VMEM((2, tq, tq), F32)],
        compiler_params=_params("arbitrary", "arbitrary", "arbitrary"),
        name="diff_attention",
    )(qkv, qkv, qkv, lq1, lk1, lq2, lk2, subln_g)


def _outproj_ln_kernel(*refs, n_lhs, with_router):
    lhs_refs = refs[:n_lhs]
    w_ref, x_ref, g_ref, b_ref = refs[n_lhs:n_lhs + 4]
    rest = refs[n_lhs + 4:]
    y = None
    off = 0
    for a_ref in lhs_refs:
        kp = a_ref.shape[1]
        part = jnp.dot(a_ref[...], w_ref[off:off + kp, :], preferred_element_type=F32)
        y = part if y is None else y + part
        off += kp
    out = _layer_norm(ALPHA * x_ref[...] + y, g_ref[...], b_ref[...])
    if with_router:
        wr_ref, o_ref, logit_ref = rest
        o_ref[...] = out
        out_hi = out.astype(BF16)
        out_lo = (out - out_hi.astype(F32)).astype(BF16)
        hi_both = jnp.dot(out_hi, wr_ref[...], preferred_element_type=F32)
        lo_hi = jnp.dot(out_lo, wr_ref[:, 0:ROUTER_LANES], preferred_element_type=F32)
        logit_ref[...] = hi_both[:, 0:ROUTER_LANES] + (hi_both[:, ROUTER_LANES:] + lo_hi)
    else:
        (o_ref,) = rest
        o_ref[...] = out


def _outproj_ln(lhs_list, w_bf16, x2d, g, b, w_router_padded=None):
    n, d = x2d.shape
    tm = _tile(n, 512)
    with_router = w_router_padded is not None
    in_specs = [pl.BlockSpec((tm, a.shape[1]), lambda i: (i, 0)) for a in lhs_list]
    in_specs += [pl.BlockSpec(w_bf16.shape, lambda i: (0, 0)),
                 pl.BlockSpec((tm, d), lambda i: (i, 0)),
                 pl.BlockSpec((1, d), lambda i: (0, 0)),
                 pl.BlockSpec((1, d), lambda i: (0, 0))]
    out_specs = [pl.BlockSpec((tm, d), lambda i: (i, 0))]
    out_shape = [jax.ShapeDtypeStruct((n, d), F32)]
    args = list(lhs_list) + [w_bf16, x2d, g, b]
    if with_router:
        wr_hi = w_router_padded.astype(BF16)
        wr_lo = (w_router_padded - wr_hi.astype(F32)).astype(BF16)
        wr = jnp.concatenate([wr_hi, wr_lo], axis=1)
        in_specs.append(pl.BlockSpec(wr.shape, lambda i: (0, 0)))
        out_specs.append(pl.BlockSpec((tm, ROUTER_LANES), lambda i: (i, 0)))
        out_shape.append(jax.ShapeDtypeStruct((n, ROUTER_LANES), F32))
        args.append(wr)
    kern = functools.partial(_outproj_ln_kernel, n_lhs=len(lhs_list), with_router=with_router)
    res = pl.pallas_call(
        kern,
        grid=(n // tm,),
        in_specs=in_specs,
        out_specs=out_specs,
        out_shape=out_shape,
        compiler_params=_params("arbitrary"),
        name="outproj_ln_router" if with_router else "outproj_ln",
    )(*args)
    return res if with_router else res[0]


def _swiglu_block(xb, wg, wu, wd):
    gate = jnp.dot(xb, wg, preferred_element_type=F32)
    up = jnp.dot(xb, wu, preferred_element_type=F32)
    h = gate / (1.0 + jnp.exp(-gate)) * up
    return jnp.dot(h.astype(BF16), wd, preferred_element_type=F32)


def _ffn_ln_kernel(x_ref, wg_ref, wu_ref, wd_ref, g_ref, b_ref, o_ref, xb_ref, acc_ref):
    j = pl.program_id(1)

    @pl.when(j == 0)
    def _():
        xb_ref[...] = x_ref[...].astype(BF16)
        acc_ref[...] = jnp.zeros_like(acc_ref)

    acc_ref[...] += _swiglu_block(xb_ref[...], wg_ref[...], wu_ref[...], wd_ref[...])

    @pl.when(j == pl.num_programs(1) - 1)
    def _():
        o_ref[...] = _layer_norm(ALPHA * x_ref[...] + acc_ref[...], g_ref[...], b_ref[...])


def _ffn_ln(x2d, wg, wu, wd, g, b):
    n, d = x2d.shape
    ff = wg.shape[1]
    tm = _tile(n, 512)
    tf = _tile(ff, 512)
    return pl.pallas_call(
        _ffn_ln_kernel,
        grid=(n // tm, ff // tf),
        in_specs=[pl.BlockSpec((tm, d), lambda i, j: (i, 0)),
                  pl.BlockSpec((d, tf), lambda i, j: (0, j)),
                  pl.BlockSpec((d, tf), lambda i, j: (0, j)),
                  pl.BlockSpec((tf, d), lambda i, j: (j, 0)),
                  pl.BlockSpec((1, d), lambda i, j: (0, 0)),
                  pl.BlockSpec((1, d), lambda i, j: (0, 0))],
        out_specs=pl.BlockSpec((tm, d), lambda i, j: (i, 0)),
        out_shape=jax.ShapeDtypeStruct((n, d), F32),
        scratch_shapes=[pltpu.VMEM((tm, d), BF16), pltpu.VMEM((tm, d), F32)],
        compiler_params=_params("arbitrary", "arbitrary"),
        name="ffn_ln",
    )(x2d, wg, wu, wd, g, b)


def _odd_inproj_kernel(x_ref, wb_ref, wc_ref, wh_ref, cw_ref, o_ref, xb_ref, zs_ref, carry_ref,
                       *, seq, tm):
    i = pl.program_id(0)
    j = pl.program_id(1)

    @pl.when(j == 0)
    def _():
        xb_ref[...] = x_ref[...].astype(BF16)

    xb = xb_ref[...]
    b_gate = jnp.dot(xb, wb_ref[...], preferred_element_type=F32)
    c_gate = jnp.dot(xb, wc_ref[...], preferred_element_type=F32)
    hx = jnp.dot(xb, wh_ref[...], preferred_element_type=F32)
    z = c_gate * hx
    first = (i * tm) % seq == 0
    zs_ref[0:CONV_HALO, :] = jnp.where(first, 0.0, carry_ref[j])
    zs_ref[CONV_HALO:, :] = z
    carry_ref[j] = z[tm - CONV_HALO:, :]
    y = (cw_ref[0:1, :] * zs_ref[pl.ds(CONV_HALO - 2, tm), :]
         + cw_ref[1:2, :] * zs_ref[pl.ds(CONV_HALO - 1, tm), :]
         + cw_ref[2:3, :] * z)
    o_ref[...] = (b_gate * y).astype(BF16)


def _odd_inproj(x2d, w_bf16, conv_w, seq):
    n, d = x2d.shape
    cw = w_bf16.shape[1] // 3
    tm = _tile(seq, 512)
    tn = _tile(cw, 512)
    nj = cw // tn
    kern = functools.partial(_odd_inproj_kernel, seq=seq, tm=tm)
    return pl.pallas_call(
        kern,
        grid=(n // tm, nj),
        in_specs=[pl.BlockSpec((tm, d), lambda i, j: (i, 0)),
                  pl.BlockSpec((d, tn), lambda i, j: (0, j)),
                  pl.BlockSpec((d, tn), lambda i, j: (0, nj + j)),
                  pl.BlockSpec((d, tn), lambda i, j: (0, 2 * nj + j)),
                  pl.BlockSpec((CONV_K, tn), lambda i, j: (0, j))],
        out_specs=pl.BlockSpec((tm, tn), lambda i, j: (i, j)),
        out_shape=jax.ShapeDtypeStruct((n, cw), BF16),
        scratch_shapes=[pltpu.VMEM((tm, d), BF16),
                        pltpu.VMEM((CONV_HALO + tm, tn), F32),
                        pltpu.VMEM((nj, CONV_HALO, tn), F32)],
        compiler_params=_params("arbitrary", "arbitrary"),
        name="odd_inproj",
    )(x2d, w_bf16, w_bf16, w_bf16, conv_w)


def _route_kernel(lt_ref, d1_ref, d2_ref, g1_ref, g2_ref, te_ref, nact_ref, zfill_ref, half2_ref, rank_ref,
                  *, tm):
    e, n = lt_ref.shape
    lt = lt_ref[...]
    eidx = lax.broadcasted_iota(jnp.int32, (e, n), 0)
    v1 = jnp.max(lt, axis=0, keepdims=True)
    i1 = jnp.min(jnp.where(lt == v1, eidx, e), axis=0, keepdims=True)
    oh1 = eidx == i1
    lt2 = jnp.where(oh1, -jnp.inf, lt)
    v2 = jnp.max(lt2, axis=0, keepdims=True)
    i2 = jnp.min(jnp.where(lt2 == v2, eidx, e), axis=0, keepdims=True)
    oh2 = eidx == i2
    ex = jnp.exp(v2 - v1)
    g1_ref[...] = 1.0 / (1.0 + ex)
    g2_ref[...] = ex / (1.0 + ex)

    sel = jnp.where(oh1 | oh2, 1.0, 0.0)
    a = lax.broadcasted_iota(jnp.int32, (CUMSUM_CHUNK, CUMSUM_CHUNK), 0)
    b = lax.broadcasted_iota(jnp.int32, (CUMSUM_CHUNK, CUMSUM_CHUNK), 1)
    upper = jnp.where(a <= b, 1.0, 0.0).astype(BF16)
    running = jnp.zeros((e, 1), F32)
    for c in range(n // CUMSUM_CHUNK):
        cols = slice(c * CUMSUM_CHUNK, (c + 1) * CUMSUM_CHUNK)
        blk = sel[:, cols]
        incl = jnp.dot(blk.astype(BF16), upper, preferred_element_type=F32)
        rank_ref[:, cols] = incl - blk + running
        running = running + incl[:, CUMSUM_CHUNK - 1:CUMSUM_CHUNK]

    log2_tm = tm.bit_length() - 1
    assert tm == 1 << log2_tm
    counts = running.astype(jnp.int32)
    padded = ((counts + (tm - 1)) >> log2_tm) << log2_tm
    e8 = lax.broadcasted_iota(jnp.int32, (e, 1), 0)
    offs = jnp.zeros((e, 1), jnp.int32)
    total = jnp.zeros((1, 1), jnp.int32)
    for k in range(e):
        offs = jnp.where(e8 == k, total, offs)
        total = total + padded[k:k + 1, :]
    ends = offs + padded
    zfill_ref[...] = jnp.where(padded > 0, ends - tm, -1)

    dest = rank_ref[...].astype(jnp.int32) + offs
    d1_ref[...] = jnp.sum(jnp.where(oh1, dest, 0), axis=0, keepdims=True)
    d2_ref[...] = jnp.sum(jnp.where(oh2, dest, 0), axis=0, keepdims=True)

    t = te_ref.shape[1]
    starts = lax.broadcasted_iota(jnp.int32, (e, t), 1) * tm
    te = jnp.sum(jnp.where(ends <= starts, 1, 0), axis=0, keepdims=True)
    te_ref[...] = jnp.minimum(te, e - 1)
    nact_ref[...] = total >> log2_tm
    in_group = (offs <= starts) & (starts < ends)
    half2 = in_group & (starts + tm // 2 < offs + counts)
    half2_ref[...] = jnp.sum(jnp.where(half2, 1, 0), axis=0, keepdims=True)


def _route(logits_t, tm, n_tiles):
    e, n = logits_t.shape
    assert n % CUMSUM_CHUNK == 0
    kern = functools.partial(_route_kernel, tm=tm)
    row_i = jax.ShapeDtypeStruct((1, n), jnp.int32)
    row_f = jax.ShapeDtypeStruct((1, n), F32)
    return pl.pallas_call(
        kern,
        out_shape=[row_i, row_i, row_f, row_f,
                   jax.ShapeDtypeStruct((1, n_tiles), jnp.int32),
                   jax.ShapeDtypeStruct((1, 1), jnp.int32),
                   jax.ShapeDtypeStruct((e, 1), jnp.int32),
                   jax.ShapeDtypeStruct((1, n_tiles), jnp.int32)],
        scratch_shapes=[pltpu.VMEM((e, n), F32)],
        compiler_params=pltpu.CompilerParams(vmem_limit_bytes=VMEM_LIMIT_BYTES),
        name="route",
    )(logits_t)


def _dispatch_kernel(fill_ref, d1_ref, d2_ref, x_ref, xs_ref, zero_ref, sem, zsem, *, td, tm, n_exp, n_tiles):
    @pl.when(pl.program_id(0) == 0)
    def _():
        zero_ref[...] = jnp.zeros_like(zero_ref)

        def fill(row0):
            cp = pltpu.make_async_copy(zero_ref, xs_ref.at[pl.ds(pl.multiple_of(row0, tm), tm), :], zsem)
            cp.start()
            cp.wait()

        for e in range(n_exp):
            pl.when(fill_ref[e] >= 0)(functools.partial(fill, fill_ref[e]))
        for t in range(n_tiles - n_exp, n_tiles):
            pl.when(t >= fill_ref[n_exp])(functools.partial(fill, t * tm))

    def row_copy(r, dst_row):
        return pltpu.make_async_copy(x_ref.at[pl.ds(r, 1), :], xs_ref.at[pl.ds(dst_row, 1), :], sem)

    def issue(r, carry):
        row_copy(r, d1_ref[0, r]).start()
        row_copy(r, d2_ref[0, r]).start()
        return carry

    lax.fori_loop(0, td, issue, 0)
    for _ in range(TOP_K):
        pltpu.make_async_copy(x_ref, xs_ref.at[pl.ds(0, td), :], sem).wait()


def _dispatch(x2d, d1, d2, fill, tm, n_tiles):
    n, d = x2d.shape
    n_exp = fill.shape[0] - 1
    td = _tile(n, 256)
    nb = n // td
    kern = functools.partial(_dispatch_kernel, td=td, tm=tm, n_exp=n_exp, n_tiles=n_tiles)
    smem_idx = lambda: pl.BlockSpec((None, 1, td), lambda i: (i, 0, 0), memory_space=pltpu.SMEM)
    return pl.pallas_call(
        kern,
        grid=(nb,),
        in_specs=[pl.BlockSpec(memory_space=pltpu.SMEM), smem_idx(), smem_idx(),
                  pl.BlockSpec((td, d), lambda i: (i, 0))],
        out_specs=pl.BlockSpec(memory_space=pl.ANY),
        out_shape=jax.ShapeDtypeStruct((n_tiles * tm, d), F32),
        scratch_shapes=[pltpu.VMEM((tm, d), F32), pltpu.SemaphoreType.DMA, pltpu.SemaphoreType.DMA],
        compiler_params=_params("arbitrary"),
        name="dispatch",
    )(fill, d1.reshape(nb, 1, td), d2.reshape(nb, 1, td), x2d)


def _moe_kernel(te_ref, nact_ref, half2_ref, xs_hbm, wg_ref, wu_ref, wd_ref, y_ref, stage_ref, xb_ref, sem,
                *, tm):
    i = pl.program_id(0)
    j = pl.program_id(1)
    nact = nact_ref[0]
    active = i < nact
    half = tm // 2

    def tile_copy(t):
        return pltpu.make_async_copy(xs_hbm.at[pl.ds(pl.multiple_of(t * tm, tm), tm), :], stage_ref, sem)

    @pl.when(j == 0)
    def _():
        y_ref[...] = jnp.zeros_like(y_ref)

    @pl.when((i == 0) & (j == 0))
    def _():
        tile_copy(0).start()

    @pl.when(active & (j == 0))
    def _():
        tile_copy(i).wait()
        xb_ref[...] = stage_ref[...].astype(BF16)

    @pl.when((j == 1) & (i + 1 < nact))
    def _():
        tile_copy(i + 1).start()

    @pl.when(active)
    def _():
        wg = wg_ref[...].astype(BF16)
        wu = wu_ref[...].astype(BF16)
        wd = wd_ref[...].astype(BF16)
        y_ref[0:half, :] += _swiglu_block(xb_ref[0:half, :], wg, wu, wd)

        @pl.when(half2_ref[i] > 0)
        def _():
            y_ref[half:, :] += _swiglu_block(xb_ref[half:, :], wg, wu, wd)


def _moe_ffn(xs, te, nact, half2, wg, wu, wd, tm):
    rows, d = xs.shape
    n_exp, _, ff = wg.shape
    tf = _tile(ff, 256)
    nj = ff // tf
    assert nj >= 2
    n_tiles = rows // tm

    def ff_blk(i, j, nact_ref):
        return jnp.where(i < nact_ref[0], j, nj - 1)

    def expert(i, te_ref, nact_ref):
        return te_ref[jnp.minimum(i, nact_ref[0] - 1)]

    grid_spec = pltpu.PrefetchScalarGridSpec(
        num_scalar_prefetch=3,
        grid=(n_tiles, nj),
        in_specs=[pl.BlockSpec(memory_space=pl.ANY),
                  pl.BlockSpec((None, d, tf), lambda i, j, t, a, h: (expert(i, t, a), 0, ff_blk(i, j, a))),
                  pl.BlockSpec((None, d, tf), lambda i, j, t, a, h: (expert(i, t, a), 0, ff_blk(i, j, a))),
                  pl.BlockSpec((None, tf, d), lambda i, j, t, a, h: (expert(i, t, a), ff_blk(i, j, a), 0))],
        out_specs=pl.BlockSpec((tm, d), lambda i, j, t, a, h: (i, 0)),
        scratch_shapes=[pltpu.VMEM((tm, d), F32), pltpu.VMEM((tm, d), BF16), pltpu.SemaphoreType.DMA],
    )
    return pl.pallas_call(
        functools.partial(_moe_kernel, tm=tm),
        grid_spec=grid_spec,
        out_shape=jax.ShapeDtypeStruct((rows, d), F32),
        compiler_params=_params("arbitrary", "arbitrary"),
        name="moe_ffn",
    )(te, nact, half2, xs, wg, wu, wd)


def _combine_kernel(d1_ref, d2_ref, d1n_ref, d2n_ref, x_ref, g1_ref, g2_ref, lg_ref, lb_ref, y_ref, o_ref,
                    buf_ref, sems, *, tc):
    i = pl.program_id(0)
    slot = i % 2

    def gather(da_ref, db_ref, s):
        def issue(r, carry):
            for k, dk_ref in enumerate((da_ref, db_ref)):
                pltpu.make_async_copy(y_ref.at[pl.ds(dk_ref[0, r], 1), :],
                                      buf_ref.at[s, k, pl.ds(r, 1), :], sems.at[s]).start()
            return carry

        lax.fori_loop(0, tc, issue, 0)

    @pl.when(i == 0)
    def _():
        gather(d1_ref, d2_ref, 0)

    @pl.when(i + 1 < pl.num_programs(0))
    def _():
        gather(d1n_ref, d2n_ref, 1 - slot)

    for k in range(TOP_K):
        pltpu.make_async_copy(y_ref.at[pl.ds(0, tc), :], buf_ref.at[slot, k], sems.at[slot]).wait()

    f = g1_ref[...] * buf_ref[slot, 0] + g2_ref[...] * buf_ref[slot, 1]
    o_ref[...] = _layer_norm(ALPHA * x_ref[...] + f, lg_ref[...], lb_ref[...])


def _combine_ln(x2d, y, d1, d2, g1, g2, lg, lb):
    n, d = x2d.shape
    tc = _tile(n, 256)
    nb = n // tc
    kern = functools.partial(_combine_kernel, tc=tc)
    cur = lambda: pl.BlockSpec((None, 1, tc), lambda i: (i, 0, 0), memory_space=pltpu.SMEM)
    nxt = lambda: pl.BlockSpec((None, 1, tc), lambda i: (jnp.minimum(i + 1, nb - 1), 0, 0),
                               memory_space=pltpu.SMEM)
    d1b, d2b = d1.reshape(nb, 1, tc), d2.reshape(nb, 1, tc)
    return pl.pallas_call(
        kern,
        grid=(nb,),
        in_specs=[cur(), cur(), nxt(), nxt(),
                  pl.BlockSpec((tc, d), lambda i: (i, 0)),
                  pl.BlockSpec((tc, 1), lambda i: (i, 0)),
                  pl.BlockSpec((tc, 1), lambda i: (i, 0)),
                  pl.BlockSpec((1, d), lambda i: (0, 0)),
                  pl.BlockSpec((1, d), lambda i: (0, 0)),
                  pl.BlockSpec(memory_space=pl.ANY)],
        out_specs=pl.BlockSpec((tc, d), lambda i: (i, 0)),
        out_shape=jax.ShapeDtypeStruct((n, d), F32),
        scratch_shapes=[pltpu.VMEM((2, TOP_K, tc, d), F32), pltpu.SemaphoreType.DMA((2,))],
        compiler_params=_params("arbitrary"),
        name="combine_ln",
    )(d1b, d2b, d1b, d2b, x2d, g1.reshape(n, 1), g2.reshape(n, 1), lg, lb, y)


def kernel(x, ev_w_in, ev_w_pool, ev_pool_scale, ev_lam_q1, ev_lam_k1, ev_lam_q2, ev_lam_k2, ev_subln_g, ev_w_out, ev_ln1_g, ev_ln1_b, ev_ffn_wg, ev_ffn_wu, ev_ffn_wd, ev_ln2_g, ev_ln2_b, od_w_in, od_conv_w, od_w_out, od_ln1_g, od_ln1_b, od_router, od_exp_wg, od_exp_wu, od_exp_wd, od_ln2_g, od_ln2_b):
    batch, seq, d = x.shape
    n = batch * seq
    assert ev_w_in.shape[0] == 1 and od_w_in.shape[0] == 1, "DEPTH == 2: one even and one odd layer"
    h = x.reshape(n, d)
    bf = lambda w: w.astype(BF16)
    row = lambda v: v.reshape(1, -1)

    pool_width = ev_w_pool.shape[1] * ev_w_pool.shape[2]
    u_pool, qkv = _even_inproj(h, bf(ev_w_in[0]), pool_width)
    a = _pool_mixer(u_pool, bf(ev_w_pool[0]), row(ev_pool_scale[0]), seq)
    o = _diff_attention(qkv, row(ev_lam_q1[0]), row(ev_lam_k1[0]), row(ev_lam_q2[0]), row(ev_lam_k2[0]),
                        row(ev_subln_g[0]), batch, seq)
    h = _outproj_ln([a, o], bf(ev_w_out[0]), h, row(ev_ln1_g[0]), row(ev_ln1_b[0]))
    h = _ffn_ln(h, bf(ev_ffn_wg[0]), bf(ev_ffn_wu[0]), bf(ev_ffn_wd[0]), row(ev_ln2_g[0]), row(ev_ln2_b[0]))

    gated = _odd_inproj(h, bf(od_w_in[0]), od_conv_w[0].reshape(CONV_K, -1), seq)
    n_exp = od_router.shape[-1]
    w_router = jnp.pad(od_router[0], ((0, 0), (0, ROUTER_LANES - n_exp)))
    h, logits = _outproj_ln([gated], bf(od_w_out[0]), h, row(od_ln1_g[0]), row(od_ln1_b[0]), w_router)

    tm = _tile(n, 1024)
    n_tiles = (TOP_K * n) // tm + n_exp
    d1, d2, g1, g2, te, nact, zfill, half2 = _route(logits[:, :n_exp].T, tm, n_tiles)
    fill = jnp.concatenate([zfill.reshape(-1), nact.reshape(-1)])
    xs = _dispatch(h, d1, d2, fill, tm, n_tiles)
    y = _moe_ffn(xs, te.reshape(-1), nact.reshape(-1), half2.reshape(-1), od_exp_wg[0], od_exp_wu[0],
                 od_exp_wd[0], tm)
    h = _combine_ln(h, y, d1, d2, g1, g2, row(od_ln2_g[0]), row(od_ln2_b[0]))
    return h.reshape(batch, seq, d)
```

```python
import functools
import math

import jax
import jax.numpy as jnp
from jax import lax
from jax.experimental import pallas as pl
from jax.experimental.pallas import tpu as pltpu

F32 = jnp.float32
BF16 = jnp.bfloat16

DEPTH = 2
ALPHA = (2 * DEPTH) ** 0.25
LN_EPS = 1e-5
RMS_EPS = 1e-5
POOL_WINDOWS = (2, 4, 8, 16)
POOL_HALO = 16
DIFF_HEAD_DIM = 64
HEAD_WIDTH = 2 * DIFF_HEAD_DIM
LAM_INIT_LAYER0 = 0.8 - 0.6 * math.exp(-0.3 * 0)
CONV_K = 3
CONV_HALO = 8
TOP_K = 2
ROUTER_LANES = 128
CUMSUM_CHUNK = 256
DMA_ISSUE_UNROLL = 8

VMEM_LIMIT_BYTES = 56 * 1024 * 1024


def _tile(n, pref):
    t = min(n, pref)
    assert n % t == 0, (n, t)
    return t


def _params(*sem):
    return pltpu.CompilerParams(dimension_semantics=sem, vmem_limit_bytes=VMEM_LIMIT_BYTES)


def _layer_norm(v, g, b):
    mu = jnp.mean(v, axis=-1, keepdims=True)
    d = v - mu
    var = jnp.mean(d * d, axis=-1, keepdims=True)
    return d * lax.rsqrt(var + LN_EPS) * g + b


def _even_inproj_kernel(x_ref, w_ref, pool_ref, qkv_ref, xb_ref):
    j = pl.program_id(1)

    @pl.when(j == 0)
    def _():
        xb_ref[...] = x_ref[...].astype(BF16)

    acc = jnp.dot(xb_ref[...], w_ref[...], preferred_element_type=F32)

    @pl.when(j == 0)
    def _():
        pool_ref[...] = acc

    @pl.when(j > 0)
    def _():
        qkv_ref[...] = acc.astype(BF16)


def _even_inproj(x2d, w_bf16, pool_width):
    n, d = x2d.shape
    width = w_bf16.shape[1]
    tn = pool_width
    assert (width - pool_width) % tn == 0
    tm = _tile(n, 1024)
    return pl.pallas_call(
        _even_inproj_kernel,
        grid=(n // tm, width // tn),
        in_specs=[pl.BlockSpec((tm, d), lambda i, j: (i, 0)),
                  pl.BlockSpec((d, tn), lambda i, j: (0, j))],
        out_specs=[pl.BlockSpec((tm, tn), lambda i, j: (i, 0)),
                   pl.BlockSpec((tm, tn), lambda i, j: (i, jnp.maximum(j - 1, 0)))],
        out_shape=[jax.ShapeDtypeStruct((n, pool_width), F32),
                   jax.ShapeDtypeStruct((n, width - pool_width), BF16)],
        scratch_shapes=[pltpu.VMEM((tm, d), BF16)],
        compiler_params=_params("arbitrary", "arbitrary"),
        name="even_inproj",
    )(x2d, w_bf16)


def _pool_kernel(u_ref, halo_ref, w_ref, scale_ref, o_ref, buf_ref, *, seq, tp, cg):
    i = pl.program_id(0)
    first = (i * tp) % seq == 0
    pos = (i * tp) % seq + lax.broadcasted_iota(jnp.int32, (tp, 1), 0)
    for g, window in enumerate(POOL_WINDOWS):
        cols = slice(g * cg, (g + 1) * cg)
        u = u_ref[:, cols]
        buf_ref[0:POOL_HALO, :] = jnp.where(first, 0.0, halo_ref[:, cols])
        buf_ref[POOL_HALO:, :] = u
        total = u
        for back in range(1, window):
            total = total + buf_ref[pl.ds(POOL_HALO - back, tp), :]
        count = jnp.minimum(pos + 1, window).astype(F32)
        pooled = total / count - u
        mixed = jnp.dot(pooled.astype(BF16), w_ref[g], preferred_element_type=F32)
        o_ref[:, cols] = (mixed * scale_ref[:, cols]).astype(BF16)


def _pool_mixer(u, w_pool_bf16, scale, seq):
    n, pw = u.shape
    groups, cg, _ = w_pool_bf16.shape
    assert groups == len(POOL_WINDOWS) and groups * cg == pw
    tp = _tile(seq, 512)
    halo_blocks = tp // POOL_HALO
    kern = functools.partial(_pool_kernel, seq=seq, tp=tp, cg=cg)
    return pl.pallas_call(
        kern,
        grid=(n // tp,),
        in_specs=[pl.BlockSpec((tp, pw), lambda i: (i, 0)),
                  pl.BlockSpec((POOL_HALO, pw), lambda i: (jnp.maximum(i * halo_blocks - 1, 0), 0)),
                  pl.BlockSpec((groups, cg, cg), lambda i: (0, 0, 0)),
                  pl.BlockSpec((1, pw), lambda i: (0, 0))],
        out_specs=pl.BlockSpec((tp, pw), lambda i: (i, 0)),
        out_shape=jax.ShapeDtypeStruct((n, pw), BF16),
        scratch_shapes=[pltpu.VMEM((POOL_HALO + tp, cg), F32)],
        compiler_params=_params("arbitrary"),
        name="pool_mixer",
    )(u, u, w_pool_bf16, scale)


def _diff_attn_kernel(q_ref, k_ref, v_ref, lq1_ref, lk1_ref, lq2_ref, lk2_ref, g_ref, o_ref,
                      qs_ref, m_ref, acc_ref, sa_ref, sb_ref, *, tq):
    qi = pl.program_id(2)
    q = q_ref[...]
    lane = lax.broadcasted_iota(jnp.int32, q.shape, 1)
    zero = jnp.zeros_like(q)
    scale = jnp.asarray(DIFF_HEAD_DIM ** -0.5, BF16)
    qs_ref[0] = jnp.where(lane < DIFF_HEAD_DIM, q, zero) * scale
    qs_ref[1] = jnp.where(lane >= DIFF_HEAD_DIM, q, zero) * scale
    m_ref[...] = jnp.full(m_ref.shape, -jnp.inf, F32)
    acc_ref[...] = jnp.zeros(acc_ref.shape, F32)
    ones = jnp.ones((tq, HEAD_WIDTH), BF16)
    n_chunks = tq // HEAD_WIDTH

    def scores(kb, s_ref):
        k = k_ref[pl.ds(pl.multiple_of(kb * tq, tq), tq), :]
        for h in range(2):
            s_ref[h] = lax.dot_general(qs_ref[h], k, (((1,), (1,)), ((), ())), preferred_element_type=F32)

    def step(kb, s_ref, masked):
        vx = jnp.concatenate([v_ref[pl.ds(pl.multiple_of(kb * tq, tq), tq), :], ones], axis=1)
        for h in range(2):
            s = s_ref[h]
            if masked:
                r = lax.broadcasted_iota(jnp.int32, (tq, tq), 0)
                c = lax.broadcasted_iota(jnp.int32, (tq, tq), 1)
                s = jnp.where(c <= r, s, -jnp.inf)
            chunks = [s[:, i * HEAD_WIDTH:(i + 1) * HEAD_WIDTH] for i in range(n_chunks)]
            m_blk = functools.reduce(jnp.maximum, chunks)
            m_old = m_ref[h]
            m_new = jnp.maximum(m_old, jnp.max(m_blk, axis=-1, keepdims=True))
            scale_old = jnp.exp(m_old - m_new)
            p = jnp.concatenate([jnp.exp(ch - m_new) for ch in chunks], axis=1).astype(BF16)
            pv = jnp.dot(p, vx, preferred_element_type=F32)
            acc_ref[h] = jnp.concatenate([scale_old, scale_old], axis=1) * acc_ref[h] + pv
            m_ref[h] = m_new

    def pair(t, carry):
        scores(2 * t + 1, sb_ref)
        step(2 * t, sa_ref, False)
        scores(2 * t + 2, sa_ref)
        step(2 * t + 1, sb_ref, False)
        return carry

    scores(0, sa_ref)
    lax.fori_loop(0, qi // 2, pair, 0)

    @pl.when(qi % 2 == 0)
    def _():
        step(qi, sa_ref, True)

    @pl.when(qi % 2 == 1)
    def _():
        scores(qi, sb_ref)
        step(qi - 1, sa_ref, False)
        step(qi, sb_ref, True)

    lam = (jnp.exp(jnp.sum(lq1_ref[...] * lk1_ref[...], keepdims=True))
           - jnp.exp(jnp.sum(lq2_ref[...] * lk2_ref[...], keepdims=True)) + LAM_INIT_LAYER0)
    o = (acc_ref[0, :, 0:HEAD_WIDTH] / acc_ref[0, :, HEAD_WIDTH:]
         - lam * (acc_ref[1, :, 0:HEAD_WIDTH] / acc_ref[1, :, HEAD_WIDTH:]))
    o = o * lax.rsqrt(jnp.mean(o * o, axis=-1, keepdims=True) + RMS_EPS)
    o_ref[...] = (o * g_ref[...] * (1.0 - LAM_INIT_LAYER0)).astype(BF16)


def _diff_attention(qkv, lq1, lk1, lq2, lk2, subln_g, batch, seq):
    n, w3 = qkv.shape
    width = w3 // 3
    heads = width // HEAD_WIDTH
    tq = _tile(seq, 512)
    nq = seq // tq
    kern = functools.partial(_diff_attn_kernel, tq=tq)
    vec = lambda: pl.BlockSpec((1, DIFF_HEAD_DIM), lambda b, h, i: (0, 0))
    return pl.pallas_call(
        kern,
        grid=(batch, heads, nq),
        in_specs=[pl.BlockSpec((tq, HEAD_WIDTH), lambda b, h, i: (b * nq + i, h)),
                  pl.BlockSpec((seq, HEAD_WIDTH), lambda b, h, i: (b, heads + h)),
                  pl.BlockSpec((seq, HEAD_WIDTH), lambda b, h, i: (b, 2 * heads + h)),
                  vec(), vec(), vec(), vec(),
                  pl.BlockSpec((1, HEAD_WIDTH), lambda b, h, i: (0, 0))],
        out_specs=pl.BlockSpec((tq, HEAD_WIDTH), lambda b, h, i: (b * nq + i, h)),
        out_shape=jax.ShapeDtypeStruct((n, width), BF16),
        scratch_shapes=[pltpu.VMEM((2, tq, HEAD_WIDTH), BF16),
                        pltpu.VMEM((2, tq, HEAD_WIDTH), F32),
                        pltpu.VMEM((2, tq, 2 * HEAD_WIDTH), F32),
                        pltpu.VMEM((2, tq, tq), F32), pltpu.VMEM((2, tq, tq), F32)],
        compiler_params=_params("arbitrary", "arbitrary", "arbitrary"),
        name="diff_attention",
    )(qkv, qkv, qkv, lq1, lk1, lq2, lk2, subln_g)


def _outproj_ln_kernel(*refs, n_lhs, with_router):
    lhs_refs = refs[:n_lhs]
    w_ref, x_ref, g_ref, b_ref = refs[n_lhs:n_lhs + 4]
    rest = refs[n_lhs + 4:]
    y = None
    off = 0
    for a_ref in lhs_refs:
        kp = a_ref.shape[1]
        part = jnp.dot(a_ref[...], w_ref[off:off + kp, :], preferred_element_type=F32)
        y = part if y is None else y + part
        off += kp
    out = _layer_norm(ALPHA * x_ref[...] + y, g_ref[...], b_ref[...])
    if with_router:
        wr_ref, o_ref, logit_ref = rest
        o_ref[...] = out
        out_hi = out.astype(BF16)
        out_lo = (out - out_hi.astype(F32)).astype(BF16)
        hi_both = jnp.dot(out_hi, wr_ref[...], preferred_element_type=F32)
        lo_hi = jnp.dot(out_lo, wr_ref[:, 0:ROUTER_LANES], preferred_element_type=F32)
        logit_ref[...] = hi_both[:, 0:ROUTER_LANES] + (hi_both[:, ROUTER_LANES:] + lo_hi)
    else:
        (o_ref,) = rest
        o_ref[...] = out


def _outproj_ln(lhs_list, w_bf16, x2d, g, b, w_router_padded=None):
    n, d = x2d.shape
    tm = _tile(n, 512)
    with_router = w_router_padded is not None
    in_specs = [pl.BlockSpec((tm, a.shape[1]), lambda i: (i, 0)) for a in lhs_list]
    in_specs += [pl.BlockSpec(w_bf16.shape, lambda i: (0, 0)),
                 pl.BlockSpec((tm, d), lambda i: (i, 0)),
                 pl.BlockSpec((1, d), lambda i: (0, 0)),
                 pl.BlockSpec((1, d), lambda i: (0, 0))]
    out_specs = [pl.BlockSpec((tm, d), lambda i: (i, 0))]
    out_shape = [jax.ShapeDtypeStruct((n, d), F32)]
    args = list(lhs_list) + [w_bf16, x2d, g, b]
    if with_router:
        wr_hi = w_router_padded.astype(BF16)
        wr_lo = (w_router_padded - wr_hi.astype(F32)).astype(BF16)
        wr = jnp.concatenate([wr_hi, wr_lo], axis=1)
        in_specs.append(pl.BlockSpec(wr.shape, lambda i: (0, 0)))
        out_specs.append(pl.BlockSpec((tm, ROUTER_LANES), lambda i: (i, 0)))
        out_shape.append(jax.ShapeDtypeStruct((n, ROUTER_LANES), F32))
        args.append(wr)
    kern = functools.partial(_outproj_ln_kernel, n_lhs=len(lhs_list), with_router=with_router)
    res = pl.pallas_call(
        kern,
        grid=(n // tm,),
        in_specs=in_specs,
        out_specs=out_specs,
        out_shape=out_shape,
        compiler_params=_params("arbitrary"),
        name="outproj_ln_router" if with_router else "outproj_ln",
    )(*args)
    return res if with_router else res[0]


def _swiglu_block(xb, wg, wu, wd):
    gate = jnp.dot(xb, wg, preferred_element_type=F32)
    up = jnp.dot(xb, wu, preferred_element_type=F32)
    h = gate / (1.0 + jnp.exp(-gate)) * up
    return jnp.dot(h.astype(BF16), wd, preferred_element_type=F32)


def _ffn_ln_kernel(x_ref, wg_ref, wu_ref, wd_ref, g_ref, b_ref, o_ref, xb_ref, acc_ref):
    j = pl.program_id(1)

    @pl.when(j == 0)
    def _():
        xb_ref[...] = x_ref[...].astype(BF16)
        acc_ref[...] = jnp.zeros_like(acc_ref)

    acc_ref[...] += _swiglu_block(xb_ref[...], wg_ref[...], wu_ref[...], wd_ref[...])

    @pl.when(j == pl.num_programs(1) - 1)
    def _():
        o_ref[...] = _layer_norm(ALPHA * x_ref[...] + acc_ref[...], g_ref[...], b_ref[...])


def _ffn_ln(x2d, wg, wu, wd, g, b):
    n, d = x2d.shape
    ff = wg.shape[1]
    tm = _tile(n, 512)
    tf = _tile(ff, 512)
    return pl.pallas_call(
        _ffn_ln_kernel,
        grid=(n // tm, ff // tf),
        in_specs=[pl.BlockSpec((tm, d), lambda i, j: (i, 0)),
                  pl.BlockSpec((d, tf), lambda i, j: (0, j)),
                  pl.BlockSpec((d, tf), lambda i, j: (0, j)),
                  pl.BlockSpec((tf, d), lambda i, j: (j, 0)),
                  pl.BlockSpec((1, d), lambda i, j: (0, 0)),
                  pl.BlockSpec((1, d), lambda i, j: (0, 0))],
        out_specs=pl.BlockSpec((tm, d), lambda i, j: (i, 0)),
        out_shape=jax.ShapeDtypeStruct((n, d), F32),
        scratch_shapes=[pltpu.VMEM((tm, d), BF16), pltpu.VMEM((tm, d), F32)],
        compiler_params=_params("arbitrary", "arbitrary"),
        name="ffn_ln",
    )(x2d, wg, wu, wd, g, b)


def _odd_inproj_kernel(x_ref, wb_ref, wc_ref, wh_ref, cw_ref, o_ref, xb_ref, zs_ref, carry_ref,
                       *, seq, tm):
    i = pl.program_id(0)
    j = pl.program_id(1)

    @pl.when(j == 0)
    def _():
        xb_ref[...] = x_ref[...].astype(BF16)

    xb = xb_ref[...]
    b_gate = jnp.dot(xb, wb_ref[...], preferred_element_type=F32)
    c_gate = jnp.dot(xb, wc_ref[...], preferred_element_type=F32)
    hx = jnp.dot(xb, wh_ref[...], preferred_element_type=F32)
    z = c_gate * hx
    first = (i * tm) % seq == 0
    zs_ref[0:CONV_HALO, :] = jnp.where(first, 0.0, carry_ref[j])
    zs_ref[CONV_HALO:, :] = z
    carry_ref[j] = z[tm - CONV_HALO:, :]
    y = (cw_ref[0:1, :] * zs_ref[pl.ds(CONV_HALO - 2, tm), :]
         + cw_ref[1:2, :] * zs_ref[pl.ds(CONV_HALO - 1, tm), :]
         + cw_ref[2:3, :] * z)
    o_ref[...] = (b_gate * y).astype(BF16)


def _odd_inproj(x2d, w_bf16, conv_w, seq):
    n, d = x2d.shape
    cw = w_bf16.shape[1] // 3
    tm = _tile(seq, 512)
    tn = _tile(cw, 512)
    nj = cw // tn
    kern = functools.partial(_odd_inproj_kernel, seq=seq, tm=tm)
    return pl.pallas_call(
        kern,
        grid=(n // tm, nj),
        in_specs=[pl.BlockSpec((tm, d), lambda i, j: (i, 0)),
                  pl.BlockSpec((d, tn), lambda i, j: (0, j)),
                  pl.BlockSpec((d, tn), lambda i, j: (0, nj + j)),
                  pl.BlockSpec((d, tn), lambda i, j: (0, 2 * nj + j)),
                  pl.BlockSpec((CONV_K, tn), lambda i, j: (0, j))],
        out_specs=pl.BlockSpec((tm, tn), lambda i, j: (i, j)),
        out_shape=jax.ShapeDtypeStruct((n, cw), BF16),
        scratch_shapes=[pltpu.VMEM((tm, d), BF16),
                        pltpu.VMEM((CONV_HALO + tm, tn), F32),
                        pltpu.VMEM((nj, CONV_HALO, tn), F32)],
        compiler_params=_params("arbitrary", "arbitrary"),
        name="odd_inproj",
    )(x2d, w_bf16, w_bf16, w_bf16, conv_w)


def _route_kernel(lt_ref, d1_ref, d2_ref, g1_ref, g2_ref, te_ref, nact_ref, zfill_ref, half2_ref, rank_ref,
                  *, tm):
    e, n = lt_ref.shape
    lt = lt_ref[...]
    eidx = lax.broadcasted_iota(jnp.int32, (e, n), 0)
    v1 = jnp.max(lt, axis=0, keepdims=True)
    i1 = jnp.min(jnp.where(lt == v1, eidx, e), axis=0, keepdims=True)
    oh1 = eidx == i1
    lt2 = jnp.where(oh1, -jnp.inf, lt)
    v2 = jnp.max(lt2, axis=0, keepdims=True)
    i2 = jnp.min(jnp.where(lt2 == v2, eidx, e), axis=0, keepdims=True)
    oh2 = eidx == i2
    ex = jnp.exp(v2 - v1)
    g1_ref[...] = 1.0 / (1.0 + ex)
    g2_ref[...] = ex / (1.0 + ex)

    sel = jnp.where(oh1 | oh2, 1.0, 0.0)
    a = lax.broadcasted_iota(jnp.int32, (CUMSUM_CHUNK, CUMSUM_CHUNK), 0)
    b = lax.broadcasted_iota(jnp.int32, (CUMSUM_CHUNK, CUMSUM_CHUNK), 1)
    upper = jnp.where(a <= b, 1.0, 0.0).astype(BF16)
    running = jnp.zeros((e, 1), F32)
    for c in range(n // CUMSUM_CHUNK):
        cols = slice(c * CUMSUM_CHUNK, (c + 1) * CUMSUM_CHUNK)
        blk = sel[:, cols]
        incl = jnp.dot(blk.astype(BF16), upper, preferred_element_type=F32)
        rank_ref[:, cols] = incl - blk + running
        running = running + incl[:, CUMSUM_CHUNK - 1:CUMSUM_CHUNK]

    log2_tm = tm.bit_length() - 1
    assert tm == 1 << log2_tm
    counts = running.astype(jnp.int32)
    padded = ((counts + (tm - 1)) >> log2_tm) << log2_tm
    e8 = lax.broadcasted_iota(jnp.int32, (e, 1), 0)
    offs = jnp.zeros((e, 1), jnp.int32)
    total = jnp.zeros((1, 1), jnp.int32)
    for k in range(e):
        offs = jnp.where(e8 == k, total, offs)
        total = total + padded[k:k + 1, :]
    ends = offs + padded
    zfill_ref[...] = jnp.where(padded > 0, ends - tm, -1)

    dest = rank_ref[...].astype(jnp.int32) + offs
    d1_ref[...] = jnp.sum(jnp.where(oh1, dest, 0), axis=0, keepdims=True)
    d2_ref[...] = jnp.sum(jnp.where(oh2, dest, 0), axis=0, keepdims=True)

    t = te_ref.shape[1]
    starts = lax.broadcasted_iota(jnp.int32, (e, t), 1) * tm
    te = jnp.sum(jnp.where(ends <= starts, 1, 0), axis=0, keepdims=True)
    te_ref[...] = jnp.minimum(te, e - 1)
    nact_ref[...] = total >> log2_tm
    in_group = (offs <= starts) & (starts < ends)
    half2 = in_group & (starts + tm // 2 < offs + counts)
    half2_ref[...] = jnp.sum(jnp.where(half2, 1, 0), axis=0, keepdims=True)


def _route(logits_t, tm, n_tiles):
    e, n = logits_t.shape
    assert n % CUMSUM_CHUNK == 0
    kern = functools.partial(_route_kernel, tm=tm)
    row_i = jax.ShapeDtypeStruct((1, n), jnp.int32)
    row_f = jax.ShapeDtypeStruct((1, n), F32)
    return pl.pallas_call(
        kern,
        out_shape=[row_i, row_i, row_f, row_f,
                   jax.ShapeDtypeStruct((1, n_tiles), jnp.int32),
                   jax.ShapeDtypeStruct((1, 1), jnp.int32),
                   jax.ShapeDtypeStruct((e, 1), jnp.int32),
                   jax.ShapeDtypeStruct((1, n_tiles), jnp.int32)],
        scratch_shapes=[pltpu.VMEM((e, n), F32)],
        compiler_params=pltpu.CompilerParams(vmem_limit_bytes=VMEM_LIMIT_BYTES),
        name="route",
    )(logits_t)


def _dispatch_kernel(fill_ref, d1_ref, d2_ref, x_ref, xs_ref, zero_ref, sem, zsem, *, td, tm, n_exp, n_tiles):
    @pl.when(pl.program_id(0) == 0)
    def _():
        zero_ref[...] = jnp.zeros_like(zero_ref)

        def fill(row0):
            cp = pltpu.make_async_copy(zero_ref, xs_ref.at[pl.ds(pl.multiple_of(row0, tm), tm), :], zsem)
            cp.start()
            cp.wait()

        for e in range(n_exp):
            pl.when(fill_ref[e] >= 0)(functools.partial(fill, fill_ref[e]))
        for t in range(n_tiles - n_exp, n_tiles):
            pl.when(t >= fill_ref[n_exp])(functools.partial(fill, t * tm))

    def row_copy(r, dst_row):
        return pltpu.make_async_copy(x_ref.at[pl.ds(r, 1), :], xs_ref.at[pl.ds(dst_row, 1), :], sem)

    def issue(r, carry):
        row_copy(r, d1_ref[0, r]).start()
        row_copy(r, d2_ref[0, r]).start()
        return carry

    lax.fori_loop(0, td, issue, 0, unroll=DMA_ISSUE_UNROLL)
    for _ in range(TOP_K):
        pltpu.make_async_copy(x_ref, xs_ref.at[pl.ds(0, td), :], sem).wait()


def _dispatch(x2d, d1, d2, fill, tm, n_tiles):
    n, d = x2d.shape
    n_exp = fill.shape[0] - 1
    td = _tile(n, 256)
    nb = n // td
    kern = functools.partial(_dispatch_kernel, td=td, tm=tm, n_exp=n_exp, n_tiles=n_tiles)
    smem_idx = lambda: pl.BlockSpec((None, 1, td), lambda i: (i, 0, 0), memory_space=pltpu.SMEM)
    return pl.pallas_call(
        kern,
        grid=(nb,),
        in_specs=[pl.BlockSpec(memory_space=pltpu.SMEM), smem_idx(), smem_idx(),
                  pl.BlockSpec((td, d), lambda i: (i, 0))],
        out_specs=pl.BlockSpec(memory_space=pl.ANY),
        out_shape=jax.ShapeDtypeStruct((n_tiles * tm, d), F32),
        scratch_shapes=[pltpu.VMEM((tm, d), F32), pltpu.SemaphoreType.DMA, pltpu.SemaphoreType.DMA],
        compiler_params=_params("arbitrary"),
        name="dispatch",
    )(fill, d1.reshape(nb, 1, td), d2.reshape(nb, 1, td), x2d)


def _moe_kernel(te_ref, nact_ref, half2_ref, xs_hbm, wg_ref, wu_ref, wd_ref, y_ref, stage_ref, xb_ref, sem,
                *, tm):
    i = pl.program_id(0)
    j = pl.program_id(1)
    nact = nact_ref[0]
    active = i < nact
    half = tm // 2

    def tile_copy(t):
        return pltpu.make_async_copy(xs_hbm.at[pl.ds(pl.multiple_of(t * tm, tm), tm), :], stage_ref, sem)

    @pl.when(j == 0)
    def _():
        y_ref[...] = jnp.zeros_like(y_ref)

    @pl.when((i == 0) & (j == 0))
    def _():
        tile_copy(0).start()

    @pl.when(active & (j == 0))
    def _():
        tile_copy(i).wait()
        xb_ref[...] = stage_ref[...].astype(BF16)

    @pl.when((j == 1) & (i + 1 < nact))
    def _():
        tile_copy(i + 1).start()

    def accumulate(rows):
        y_ref[rows, :] += _swiglu_block(xb_ref[rows, :], wg_ref[...].astype(BF16), wu_ref[...].astype(BF16),
                                        wd_ref[...].astype(BF16))

    @pl.when(active & (half2_ref[i] > 0))
    def _():
        accumulate(slice(None))

    @pl.when(active & (half2_ref[i] == 0))
    def _():
        accumulate(slice(0, half))


def _moe_ffn(xs, te, nact, half2, wg, wu, wd, tm):
    rows, d = xs.shape
    n_exp, _, ff = wg.shape
    tf = _tile(ff, 256)
    nj = ff // tf
    assert nj >= 2
    n_tiles = rows // tm

    def ff_blk(i, j, nact_ref):
        return jnp.where(i < nact_ref[0], j, nj - 1)

    def expert(i, te_ref, nact_ref):
        return te_ref[jnp.minimum(i, nact_ref[0] - 1)]

    grid_spec = pltpu.PrefetchScalarGridSpec(
        num_scalar_prefetch=3,
        grid=(n_tiles, nj),
        in_specs=[pl.BlockSpec(memory_space=pl.ANY),
                  pl.BlockSpec((None, d, tf), lambda i, j, t, a, h: (expert(i, t, a), 0, ff_blk(i, j, a))),
                  pl.BlockSpec((None, d, tf), lambda i, j, t, a, h: (expert(i, t, a), 0, ff_blk(i, j, a))),
                  pl.BlockSpec((None, tf, d), lambda i, j, t, a, h: (expert(i, t, a), ff_blk(i, j, a), 0))],
        out_specs=pl.BlockSpec((tm, d), lambda i, j, t, a, h: (i, 0)),
        scratch_shapes=[pltpu.VMEM((tm, d), F32), pltpu.VMEM((tm, d), BF16), pltpu.SemaphoreType.DMA],
    )
    return pl.pallas_call(
        functools.partial(_moe_kernel, tm=tm),
        grid_spec=grid_spec,
        out_shape=jax.ShapeDtypeStruct((rows, d), F32),
        compiler_params=_params("arbitrary", "arbitrary"),
        name="moe_ffn",
    )(te, nact, half2, xs, wg, wu, wd)


def _combine_kernel(d1_ref, d2_ref, d1n_ref, d2n_ref, x_ref, g1_ref, g2_ref, lg_ref, lb_ref, y_ref, o_ref,
                    buf_ref, sems, *, tc):
    i = pl.program_id(0)

    def gather(da_ref, db_ref, s):
        def issue(r, carry):
            for k, dk_ref in enumerate((da_ref, db_ref)):
                pltpu.make_async_copy(y_ref.at[pl.ds(dk_ref[0, r], 1), :],
                                      buf_ref.at[s, k, pl.ds(r, 1), :], sems.at[s]).start()
            return carry

        lax.fori_loop(0, tc, issue, 0, unroll=DMA_ISSUE_UNROLL)

    @pl.when(i == 0)
    def _():
        gather(d1_ref, d2_ref, 0)

    def tile(s):
        @pl.when(i + 1 < pl.num_programs(0))
        def _():
            gather(d1n_ref, d2n_ref, 1 - s)

        for k in range(TOP_K):
            pltpu.make_async_copy(y_ref.at[pl.ds(0, tc), :], buf_ref.at[s, k], sems.at[s]).wait()

        f = g1_ref[...] * buf_ref[s, 0] + g2_ref[...] * buf_ref[s, 1]
        o_ref[...] = _layer_norm(ALPHA * x_ref[...] + f, lg_ref[...], lb_ref[...])

    pl.when(i % 2 == 0)(functools.partial(tile, 0))
    pl.when(i % 2 == 1)(functools.partial(tile, 1))


def _combine_ln(x2d, y, d1, d2, g1, g2, lg, lb):
    n, d = x2d.shape
    tc = _tile(n, 256)
    nb = n // tc
    kern = functools.partial(_combine_kernel, tc=tc)
    cur = lambda: pl.BlockSpec((None, 1, tc), lambda i: (i, 0, 0), memory_space=pltpu.SMEM)
    nxt = lambda: pl.BlockSpec((None, 1, tc), lambda i: (jnp.minimum(i + 1, nb - 1), 0, 0),
                               memory_space=pltpu.SMEM)
    d1b, d2b = d1.reshape(nb, 1, tc), d2.reshape(nb, 1, tc)
    return pl.pallas_call(
        kern,
        grid=(nb,),
        in_specs=[cur(), cur(), nxt(), nxt(),
                  pl.BlockSpec((tc, d), lambda i: (i, 0)),
                  pl.BlockSpec((tc, 1), lambda i: (i, 0)),
                  pl.BlockSpec((tc, 1), lambda i: (i, 0)),
                  pl.BlockSpec((1, d), lambda i: (0, 0)),
                  pl.BlockSpec((1, d), lambda i: (0, 0)),
                  pl.BlockSpec(memory_space=pl.ANY)],
        out_specs=pl.BlockSpec((tc, d), lambda i: (i, 0)),
        out_shape=jax.ShapeDtypeStruct((n, d), F32),
        scratch_shapes=[pltpu.VMEM((2, TOP_K, tc, d), F32), pltpu.SemaphoreType.DMA((2,))],
        compiler_params=_params("arbitrary"),
        name="combine_ln",
    )(d1b, d2b, d1b, d2b, x2d, g1.reshape(n, 1), g2.reshape(n, 1), lg, lb, y)


def kernel(x, ev_w_in, ev_w_pool, ev_pool_scale, ev_lam_q1, ev_lam_k1, ev_lam_q2, ev_lam_k2, ev_subln_g, ev_w_out, ev_ln1_g, ev_ln1_b, ev_ffn_wg, ev_ffn_wu, ev_ffn_wd, ev_ln2_g, ev_ln2_b, od_w_in, od_conv_w, od_w_out, od_ln1_g, od_ln1_b, od_router, od_exp_wg, od_exp_wu, od_exp_wd, od_ln2_g, od_ln2_b):
    batch, seq, d = x.shape
    n = batch * seq
    assert ev_w_in.shape[0] == 1 and od_w_in.shape[0] == 1, "DEPTH == 2: one even and one odd layer"
    h = x.reshape(n, d)
    bf = lambda w: w.astype(BF16)
    row = lambda v: v.reshape(1, -1)

    pool_width = ev_w_pool.shape[1] * ev_w_pool.shape[2]
    u_pool, qkv = _even_inproj(h, bf(ev_w_in[0]), pool_width)
    a = _pool_mixer(u_pool, bf(ev_w_pool[0]), row(ev_pool_scale[0]), seq)
    o = _diff_attention(qkv, row(ev_lam_q1[0]), row(ev_lam_k1[0]), row(ev_lam_q2[0]), row(ev_lam_k2[0]),
                        row(ev_subln_g[0]), batch, seq)
    h = _outproj_ln([a, o], bf(ev_w_out[0]), h, row(ev_ln1_g[0]), row(ev_ln1_b[0]))
    h = _ffn_ln(h, bf(ev_ffn_wg[0]), bf(ev_ffn_wu[0]), bf(ev_ffn_wd[0]), row(ev_ln2_g[0]), row(ev_ln2_b[0]))

    gated = _odd_inproj(h, bf(od_w_in[0]), od_conv_w[0].reshape(CONV_K, -1), seq)
    n_exp = od_router.shape[-1]
    w_router = jnp.pad(od_router[0], ((0, 0), (0, ROUTER_LANES - n_exp)))
    h, logits = _outproj_ln([gated], bf(od_w_out[0]), h, row(od_ln1_g[0]), row(od_ln1_b[0]), w_router)

    tm = _tile(n, 1024)
    n_tiles = (TOP_K * n) // tm + n_exp
    d1, d2, g1, g2, te, nact, zfill, half2 = _route(logits[:, :n_exp].T, tm, n_tiles)
    fill = jnp.concatenate([zfill.reshape(-1), nact.reshape(-1)])
    xs = _dispatch(h, d1, d2, fill, tm, n_tiles)
    y = _moe_ffn(xs, te.reshape(-1), nact.reshape(-1), half2.reshape(-1), od_exp_wg[0], od_exp_wu[0],
                 od_exp_wd[0], tm)
    h = _combine_ln(h, y, d1, d2, g1, g2, row(od_ln2_g[0]), row(od_ln2_b[0]))
    return h.reshape(batch, seq, d)
```

```python
import functools
import math

import jax
import jax.numpy as jnp
from jax import lax
from jax.experimental import pallas as pl
from jax.experimental.pallas import tpu as pltpu

F32 = jnp.float32
BF16 = jnp.bfloat16

DEPTH = 2
ALPHA = (2 * DEPTH) ** 0.25
LN_EPS = 1e-5
RMS_EPS = 1e-5
POOL_WINDOWS = (2, 4, 8, 16)
POOL_HALO = 16
DIFF_HEAD_DIM = 64
HEAD_WIDTH = 2 * DIFF_HEAD_DIM
LAM_INIT_LAYER0 = 0.8 - 0.6 * math.exp(-0.3 * 0)
CONV_K = 3
CONV_HALO = 8
TOP_K = 2
ROUTER_LANES = 128
CUMSUM_CHUNK = 256
ODD_COL_GROUPS = 2
OUTPROJ_ROW_GROUPS = 2
FFN_COL_GROUPS = 2
DMA_ISSUE_UNROLL = 8

VMEM_LIMIT_BYTES = 56 * 1024 * 1024
BIG_TILE_VMEM_LIMIT_BYTES = 61 * 1024 * 1024


def _tile(n, pref):
    t = min(n, pref)
    assert n % t == 0, (n, t)
    return t


def _params(*sem):
    return pltpu.CompilerParams(dimension_semantics=sem, vmem_limit_bytes=VMEM_LIMIT_BYTES)


def _layer_norm(v, g, b):
    mu = jnp.mean(v, axis=-1, keepdims=True)
    d = v - mu
    var = jnp.mean(d * d, axis=-1, keepdims=True)
    return d * lax.rsqrt(var + LN_EPS) * g + b


def _even_inproj_kernel(x_ref, w_ref, pool_ref, qkv_ref, xb_ref):
    j = pl.program_id(1)

    @pl.when(j == 0)
    def _():
        xb_ref[...] = x_ref[...].astype(BF16)

    acc = jnp.dot(xb_ref[...], w_ref[...], preferred_element_type=F32)

    @pl.when(j == 0)
    def _():
        pool_ref[...] = acc

    @pl.when(j > 0)
    def _():
        qkv_ref[...] = acc.astype(BF16)


def _even_inproj(x2d, w_bf16, pool_width):
    n, d = x2d.shape
    width = w_bf16.shape[1]
    tn = pool_width
    assert (width - pool_width) % tn == 0
    tm = _tile(n, 1024)
    return pl.pallas_call(
        _even_inproj_kernel,
        grid=(n // tm, width // tn),
        in_specs=[pl.BlockSpec((tm, d), lambda i, j: (i, 0)),
                  pl.BlockSpec((d, tn), lambda i, j: (0, j))],
        out_specs=[pl.BlockSpec((tm, tn), lambda i, j: (i, 0)),
                   pl.BlockSpec((tm, tn), lambda i, j: (i, jnp.maximum(j - 1, 0)))],
        out_shape=[jax.ShapeDtypeStruct((n, pool_width), F32),
                   jax.ShapeDtypeStruct((n, width - pool_width), BF16)],
        scratch_shapes=[pltpu.VMEM((tm, d), BF16)],
        compiler_params=_params("arbitrary", "arbitrary"),
        name="even_inproj",
    )(x2d, w_bf16)


def _pool_kernel(u_ref, halo_ref, w_ref, scale_ref, o_ref, buf_ref, *, seq, tp, cg):
    i = pl.program_id(0)
    first = (i * tp) % seq == 0
    pos = (i * tp) % seq + lax.broadcasted_iota(jnp.int32, (tp, 1), 0)
    for g, window in enumerate(POOL_WINDOWS):
        cols = slice(g * cg, (g + 1) * cg)
        u = u_ref[:, cols]
        buf_ref[0:POOL_HALO, :] = jnp.where(first, 0.0, halo_ref[:, cols])
        buf_ref[POOL_HALO:, :] = u
        total = u
        for back in range(1, window):
            total = total + buf_ref[pl.ds(POOL_HALO - back, tp), :]
        count = jnp.minimum(pos + 1, window).astype(F32)
        pooled = total / count - u
        mixed = jnp.dot(pooled.astype(BF16), w_ref[g], preferred_element_type=F32)
        o_ref[:, cols] = (mixed * scale_ref[:, cols]).astype(BF16)


def _pool_mixer(u, w_pool_bf16, scale, seq):
    n, pw = u.shape
    groups, cg, _ = w_pool_bf16.shape
    assert groups == len(POOL_WINDOWS) and groups * cg == pw
    tp = _tile(seq, 512)
    halo_blocks = tp // POOL_HALO
    kern = functools.partial(_pool_kernel, seq=seq, tp=tp, cg=cg)
    return pl.pallas_call(
        kern,
        grid=(n // tp,),
        in_specs=[pl.BlockSpec((tp, pw), lambda i: (i, 0)),
                  pl.BlockSpec((POOL_HALO, pw), lambda i: (jnp.maximum(i * halo_blocks - 1, 0), 0)),
                  pl.BlockSpec((groups, cg, cg), lambda i: (0, 0, 0)),
                  pl.BlockSpec((1, pw), lambda i: (0, 0))],
        out_specs=pl.BlockSpec((tp, pw), lambda i: (i, 0)),
        out_shape=jax.ShapeDtypeStruct((n, pw), BF16),
        scratch_shapes=[pltpu.VMEM((POOL_HALO + tp, cg), F32)],
        compiler_params=_params("arbitrary"),
        name="pool_mixer",
    )(u, u, w_pool_bf16, scale)


def _diff_attn_kernel(q_ref, k_ref, v_ref, lq1_ref, lk1_ref, lq2_ref, lk2_ref, g_ref, o_ref,
                      qs_ref, m_ref, acc_ref, sa_ref, sb_ref, *, tq):
    qi = pl.program_id(2)
    q = q_ref[...]
    lane = lax.broadcasted_iota(jnp.int32, q.shape, 1)
    zero = jnp.zeros_like(q)
    scale = jnp.asarray(DIFF_HEAD_DIM ** -0.5, BF16)
    qs_ref[0] = jnp.where(lane < DIFF_HEAD_DIM, q, zero) * scale
    qs_ref[1] = jnp.where(lane >= DIFF_HEAD_DIM, q, zero) * scale
    m_ref[...] = jnp.full(m_ref.shape, -jnp.inf, F32)
    acc_ref[...] = jnp.zeros(acc_ref.shape, F32)
    ones = jnp.ones((tq, HEAD_WIDTH), BF16)
    n_chunks = tq // HEAD_WIDTH

    def scores(kb, s_ref):
        k = k_ref[pl.ds(pl.multiple_of(kb * tq, tq), tq), :]
        for h in range(2):
            s_ref[h] = lax.dot_general(qs_ref[h], k, (((1,), (1,)), ((), ())), preferred_element_type=F32)

    def step(kb, s_ref, masked):
        vx = jnp.concatenate([v_ref[pl.ds(pl.multiple_of(kb * tq, tq), tq), :], ones], axis=1)
        for h in range(2):
            s = s_ref[h]
            if masked:
                r = lax.broadcasted_iota(jnp.int32, (tq, tq), 0)
                c = lax.broadcasted_iota(jnp.int32, (tq, tq), 1)
                s = jnp.where(c <= r, s, -jnp.inf)
            chunks = [s[:, i * HEAD_WIDTH:(i + 1) * HEAD_WIDTH] for i in range(n_chunks)]
            m_blk = functools.reduce(jnp.maximum, chunks)
            m_old = m_ref[h]
            m_new = jnp.maximum(m_old, jnp.max(m_blk, axis=-1, keepdims=True))
            scale_old = jnp.exp(m_old - m_new)
            p = jnp.concatenate([jnp.exp(ch - m_new) for ch in chunks], axis=1).astype(BF16)
            pv = jnp.dot(p, vx, preferred_element_type=F32)
            acc_ref[h] = jnp.concatenate([scale_old, scale_old], axis=1) * acc_ref[h] + pv
            m_ref[h] = m_new

    def pair(t, carry):
        scores(2 * t + 1, sb_ref)
        step(2 * t, sa_ref, False)
        scores(2 * t + 2, sa_ref)
        step(2 * t + 1, sb_ref, False)
        return carry

    scores(0, sa_ref)
    lax.fori_loop(0, qi // 2, pair, 0)

    @pl.when(qi % 2 == 0)
    def _():
        step(qi, sa_ref, True)

    @pl.when(qi % 2 == 1)
    def _():
        scores(qi, sb_ref)
        step(qi - 1, sa_ref, False)
        step(qi, sb_ref, True)

    lam = (jnp.exp(jnp.sum(lq1_ref[...] * lk1_ref[...], keepdims=True))
           - jnp.exp(jnp.sum(lq2_ref[...] * lk2_ref[...], keepdims=True)) + LAM_INIT_LAYER0)
    o = (acc_ref[0, :, 0:HEAD_WIDTH] / acc_ref[0, :, HEAD_WIDTH:]
         - lam * (acc_ref[1, :, 0:HEAD_WIDTH] / acc_ref[1, :, HEAD_WIDTH:]))
    o = o * lax.rsqrt(jnp.mean(o * o, axis=-1, keepdims=True) + RMS_EPS)
    o_ref[...] = (o * g_ref[...] * (1.0 - LAM_INIT_LAYER0)).astype(BF16)


def _diff_attention(qkv, lq1, lk1, lq2, lk2, subln_g, batch, seq):
    n, w3 = qkv.shape
    width = w3 // 3
    heads = width // HEAD_WIDTH
    tq = _tile(seq, 512)
    nq = seq // tq
    kern = functools.partial(_diff_attn_kernel, tq=tq)
    vec = lambda: pl.BlockSpec((1, DIFF_HEAD_DIM), lambda b, h, i: (0, 0))
    return pl.pallas_call(
        kern,
        grid=(batch, heads, nq),
        in_specs=[pl.BlockSpec((tq, HEAD_WIDTH), lambda b, h, i: (b * nq + i, h)),
                  pl.BlockSpec((seq, HEAD_WIDTH), lambda b, h, i: (b, heads + h)),
                  pl.BlockSpec((seq, HEAD_WIDTH), lambda b, h, i: (b, 2 * heads + h)),
                  vec(), vec(), vec(), vec(),
                  pl.BlockSpec((1, HEAD_WIDTH), lambda b, h, i: (0, 0))],
        out_specs=pl.BlockSpec((tq, HEAD_WIDTH), lambda b, h, i: (b * nq + i, h)),
        out_shape=jax.ShapeDtypeStruct((n, width), BF16),
        scratch_shapes=[pltpu.VMEM((2, tq, HEAD_WIDTH), BF16),
                        pltpu.VMEM((2, tq, HEAD_WIDTH), F32),
                        pltpu.VMEM((2, tq, 2 * HEAD_WIDTH), F32),
                        pltpu.VMEM((2, tq, tq), F32), pltpu.VMEM((2, tq, tq), F32)],
        compiler_params=_params("arbitrary", "arbitrary", "arbitrary"),
        name="diff_attention",
    )(qkv, qkv, qkv, lq1, lk1, lq2, lk2, subln_g)


def _outproj_ln_kernel(*refs, n_lhs, with_router):
    lhs_refs = refs[:n_lhs]
    w_ref, x_ref, g_ref, b_ref = refs[n_lhs:n_lhs + 4]
    rest = refs[n_lhs + 4:]
    if with_router:
        wr_ref, o_ref, logit_ref = rest
    else:
        (o_ref,) = rest
    tm = x_ref.shape[0]
    n_groups = 1 if with_router else OUTPROJ_ROW_GROUPS
    group = tm // n_groups
    for r in range(n_groups):
        rows = slice(r * group, (r + 1) * group)
        y = None
        off = 0
        for a_ref in lhs_refs:
            kp = a_ref.shape[1]
            part = jnp.dot(a_ref[rows, :], w_ref[off:off + kp, :], preferred_element_type=F32)
            y = part if y is None else y + part
            off += kp
        out = _layer_norm(ALPHA * x_ref[rows, :] + y, g_ref[...], b_ref[...])
        o_ref[rows, :] = out
        if with_router:
            out_hi = out.astype(BF16)
            out_lo = (out - out_hi.astype(F32)).astype(BF16)
            hi_both = jnp.dot(out_hi, wr_ref[...], preferred_element_type=F32)
            lo_hi = jnp.dot(out_lo, wr_ref[:, 0:ROUTER_LANES], preferred_element_type=F32)
            logit_ref[rows, :] = hi_both[:, 0:ROUTER_LANES] + (hi_both[:, ROUTER_LANES:] + lo_hi)


def _outproj_ln(lhs_list, w_bf16, x2d, g, b, w_router_padded=None):
    n, d = x2d.shape
    tm = _tile(n, 512)
    with_router = w_router_padded is not None
    in_specs = [pl.BlockSpec((tm, a.shape[1]), lambda i: (i, 0)) for a in lhs_list]
    in_specs += [pl.BlockSpec(w_bf16.shape, lambda i: (0, 0)),
                 pl.BlockSpec((tm, d), lambda i: (i, 0)),
                 pl.BlockSpec((1, d), lambda i: (0, 0)),
                 pl.BlockSpec((1, d), lambda i: (0, 0))]
    out_specs = [pl.BlockSpec((tm, d), lambda i: (i, 0))]
    out_shape = [jax.ShapeDtypeStruct((n, d), F32)]
    args = list(lhs_list) + [w_bf16, x2d, g, b]
    if with_router:
        wr_hi = w_router_padded.astype(BF16)
        wr_lo = (w_router_padded - wr_hi.astype(F32)).astype(BF16)
        wr = jnp.concatenate([wr_hi, wr_lo], axis=1)
        in_specs.append(pl.BlockSpec(wr.shape, lambda i: (0, 0)))
        out_specs.append(pl.BlockSpec((tm, ROUTER_LANES), lambda i: (i, 0)))
        out_shape.append(jax.ShapeDtypeStruct((n, ROUTER_LANES), F32))
        args.append(wr)
    kern = functools.partial(_outproj_ln_kernel, n_lhs=len(lhs_list), with_router=with_router)
    res = pl.pallas_call(
        kern,
        grid=(n // tm,),
        in_specs=in_specs,
        out_specs=out_specs,
        out_shape=out_shape,
        compiler_params=_params("arbitrary"),
        name="outproj_ln_router" if with_router else "outproj_ln",
    )(*args)
    return res if with_router else res[0]


def _swiglu_hidden(xb, wg, wu):
    gate = jnp.dot(xb, wg, preferred_element_type=F32)
    up = jnp.dot(xb, wu, preferred_element_type=F32)
    return (gate / (1.0 + jnp.exp(-gate)) * up).astype(BF16)


def _swiglu_block(xb, wg, wu, wd):
    return jnp.dot(_swiglu_hidden(xb, wg, wu), wd, preferred_element_type=F32)


def _ffn_ln_kernel(x_ref, wg_ref, wu_ref, wd_ref, g_ref, b_ref, o_ref, xb_ref):
    j = pl.program_id(1)

    @pl.when(j == 0)
    def _():
        xb_ref[...] = x_ref[...].astype(BF16)
        o_ref[...] = jnp.zeros_like(o_ref)

    xb = xb_ref[...]
    tf = wg_ref.shape[1]
    group = tf // FFN_COL_GROUPS
    hidden = [_swiglu_hidden(xb, wg_ref[:, c * group:(c + 1) * group], wu_ref[:, c * group:(c + 1) * group])
              for c in range(FFN_COL_GROUPS)]
    o_ref[...] += jnp.dot(jnp.concatenate(hidden, axis=1), wd_ref[...], preferred_element_type=F32)

    @pl.when(j == pl.num_programs(1) - 1)
    def _():
        o_ref[...] = _layer_norm(ALPHA * x_ref[...] + o_ref[...], g_ref[...], b_ref[...])


def _ffn_ln(x2d, wg, wu, wd, g, b):
    n, d = x2d.shape
    ff = wg.shape[1]
    tm = _tile(n, 1024)
    tf = _tile(ff, 512)
    return pl.pallas_call(
        _ffn_ln_kernel,
        grid=(n // tm, ff // tf),
        in_specs=[pl.BlockSpec((tm, d), lambda i, j: (i, 0)),
                  pl.BlockSpec((d, tf), lambda i, j: (0, j)),
                  pl.BlockSpec((d, tf), lambda i, j: (0, j)),
                  pl.BlockSpec((tf, d), lambda i, j: (j, 0)),
                  pl.BlockSpec((1, d), lambda i, j: (0, 0)),
                  pl.BlockSpec((1, d), lambda i, j: (0, 0))],
        out_specs=pl.BlockSpec((tm, d), lambda i, j: (i, 0)),
        out_shape=jax.ShapeDtypeStruct((n, d), F32),
        scratch_shapes=[pltpu.VMEM((tm, d), BF16)],
        compiler_params=pltpu.CompilerParams(dimension_semantics=("arbitrary", "arbitrary"),
                                             vmem_limit_bytes=BIG_TILE_VMEM_LIMIT_BYTES),
        name="ffn_ln",
    )(x2d, wg, wu, wd, g, b)


def _odd_inproj_kernel(x_ref, wb_ref, wc_ref, wh_ref, cw_ref, o_ref, xb_ref, zs_ref, carry_ref,
                       *, seq, tm):
    i = pl.program_id(0)
    j = pl.program_id(1)

    @pl.when(j == 0)
    def _():
        xb_ref[...] = x_ref[...].astype(BF16)

    xb = xb_ref[...]
    first = (i * tm) % seq == 0
    tn = o_ref.shape[1]
    group = tn // ODD_COL_GROUPS
    for c in range(ODD_COL_GROUPS):
        cols = slice(c * group, (c + 1) * group)
        b_gate = jnp.dot(xb, wb_ref[:, cols], preferred_element_type=F32)
        c_gate = jnp.dot(xb, wc_ref[:, cols], preferred_element_type=F32)
        hx = jnp.dot(xb, wh_ref[:, cols], preferred_element_type=F32)
        z = c_gate * hx
        zs_ref[0:CONV_HALO, cols] = jnp.where(first, 0.0, carry_ref[j, :, cols])
        zs_ref[CONV_HALO:, cols] = z
        carry_ref[j, :, cols] = z[tm - CONV_HALO:, :]
        y = (cw_ref[0:1, cols] * zs_ref[pl.ds(CONV_HALO - 2, tm), cols]
             + cw_ref[1:2, cols] * zs_ref[pl.ds(CONV_HALO - 1, tm), cols]
             + cw_ref[2:3, cols] * z)
        o_ref[:, cols] = (b_gate * y).astype(BF16)


def _odd_inproj(x2d, w_bf16, conv_w, seq):
    n, d = x2d.shape
    cw = w_bf16.shape[1] // 3
    tm = _tile(seq, 512)
    tn = _tile(cw, 512)
    nj = cw // tn
    kern = functools.partial(_odd_inproj_kernel, seq=seq, tm=tm)
    return pl.pallas_call(
        kern,
        grid=(n // tm, nj),
        in_specs=[pl.BlockSpec((tm, d), lambda i, j: (i, 0)),
                  pl.BlockSpec((d, tn), lambda i, j: (0, j)),
                  pl.BlockSpec((d, tn), lambda i, j: (0, nj + j)),
                  pl.BlockSpec((d, tn), lambda i, j: (0, 2 * nj + j)),
                  pl.BlockSpec((CONV_K, tn), lambda i, j: (0, j))],
        out_specs=pl.BlockSpec((tm, tn), lambda i, j: (i, j)),
        out_shape=jax.ShapeDtypeStruct((n, cw), BF16),
        scratch_shapes=[pltpu.VMEM((tm, d), BF16),
                        pltpu.VMEM((CONV_HALO + tm, tn), F32),
                        pltpu.VMEM((nj, CONV_HALO, tn), F32)],
        compiler_params=_params("arbitrary", "arbitrary"),
        name="odd_inproj",
    )(x2d, w_bf16, w_bf16, w_bf16, conv_w)


def _route_kernel(lt_ref, d1_ref, d2_ref, g1_ref, g2_ref, te_ref, nact_ref, zfill_ref, half2_ref, rank_ref,
                  *, tm):
    e, n = lt_ref.shape
    lt = lt_ref[...]
    eidx = lax.broadcasted_iota(jnp.int32, (e, n), 0)
    v1 = jnp.max(lt, axis=0, keepdims=True)
    i1 = jnp.min(jnp.where(lt == v1, eidx, e), axis=0, keepdims=True)
    oh1 = eidx == i1
    lt2 = jnp.where(oh1, -jnp.inf, lt)
    v2 = jnp.max(lt2, axis=0, keepdims=True)
    i2 = jnp.min(jnp.where(lt2 == v2, eidx, e), axis=0, keepdims=True)
    oh2 = eidx == i2
    ex = jnp.exp(v2 - v1)
    g1_ref[...] = 1.0 / (1.0 + ex)
    g2_ref[...] = ex / (1.0 + ex)

    sel = jnp.where(oh1 | oh2, 1.0, 0.0)
    a = lax.broadcasted_iota(jnp.int32, (CUMSUM_CHUNK, CUMSUM_CHUNK), 0)
    b = lax.broadcasted_iota(jnp.int32, (CUMSUM_CHUNK, CUMSUM_CHUNK), 1)
    upper = jnp.where(a <= b, 1.0, 0.0).astype(BF16)
    running = jnp.zeros((e, 1), F32)
    for c in range(n // CUMSUM_CHUNK):
        cols = slice(c * CUMSUM_CHUNK, (c + 1) * CUMSUM_CHUNK)
        blk = sel[:, cols]
        incl = jnp.dot(blk.astype(BF16), upper, preferred_element_type=F32)
        rank_ref[:, cols] = incl - blk + running
        running = running + incl[:, CUMSUM_CHUNK - 1:CUMSUM_CHUNK]

    log2_tm = tm.bit_length() - 1
    assert tm == 1 << log2_tm
    counts = running.astype(jnp.int32)
    padded = ((counts + (tm - 1)) >> log2_tm) << log2_tm
    e8 = lax.broadcasted_iota(jnp.int32, (e, 1), 0)
    offs = jnp.zeros((e, 1), jnp.int32)
    total = jnp.zeros((1, 1), jnp.int32)
    for k in range(e):
        offs = jnp.where(e8 == k, total, offs)
        total = total + padded[k:k + 1, :]
    ends = offs + padded
    zfill_ref[...] = jnp.where(padded > 0, ends - tm, -1)

    dest = rank_ref[...].astype(jnp.int32) + offs
    d1_ref[...] = jnp.sum(jnp.where(oh1, dest, 0), axis=0, keepdims=True)
    d2_ref[...] = jnp.sum(jnp.where(oh2, dest, 0), axis=0, keepdims=True)

    t = te_ref.shape[1]
    starts = lax.broadcasted_iota(jnp.int32, (e, t), 1) * tm
    te = jnp.sum(jnp.where(ends <= starts, 1, 0), axis=0, keepdims=True)
    te_ref[...] = jnp.minimum(te, e - 1)
    nact_ref[...] = total >> log2_tm
    in_group = (offs <= starts) & (starts < ends)
    half2 = in_group & (starts + tm // 2 < offs + counts)
    half2_ref[...] = jnp.sum(jnp.where(half2, 1, 0), axis=0, keepdims=True)


def _route(logits_t, tm, n_tiles):
    e, n = logits_t.shape
    assert n % CUMSUM_CHUNK == 0
    kern = functools.partial(_route_kernel, tm=tm)
    row_i = jax.ShapeDtypeStruct((1, n), jnp.int32)
    row_f = jax.ShapeDtypeStruct((1, n), F32)
    return pl.pallas_call(
        kern,
        out_shape=[row_i, row_i, row_f, row_f,
                   jax.ShapeDtypeStruct((1, n_tiles), jnp.int32),
                   jax.ShapeDtypeStruct((1, 1), jnp.int32),
                   jax.ShapeDtypeStruct((e, 1), jnp.int32),
                   jax.ShapeDtypeStruct((1, n_tiles), jnp.int32)],
        scratch_shapes=[pltpu.VMEM((e, n), F32)],
        compiler_params=pltpu.CompilerParams(vmem_limit_bytes=VMEM_LIMIT_BYTES),
        name="route",
    )(logits_t)


def _dispatch_kernel(fill_ref, d1_ref, d2_ref, x_ref, xs_ref, zero_ref, sem, zsem, *, td, tm, n_exp, n_tiles):
    @pl.when(pl.program_id(0) == 0)
    def _():
        zero_ref[...] = jnp.zeros_like(zero_ref)

        def fill(row0):
            cp = pltpu.make_async_copy(zero_ref, xs_ref.at[pl.ds(pl.multiple_of(row0, tm), tm), :], zsem)
            cp.start()
            cp.wait()

        for e in range(n_exp):
            pl.when(fill_ref[e] >= 0)(functools.partial(fill, fill_ref[e]))
        for t in range(n_tiles - n_exp, n_tiles):
            pl.when(t >= fill_ref[n_exp])(functools.partial(fill, t * tm))

    def row_copy(r, dst_row):
        return pltpu.make_async_copy(x_ref.at[pl.ds(r, 1), :], xs_ref.at[pl.ds(dst_row, 1), :], sem)

    def issue(r, carry):
        row_copy(r, d1_ref[0, r]).start()
        row_copy(r, d2_ref[0, r]).start()
        return carry

    lax.fori_loop(0, td, issue, 0, unroll=DMA_ISSUE_UNROLL)
    for _ in range(TOP_K):
        pltpu.make_async_copy(x_ref, xs_ref.at[pl.ds(0, td), :], sem).wait()


def _dispatch(x2d, d1, d2, fill, tm, n_tiles):
    n, d = x2d.shape
    n_exp = fill.shape[0] - 1
    td = _tile(n, 256)
    nb = n // td
    kern = functools.partial(_dispatch_kernel, td=td, tm=tm, n_exp=n_exp, n_tiles=n_tiles)
    smem_idx = lambda: pl.BlockSpec((None, 1, td), lambda i: (i, 0, 0), memory_space=pltpu.SMEM)
    return pl.pallas_call(
        kern,
        grid=(nb,),
        in_specs=[pl.BlockSpec(memory_space=pltpu.SMEM), smem_idx(), smem_idx(),
                  pl.BlockSpec((td, d), lambda i: (i, 0))],
        out_specs=pl.BlockSpec(memory_space=pl.ANY),
        out_shape=jax.ShapeDtypeStruct((n_tiles * tm, d), F32),
        scratch_shapes=[pltpu.VMEM((tm, d), F32), pltpu.SemaphoreType.DMA, pltpu.SemaphoreType.DMA],
        compiler_params=_params("arbitrary"),
        name="dispatch",
    )(fill, d1.reshape(nb, 1, td), d2.reshape(nb, 1, td), x2d)


def _moe_kernel(te_ref, nact_ref, half2_ref, xs_hbm, wg_ref, wu_ref, wd_ref, y_ref, stage_ref, xb_ref, sem,
                *, tm):
    i = pl.program_id(0)
    j = pl.program_id(1)
    nact = nact_ref[0]
    active = i < nact
    half = tm // 2

    def tile_copy(t):
        return pltpu.make_async_copy(xs_hbm.at[pl.ds(pl.multiple_of(t * tm, tm), tm), :], stage_ref, sem)

    @pl.when(j == 0)
    def _():
        y_ref[...] = jnp.zeros_like(y_ref)

    @pl.when((i == 0) & (j == 0))
    def _():
        tile_copy(0).start()

    @pl.when(active & (j == 0))
    def _():
        tile_copy(i).wait()
        xb_ref[...] = stage_ref[...].astype(BF16)

    @pl.when((j == 1) & (i + 1 < nact))
    def _():
        tile_copy(i + 1).start()

    def accumulate(rows):
        y_ref[rows, :] += _swiglu_block(xb_ref[rows, :], wg_ref[...].astype(BF16), wu_ref[...].astype(BF16),
                                        wd_ref[...].astype(BF16))

    @pl.when(active & (half2_ref[i] > 0))
    def _():
        accumulate(slice(None))

    @pl.when(active & (half2_ref[i] == 0))
    def _():
        accumulate(slice(0, half))


def _moe_ffn(xs, te, nact, half2, wg, wu, wd, tm):
    rows, d = xs.shape
    n_exp, _, ff = wg.shape
    tf = _tile(ff, 512)
    nj = ff // tf
    assert nj >= 2
    n_tiles = rows // tm

    def ff_blk(i, j, nact_ref):
        return jnp.where(i < nact_ref[0], j, nj - 1)

    def expert(i, te_ref, nact_ref):
        return te_ref[jnp.minimum(i, nact_ref[0] - 1)]

    grid_spec = pltpu.PrefetchScalarGridSpec(
        num_scalar_prefetch=3,
        grid=(n_tiles, nj),
        in_specs=[pl.BlockSpec(memory_space=pl.ANY),
                  pl.BlockSpec((None, d, tf), lambda i, j, t, a, h: (expert(i, t, a), 0, ff_blk(i, j, a))),
                  pl.BlockSpec((None, d, tf), lambda i, j, t, a, h: (expert(i, t, a), 0, ff_blk(i, j, a))),
                  pl.BlockSpec((None, tf, d), lambda i, j, t, a, h: (expert(i, t, a), ff_blk(i, j, a), 0))],
        out_specs=pl.BlockSpec((tm, d), lambda i, j, t, a, h: (i, 0)),
        scratch_shapes=[pltpu.VMEM((tm, d), F32), pltpu.VMEM((tm, d), BF16), pltpu.SemaphoreType.DMA],
    )
    return pl.pallas_call(
        functools.partial(_moe_kernel, tm=tm),
        grid_spec=grid_spec,
        out_shape=jax.ShapeDtypeStruct((rows, d), F32),
        compiler_params=pltpu.CompilerParams(dimension_semantics=("arbitrary", "arbitrary"),
                                             vmem_limit_bytes=BIG_TILE_VMEM_LIMIT_BYTES),
        name="moe_ffn",
    )(te, nact, half2, xs, wg, wu, wd)


def _combine_kernel(d1_ref, d2_ref, d1n_ref, d2n_ref, x_ref, g1_ref, g2_ref, lg_ref, lb_ref, y_ref, o_ref,
                    buf_ref, sems, *, tc):
    i = pl.program_id(0)

    def gather(da_ref, db_ref, s):
        def issue(r, carry):
            for k, dk_ref in enumerate((da_ref, db_ref)):
                pltpu.make_async_copy(y_ref.at[pl.ds(dk_ref[0, r], 1), :],
                                      buf_ref.at[s, k, pl.ds(r, 1), :], sems.at[s]).start()
            return carry

        lax.fori_loop(0, tc, issue, 0, unroll=DMA_ISSUE_UNROLL)

    @pl.when(i == 0)
    def _():
        gather(d1_ref, d2_ref, 0)

    def tile(s):
        @pl.when(i + 1 < pl.num_programs(0))
        def _():
            gather(d1n_ref, d2n_ref, 1 - s)

        for k in range(TOP_K):
            pltpu.make_async_copy(y_ref.at[pl.ds(0, tc), :], buf_ref.at[s, k], sems.at[s]).wait()

        f = g1_ref[...] * buf_ref[s, 0] + g2_ref[...] * buf_ref[s, 1]
        o_ref[...] = _layer_norm(ALPHA * x_ref[...] + f, lg_ref[...], lb_ref[...])

    pl.when(i % 2 == 0)(functools.partial(tile, 0))
    pl.when(i % 2 == 1)(functools.partial(tile, 1))


def _combine_ln(x2d, y, d1, d2, g1, g2, lg, lb):
    n, d = x2d.shape
    tc = _tile(n, 256)
    nb = n // tc
    kern = functools.partial(_combine_kernel, tc=tc)
    cur = lambda: pl.BlockSpec((None, 1, tc), lambda i: (i, 0, 0), memory_space=pltpu.SMEM)
    nxt = lambda: pl.BlockSpec((None, 1, tc), lambda i: (jnp.minimum(i + 1, nb - 1), 0, 0),
                               memory_space=pltpu.SMEM)
    d1b, d2b = d1.reshape(nb, 1, tc), d2.reshape(nb, 1, tc)
    return pl.pallas_call(
        kern,
        grid=(nb,),
        in_specs=[cur(), cur(), nxt(), nxt(),
                  pl.BlockSpec((tc, d), lambda i: (i, 0)),
                  pl.BlockSpec((tc, 1), lambda i: (i, 0)),
                  pl.BlockSpec((tc, 1), lambda i: (i, 0)),
                  pl.BlockSpec((1, d), lambda i: (0, 0)),
                  pl.BlockSpec((1, d), lambda i: (0, 0)),
                  pl.BlockSpec(memory_space=pl.ANY)],
        out_specs=pl.BlockSpec((tc, d), lambda i: (i, 0)),
        out_shape=jax.ShapeDtypeStruct((n, d), F32),
        scratch_shapes=[pltpu.VMEM((2, TOP_K, tc, d), F32), pltpu.SemaphoreType.DMA((2,))],
        compiler_params=_params("arbitrary"),
        name="combine_ln",
    )(d1b, d2b, d1b, d2b, x2d, g1.reshape(n, 1), g2.reshape(n, 1), lg, lb, y)


def kernel(x, ev_w_in, ev_w_pool, ev_pool_scale, ev_lam_q1, ev_lam_k1, ev_lam_q2, ev_lam_k2, ev_subln_g, ev_w_out, ev_ln1_g, ev_ln1_b, ev_ffn_wg, ev_ffn_wu, ev_ffn_wd, ev_ln2_g, ev_ln2_b, od_w_in, od_conv_w, od_w_out, od_ln1_g, od_ln1_b, od_router, od_exp_wg, od_exp_wu, od_exp_wd, od_ln2_g, od_ln2_b):
    batch, seq, d = x.shape
    n = batch * seq
    assert ev_w_in.shape[0] == 1 and od_w_in.shape[0] == 1, "DEPTH == 2: one even and one odd layer"
    h = x.reshape(n, d)
    bf = lambda w: w.astype(BF16)
    row = lambda v: v.reshape(1, -1)

    pool_width = ev_w_pool.shape[1] * ev_w_pool.shape[2]
    u_pool, qkv = _even_inproj(h, bf(ev_w_in[0]), pool_width)
    a = _pool_mixer(u_pool, bf(ev_w_pool[0]), row(ev_pool_scale[0]), seq)
    o = _diff_attention(qkv, row(ev_lam_q1[0]), row(ev_lam_k1[0]), row(ev_lam_q2[0]), row(ev_lam_k2[0]),
                        row(ev_subln_g[0]), batch, seq)
    h = _outproj_ln([a, o], bf(ev_w_out[0]), h, row(ev_ln1_g[0]), row(ev_ln1_b[0]))
    h = _ffn_ln(h, bf(ev_ffn_wg[0]), bf(ev_ffn_wu[0]), bf(ev_ffn_wd[0]), row(ev_ln2_g[0]), row(ev_ln2_b[0]))

    gated = _odd_inproj(h, bf(od_w_in[0]), od_conv_w[0].reshape(CONV_K, -1), seq)
    n_exp = od_router.shape[-1]
    w_router = jnp.pad(od_router[0], ((0, 0), (0, ROUTER_LANES - n_exp)))
    h, logits = _outproj_ln([gated], bf(od_w_out[0]), h, row(od_ln1_g[0]), row(od_ln1_b[0]), w_router)

    tm = _tile(n, 1024)
    n_tiles = (TOP_K * n) // tm + n_exp
    d1, d2, g1, g2, te, nact, zfill, half2 = _route(logits[:, :n_exp].T, tm, n_tiles)
    fill = jnp.concatenate([zfill.reshape(-1), nact.reshape(-1)])
    xs = _dispatch(h, d1, d2, fill, tm, n_tiles)
    y = _moe_ffn(xs, te.reshape(-1), nact.reshape(-1), half2.reshape(-1), od_exp_wg[0], od_exp_wu[0],
                 od_exp_wd[0], tm)
    h = _combine_ln(h, y, d1, d2, g1, g2, row(od_ln2_g[0]), row(od_ln2_b[0]))
    return h.reshape(batch, seq, d)
```

```python
import functools
import math

import jax
import jax.numpy as jnp
from jax import lax
from jax.experimental import pallas as pl
from jax.experimental.pallas import tpu as pltpu

F32 = jnp.float32
BF16 = jnp.bfloat16

DEPTH = 2
ALPHA = (2 * DEPTH) ** 0.25
LN_EPS = 1e-5
RMS_EPS = 1e-5
POOL_WINDOWS = (2, 4, 8, 16)
POOL_HALO = 16
DIFF_HEAD_DIM = 64
HEAD_WIDTH = 2 * DIFF_HEAD_DIM
LAM_INIT_LAYER0 = 0.8 - 0.6 * math.exp(-0.3 * 0)
CONV_K = 3
CONV_HALO = 8
TOP_K = 2
ROUTER_LANES = 128
CUMSUM_CHUNK = 256
ODD_COL_GROUPS = 2
OUTPROJ_ROW_GROUPS = 2
FFN_COL_GROUPS = 2
SCALAR_LOOP_UNROLL = 8
DMA_ISSUE_UNROLL = 8

VMEM_LIMIT_BYTES = 56 * 1024 * 1024
BIG_TILE_VMEM_LIMIT_BYTES = 61 * 1024 * 1024


def _tile(n, pref):
    t = min(n, pref)
    assert n % t == 0, (n, t)
    return t


def _params(*sem):
    return pltpu.CompilerParams(dimension_semantics=sem, vmem_limit_bytes=VMEM_LIMIT_BYTES)


def _layer_norm(v, g, b):
    mu = jnp.mean(v, axis=-1, keepdims=True)
    d = v - mu
    var = jnp.mean(d * d, axis=-1, keepdims=True)
    return d * lax.rsqrt(var + LN_EPS) * g + b


def _even_inproj_kernel(x_ref, w_ref, pool_ref, qkv_ref, xb_ref):
    j = pl.program_id(1)

    @pl.when(j == 0)
    def _():
        xb_ref[...] = x_ref[...].astype(BF16)

    acc = jnp.dot(xb_ref[...], w_ref[...], preferred_element_type=F32)

    @pl.when(j == 0)
    def _():
        pool_ref[...] = acc

    @pl.when(j > 0)
    def _():
        qkv_ref[...] = acc.astype(BF16)


def _even_inproj(x2d, w_bf16, pool_width):
    n, d = x2d.shape
    width = w_bf16.shape[1]
    tn = pool_width
    assert (width - pool_width) % tn == 0
    tm = _tile(n, 1024)
    return pl.pallas_call(
        _even_inproj_kernel,
        grid=(n // tm, width // tn),
        in_specs=[pl.BlockSpec((tm, d), lambda i, j: (i, 0)),
                  pl.BlockSpec((d, tn), lambda i, j: (0, j))],
        out_specs=[pl.BlockSpec((tm, tn), lambda i, j: (i, 0)),
                   pl.BlockSpec((tm, tn), lambda i, j: (i, jnp.maximum(j - 1, 0)))],
        out_shape=[jax.ShapeDtypeStruct((n, pool_width), F32),
                   jax.ShapeDtypeStruct((n, width - pool_width), BF16)],
        scratch_shapes=[pltpu.VMEM((tm, d), BF16)],
        compiler_params=_params("arbitrary", "arbitrary"),
        name="even_inproj",
    )(x2d, w_bf16)


def _pool_kernel(u_ref, halo_ref, w_ref, scale_ref, o_ref, buf_ref, *, seq, tp, cg):
    i = pl.program_id(0)
    first = (i * tp) % seq == 0
    pos = (i * tp) % seq + lax.broadcasted_iota(jnp.int32, (tp, 1), 0)
    for g, window in enumerate(POOL_WINDOWS):
        cols = slice(g * cg, (g + 1) * cg)
        u = u_ref[:, cols]
        buf_ref[0:POOL_HALO, :] = jnp.where(first, 0.0, halo_ref[:, cols])
        buf_ref[POOL_HALO:, :] = u
        total = u
        for back in range(1, window):
            total = total + buf_ref[pl.ds(POOL_HALO - back, tp), :]
        count = jnp.minimum(pos + 1, window).astype(F32)
        pooled = total / count - u
        mixed = jnp.dot(pooled.astype(BF16), w_ref[g], preferred_element_type=F32)
        o_ref[:, cols] = (mixed * scale_ref[:, cols]).astype(BF16)


def _pool_mixer(u, w_pool_bf16, scale, seq):
    n, pw = u.shape
    groups, cg, _ = w_pool_bf16.shape
    assert groups == len(POOL_WINDOWS) and groups * cg == pw
    tp = _tile(seq, 512)
    halo_blocks = tp // POOL_HALO
    kern = functools.partial(_pool_kernel, seq=seq, tp=tp, cg=cg)
    return pl.pallas_call(
        kern,
        grid=(n // tp,),
        in_specs=[pl.BlockSpec((tp, pw), lambda i: (i, 0)),
                  pl.BlockSpec((POOL_HALO, pw), lambda i: (jnp.maximum(i * halo_blocks - 1, 0), 0)),
                  pl.BlockSpec((groups, cg, cg), lambda i: (0, 0, 0)),
                  pl.BlockSpec((1, pw), lambda i: (0, 0))],
        out_specs=pl.BlockSpec((tp, pw), lambda i: (i, 0)),
        out_shape=jax.ShapeDtypeStruct((n, pw), BF16),
        scratch_shapes=[pltpu.VMEM((POOL_HALO + tp, cg), F32)],
        compiler_params=_params("arbitrary"),
        name="pool_mixer",
    )(u, u, w_pool_bf16, scale)


def _diff_attn_kernel(q_ref, k_ref, v_ref, lq1_ref, lk1_ref, lq2_ref, lk2_ref, g_ref, o_ref,
                      qs_ref, m_ref, acc_ref, sa_ref, sb_ref, *, tq):
    qi = pl.program_id(2)
    q = q_ref[...]
    lane = lax.broadcasted_iota(jnp.int32, q.shape, 1)
    zero = jnp.zeros_like(q)
    scale = jnp.asarray(DIFF_HEAD_DIM ** -0.5, BF16)
    qs_ref[0] = jnp.where(lane < DIFF_HEAD_DIM, q, zero) * scale
    qs_ref[1] = jnp.where(lane >= DIFF_HEAD_DIM, q, zero) * scale
    m_ref[...] = jnp.full(m_ref.shape, -jnp.inf, F32)
    acc_ref[...] = jnp.zeros(acc_ref.shape, F32)
    ones = jnp.ones((tq, HEAD_WIDTH), BF16)
    n_chunks = tq // HEAD_WIDTH

    def scores(kb, s_ref):
        k = k_ref[pl.ds(pl.multiple_of(kb * tq, tq), tq), :]
        for h in range(2):
            s_ref[h] = lax.dot_general(qs_ref[h], k, (((1,), (1,)), ((), ())), preferred_element_type=F32)

    def step(kb, s_ref, masked):
        vx = jnp.concatenate([v_ref[pl.ds(pl.multiple_of(kb * tq, tq), tq), :], ones], axis=1)
        for h in range(2):
            s = s_ref[h]
            if masked:
                r = lax.broadcasted_iota(jnp.int32, (tq, tq), 0)
                c = lax.broadcasted_iota(jnp.int32, (tq, tq), 1)
                s = jnp.where(c <= r, s, -jnp.inf)
            chunks = [s[:, i * HEAD_WIDTH:(i + 1) * HEAD_WIDTH] for i in range(n_chunks)]
            m_blk = functools.reduce(jnp.maximum, chunks)
            m_old = m_ref[h]
            m_new = jnp.maximum(m_old, jnp.max(m_blk, axis=-1, keepdims=True))
            scale_old = jnp.exp(m_old - m_new)
            p = jnp.concatenate([jnp.exp(ch - m_new) for ch in chunks], axis=1).astype(BF16)
            pv = jnp.dot(p, vx, preferred_element_type=F32)
            acc_ref[h] = jnp.concatenate([scale_old, scale_old], axis=1) * acc_ref[h] + pv
            m_ref[h] = m_new

    def pair(t, carry):
        scores(2 * t + 1, sb_ref)
        step(2 * t, sa_ref, False)
        scores(2 * t + 2, sa_ref)
        step(2 * t + 1, sb_ref, False)
        return carry

    scores(0, sa_ref)
    lax.fori_loop(0, qi // 2, pair, 0)

    @pl.when(qi % 2 == 0)
    def _():
        step(qi, sa_ref, True)

    @pl.when(qi % 2 == 1)
    def _():
        scores(qi, sb_ref)
        step(qi - 1, sa_ref, False)
        step(qi, sb_ref, True)

    lam = (jnp.exp(jnp.sum(lq1_ref[...] * lk1_ref[...], keepdims=True))
           - jnp.exp(jnp.sum(lq2_ref[...] * lk2_ref[...], keepdims=True)) + LAM_INIT_LAYER0)
    o = (acc_ref[0, :, 0:HEAD_WIDTH] / acc_ref[0, :, HEAD_WIDTH:]
         - lam * (acc_ref[1, :, 0:HEAD_WIDTH] / acc_ref[1, :, HEAD_WIDTH:]))
    o = o * lax.rsqrt(jnp.mean(o * o, axis=-1, keepdims=True) + RMS_EPS)
    o_ref[...] = (o * g_ref[...] * (1.0 - LAM_INIT_LAYER0)).astype(BF16)


def _diff_attention(qkv, lq1, lk1, lq2, lk2, subln_g, batch, seq):
    n, w3 = qkv.shape
    width = w3 // 3
    heads = width // HEAD_WIDTH
    tq = _tile(seq, 512)
    nq = seq // tq
    kern = functools.partial(_diff_attn_kernel, tq=tq)
    vec = lambda: pl.BlockSpec((1, DIFF_HEAD_DIM), lambda b, h, i: (0, 0))
    return pl.pallas_call(
        kern,
        grid=(batch, heads, nq),
        in_specs=[pl.BlockSpec((tq, HEAD_WIDTH), lambda b, h, i: (b * nq + i, h)),
                  pl.BlockSpec((seq, HEAD_WIDTH), lambda b, h, i: (b, heads + h)),
                  pl.BlockSpec((seq, HEAD_WIDTH), lambda b, h, i: (b, 2 * heads + h)),
                  vec(), vec(), vec(), vec(),
                  pl.BlockSpec((1, HEAD_WIDTH), lambda b, h, i: (0, 0))],
        out_specs=pl.BlockSpec((tq, HEAD_WIDTH), lambda b, h, i: (b * nq + i, h)),
        out_shape=jax.ShapeDtypeStruct((n, width), BF16),
        scratch_shapes=[pltpu.VMEM((2, tq, HEAD_WIDTH), BF16),
                        pltpu.VMEM((2, tq, HEAD_WIDTH), F32),
                        pltpu.VMEM((2, tq, 2 * HEAD_WIDTH), F32),
                        pltpu.VMEM((2, tq, tq), F32), pltpu.VMEM((2, tq, tq), F32)],
        compiler_params=_params("arbitrary", "arbitrary", "arbitrary"),
        name="diff_attention",
    )(qkv, qkv, qkv, lq1, lk1, lq2, lk2, subln_g)


def _outproj_ln_kernel(*refs, n_lhs, with_router):
    lhs_refs = refs[:n_lhs]
    w_ref, x_ref, g_ref, b_ref = refs[n_lhs:n_lhs + 4]
    rest = refs[n_lhs + 4:]
    if with_router:
        wr_ref, o_ref, logit_ref = rest
    else:
        (o_ref,) = rest
    tm = x_ref.shape[0]
    n_groups = 1 if with_router else OUTPROJ_ROW_GROUPS
    group = tm // n_groups
    for r in range(n_groups):
        rows = slice(r * group, (r + 1) * group)
        y = None
        off = 0
        for a_ref in lhs_refs:
            kp = a_ref.shape[1]
            part = jnp.dot(a_ref[rows, :], w_ref[off:off + kp, :], preferred_element_type=F32)
            y = part if y is None else y + part
            off += kp
        out = _layer_norm(ALPHA * x_ref[rows, :] + y, g_ref[...], b_ref[...])
        o_ref[rows, :] = out
        if with_router:
            out_hi = out.astype(BF16)
            out_lo = (out - out_hi.astype(F32)).astype(BF16)
            hi_both = jnp.dot(out_hi, wr_ref[...], preferred_element_type=F32)
            lo_hi = jnp.dot(out_lo, wr_ref[:, 0:ROUTER_LANES], preferred_element_type=F32)
            logit_ref[rows, :] = hi_both[:, 0:ROUTER_LANES] + (hi_both[:, ROUTER_LANES:] + lo_hi)


def _outproj_ln(lhs_list, w_bf16, x2d, g, b, w_router_padded=None):
    n, d = x2d.shape
    tm = _tile(n, 512)
    with_router = w_router_padded is not None
    in_specs = [pl.BlockSpec((tm, a.shape[1]), lambda i: (i, 0)) for a in lhs_list]
    in_specs += [pl.BlockSpec(w_bf16.shape, lambda i: (0, 0)),
                 pl.BlockSpec((tm, d), lambda i: (i, 0)),
                 pl.BlockSpec((1, d), lambda i: (0, 0)),
                 pl.BlockSpec((1, d), lambda i: (0, 0))]
    out_specs = [pl.BlockSpec((tm, d), lambda i: (i, 0))]
    out_shape = [jax.ShapeDtypeStruct((n, d), F32)]
    args = list(lhs_list) + [w_bf16, x2d, g, b]
    if with_router:
        wr_hi = w_router_padded.astype(BF16)
        wr_lo = (w_router_padded - wr_hi.astype(F32)).astype(BF16)
        wr = jnp.concatenate([wr_hi, wr_lo], axis=1)
        in_specs.append(pl.BlockSpec(wr.shape, lambda i: (0, 0)))
        out_specs.append(pl.BlockSpec((tm, ROUTER_LANES), lambda i: (i, 0)))
        out_shape.append(jax.ShapeDtypeStruct((n, ROUTER_LANES), F32))
        args.append(wr)
    kern = functools.partial(_outproj_ln_kernel, n_lhs=len(lhs_list), with_router=with_router)
    res = pl.pallas_call(
        kern,
        grid=(n // tm,),
        in_specs=in_specs,
        out_specs=out_specs,
        out_shape=out_shape,
        compiler_params=_params("arbitrary"),
        name="outproj_ln_router" if with_router else "outproj_ln",
    )(*args)
    return res if with_router else res[0]


def _swiglu_hidden(xb, wg, wu):
    gate = jnp.dot(xb, wg, preferred_element_type=F32)
    up = jnp.dot(xb, wu, preferred_element_type=F32)
    return (gate / (1.0 + jnp.exp(-gate)) * up).astype(BF16)


def _swiglu_block(xb, wg, wu, wd):
    return jnp.dot(_swiglu_hidden(xb, wg, wu), wd, preferred_element_type=F32)


def _ffn_ln_kernel(x_ref, wg_ref, wu_ref, wd_ref, g_ref, b_ref, o_ref, xb_ref):
    j = pl.program_id(1)

    @pl.when(j == 0)
    def _():
        xb_ref[...] = x_ref[...].astype(BF16)
        o_ref[...] = jnp.zeros_like(o_ref)

    xb = xb_ref[...]
    tf = wg_ref.shape[1]
    group = tf // FFN_COL_GROUPS
    hidden = [_swiglu_hidden(xb, wg_ref[:, c * group:(c + 1) * group], wu_ref[:, c * group:(c + 1) * group])
              for c in range(FFN_COL_GROUPS)]
    o_ref[...] += jnp.dot(jnp.concatenate(hidden, axis=1), wd_ref[...], preferred_element_type=F32)

    @pl.when(j == pl.num_programs(1) - 1)
    def _():
        o_ref[...] = _layer_norm(ALPHA * x_ref[...] + o_ref[...], g_ref[...], b_ref[...])


def _ffn_ln(x2d, wg, wu, wd, g, b):
    n, d = x2d.shape
    ff = wg.shape[1]
    tm = _tile(n, 1024)
    tf = _tile(ff, 512)
    return pl.pallas_call(
        _ffn_ln_kernel,
        grid=(n // tm, ff // tf),
        in_specs=[pl.BlockSpec((tm, d), lambda i, j: (i, 0)),
                  pl.BlockSpec((d, tf), lambda i, j: (0, j)),
                  pl.BlockSpec((d, tf), lambda i, j: (0, j)),
                  pl.BlockSpec((tf, d), lambda i, j: (j, 0)),
                  pl.BlockSpec((1, d), lambda i, j: (0, 0)),
                  pl.BlockSpec((1, d), lambda i, j: (0, 0))],
        out_specs=pl.BlockSpec((tm, d), lambda i, j: (i, 0)),
        out_shape=jax.ShapeDtypeStruct((n, d), F32),
        scratch_shapes=[pltpu.VMEM((tm, d), BF16)],
        compiler_params=pltpu.CompilerParams(dimension_semantics=("arbitrary", "arbitrary"),
                                             vmem_limit_bytes=BIG_TILE_VMEM_LIMIT_BYTES),
        name="ffn_ln",
    )(x2d, wg, wu, wd, g, b)


def _odd_inproj_kernel(x_ref, wb_ref, wc_ref, wh_ref, cw_ref, o_ref, xb_ref, zs_ref, carry_ref,
                       *, seq, tm):
    i = pl.program_id(0)
    j = pl.program_id(1)

    @pl.when(j == 0)
    def _():
        xb_ref[...] = x_ref[...].astype(BF16)

    xb = xb_ref[...]
    first = (i * tm) % seq == 0
    tn = o_ref.shape[1]
    group = tn // ODD_COL_GROUPS
    for c in range(ODD_COL_GROUPS):
        cols = slice(c * group, (c + 1) * group)
        b_gate = jnp.dot(xb, wb_ref[:, cols], preferred_element_type=F32)
        c_gate = jnp.dot(xb, wc_ref[:, cols], preferred_element_type=F32)
        hx = jnp.dot(xb, wh_ref[:, cols], preferred_element_type=F32)
        z = c_gate * hx
        zs_ref[0:CONV_HALO, cols] = jnp.where(first, 0.0, carry_ref[j, :, cols])
        zs_ref[CONV_HALO:, cols] = z
        carry_ref[j, :, cols] = z[tm - CONV_HALO:, :]
        y = (cw_ref[0:1, cols] * zs_ref[pl.ds(CONV_HALO - 2, tm), cols]
             + cw_ref[1:2, cols] * zs_ref[pl.ds(CONV_HALO - 1, tm), cols]
             + cw_ref[2:3, cols] * z)
        o_ref[:, cols] = (b_gate * y).astype(BF16)


def _odd_inproj(x2d, w_bf16, conv_w, seq):
    n, d = x2d.shape
    cw = w_bf16.shape[1] // 3
    tm = _tile(seq, 512)
    tn = _tile(cw, 512)
    nj = cw // tn
    kern = functools.partial(_odd_inproj_kernel, seq=seq, tm=tm)
    return pl.pallas_call(
        kern,
        grid=(n // tm, nj),
        in_specs=[pl.BlockSpec((tm, d), lambda i, j: (i, 0)),
                  pl.BlockSpec((d, tn), lambda i, j: (0, j)),
                  pl.BlockSpec((d, tn), lambda i, j: (0, nj + j)),
                  pl.BlockSpec((d, tn), lambda i, j: (0, 2 * nj + j)),
                  pl.BlockSpec((CONV_K, tn), lambda i, j: (0, j))],
        out_specs=pl.BlockSpec((tm, tn), lambda i, j: (i, j)),
        out_shape=jax.ShapeDtypeStruct((n, cw), BF16),
        scratch_shapes=[pltpu.VMEM((tm, d), BF16),
                        pltpu.VMEM((CONV_HALO + tm, tn), F32),
                        pltpu.VMEM((nj, CONV_HALO, tn), F32)],
        compiler_params=_params("arbitrary", "arbitrary"),
        name="odd_inproj",
    )(x2d, w_bf16, w_bf16, w_bf16, conv_w)


def _route_kernel(lt_ref, d1_ref, d2_ref, g1_ref, g2_ref, te_ref, nact_ref, half2_ref, rank_ref, *, tm):
    e, n = lt_ref.shape
    lt = lt_ref[...]
    eidx = lax.broadcasted_iota(jnp.int32, (e, n), 0)
    v1 = jnp.max(lt, axis=0, keepdims=True)
    i1 = jnp.min(jnp.where(lt == v1, eidx, e), axis=0, keepdims=True)
    oh1 = eidx == i1
    lt2 = jnp.where(oh1, -jnp.inf, lt)
    v2 = jnp.max(lt2, axis=0, keepdims=True)
    i2 = jnp.min(jnp.where(lt2 == v2, eidx, e), axis=0, keepdims=True)
    oh2 = eidx == i2
    ex = jnp.exp(v2 - v1)
    g1_ref[...] = 1.0 / (1.0 + ex)
    g2_ref[...] = ex / (1.0 + ex)

    sel = jnp.where(oh1 | oh2, 1.0, 0.0)
    a = lax.broadcasted_iota(jnp.int32, (CUMSUM_CHUNK, CUMSUM_CHUNK), 0)
    b = lax.broadcasted_iota(jnp.int32, (CUMSUM_CHUNK, CUMSUM_CHUNK), 1)
    upper = jnp.where(a <= b, 1.0, 0.0).astype(BF16)
    running = jnp.zeros((e, 1), F32)
    for c in range(n // CUMSUM_CHUNK):
        cols = slice(c * CUMSUM_CHUNK, (c + 1) * CUMSUM_CHUNK)
        blk = sel[:, cols]
        incl = jnp.dot(blk.astype(BF16), upper, preferred_element_type=F32)
        rank_ref[:, cols] = incl - blk + running
        running = running + incl[:, CUMSUM_CHUNK - 1:CUMSUM_CHUNK]

    log2_tm = tm.bit_length() - 1
    assert tm == 1 << log2_tm
    counts = running.astype(jnp.int32)
    padded = ((counts + (tm - 1)) >> log2_tm) << log2_tm
    e8 = lax.broadcasted_iota(jnp.int32, (e, 1), 0)
    offs = jnp.zeros((e, 1), jnp.int32)
    total = jnp.zeros((1, 1), jnp.int32)
    for k in range(e):
        offs = jnp.where(e8 == k, total, offs)
        total = total + padded[k:k + 1, :]
    ends = offs + padded

    dest = rank_ref[...].astype(jnp.int32) + offs
    d1_ref[...] = jnp.sum(jnp.where(oh1, dest, 0), axis=0, keepdims=True)
    d2_ref[...] = jnp.sum(jnp.where(oh2, dest, 0), axis=0, keepdims=True)

    t = te_ref.shape[1]
    starts = lax.broadcasted_iota(jnp.int32, (e, t), 1) * tm
    te = jnp.sum(jnp.where(ends <= starts, 1, 0), axis=0, keepdims=True)
    te_ref[...] = jnp.minimum(te, e - 1)
    nact_ref[...] = total >> log2_tm
    in_group = (offs <= starts) & (starts < ends)
    half2 = in_group & (starts + tm // 2 < offs + counts)
    half2_ref[...] = jnp.sum(jnp.where(half2, 1, 0), axis=0, keepdims=True)


def _route(logits_t, tm, n_tiles):
    e, n = logits_t.shape
    assert n % CUMSUM_CHUNK == 0
    kern = functools.partial(_route_kernel, tm=tm)
    row_i = jax.ShapeDtypeStruct((1, n), jnp.int32)
    row_f = jax.ShapeDtypeStruct((1, n), F32)
    return pl.pallas_call(
        kern,
        out_shape=[row_i, row_i, row_f, row_f,
                   jax.ShapeDtypeStruct((1, n_tiles), jnp.int32),
                   jax.ShapeDtypeStruct((1, 1), jnp.int32),
                   jax.ShapeDtypeStruct((1, n_tiles), jnp.int32)],
        scratch_shapes=[pltpu.VMEM((e, n), F32)],
        compiler_params=pltpu.CompilerParams(vmem_limit_bytes=VMEM_LIMIT_BYTES),
        name="route",
    )(logits_t)


def _invert_kernel(d1_ref, d2_ref, src_ref):
    def init(r, carry):
        src_ref[r] = 0
        return carry

    lax.fori_loop(0, src_ref.shape[0], init, 0, unroll=SCALAR_LOOP_UNROLL)

    def scatter(t, carry):
        src_ref[d1_ref[t]] = t
        src_ref[d2_ref[t]] = t
        return carry

    lax.fori_loop(0, d1_ref.shape[0], scatter, 0, unroll=SCALAR_LOOP_UNROLL)


def _invert(d1, d2, rows):
    smem = lambda: pl.BlockSpec(memory_space=pltpu.SMEM)
    return pl.pallas_call(
        _invert_kernel,
        in_specs=[smem(), smem()],
        out_specs=smem(),
        out_shape=jax.ShapeDtypeStruct((rows,), jnp.int32),
        name="invert",
    )(d1.reshape(-1), d2.reshape(-1))


def _moe_kernel(te_ref, nact_ref, half2_ref, src_ref, h_hbm, wg_ref, wu_ref, wd_ref, y_ref, stage_ref, xb_ref, sem,
                *, tm, nj):
    i = pl.program_id(0)
    j = pl.program_id(1)
    nact = nact_ref[0]
    active = i < nact
    half = tm // 2
    per_step = tm // nj
    leftover = tm - per_step * nj

    def row_copy(tile, r):
        return pltpu.make_async_copy(h_hbm.at[pl.ds(src_ref[tile * tm + r], 1), :],
                                     stage_ref.at[pl.ds(r, 1), :], sem)

    def issue_rows(tile, first, count):
        for u in range(count):
            row_copy(tile, first + u).start()

    @pl.when(j == 0)
    def _():
        y_ref[...] = jnp.zeros_like(y_ref)

    @pl.when((i == 0) & (j == 0))
    def _():
        def first_tile(r, carry):
            row_copy(0, r).start()
            return carry

        lax.fori_loop(0, tm, first_tile, 0, unroll=DMA_ISSUE_UNROLL)

    @pl.when((i <= nact) & (j == 0))
    def _():
        pltpu.make_async_copy(h_hbm.at[pl.ds(0, tm), :], stage_ref, sem).wait()

    @pl.when(active & (j == 0))
    def _():
        xb_ref[...] = stage_ref[...].astype(BF16)
        issue_rows(i + 1, per_step * nj, leftover)

    def accumulate(rows):
        issue_rows(i + 1, j * per_step, per_step)
        y_ref[rows, :] += _swiglu_block(xb_ref[rows, :], wg_ref[...].astype(BF16), wu_ref[...].astype(BF16),
                                        wd_ref[...].astype(BF16))

    @pl.when(active & (half2_ref[i] > 0))
    def _():
        accumulate(slice(None))

    @pl.when(active & (half2_ref[i] == 0))
    def _():
        accumulate(slice(0, half))


def _moe_ffn(h2d, src, te, nact, half2, wg, wu, wd, tm):
    n, d = h2d.shape
    rows = src.shape[0]
    n_exp, _, ff = wg.shape
    tf = _tile(ff, 512)
    nj = ff // tf
    n_tiles = rows // tm
    assert n >= tm

    def ff_blk(i, j, nact_ref):
        return jnp.where(i < nact_ref[0], j, nj - 1)

    def expert(i, te_ref, nact_ref):
        return te_ref[jnp.minimum(i, nact_ref[0] - 1)]

    grid_spec = pltpu.PrefetchScalarGridSpec(
        num_scalar_prefetch=4,
        grid=(n_tiles, nj),
        in_specs=[pl.BlockSpec(memory_space=pl.ANY),
                  pl.BlockSpec((None, d, tf), lambda i, j, t, a, h, s: (expert(i, t, a), 0, ff_blk(i, j, a))),
                  pl.BlockSpec((None, d, tf), lambda i, j, t, a, h, s: (expert(i, t, a), 0, ff_blk(i, j, a))),
                  pl.BlockSpec((None, tf, d), lambda i, j, t, a, h, s: (expert(i, t, a), ff_blk(i, j, a), 0))],
        out_specs=pl.BlockSpec((tm, d), lambda i, j, t, a, h, s: (i, 0)),
        scratch_shapes=[pltpu.VMEM((tm, d), F32), pltpu.VMEM((tm, d), BF16), pltpu.SemaphoreType.DMA],
    )
    return pl.pallas_call(
        functools.partial(_moe_kernel, tm=tm, nj=nj),
        grid_spec=grid_spec,
        out_shape=jax.ShapeDtypeStruct((rows, d), F32),
        compiler_params=pltpu.CompilerParams(dimension_semantics=("arbitrary", "arbitrary"),
                                             vmem_limit_bytes=BIG_TILE_VMEM_LIMIT_BYTES),
        name="moe_ffn",
    )(te, nact, half2, src, h2d, wg, wu, wd)


def _combine_kernel(d1_ref, d2_ref, d1n_ref, d2n_ref, x_ref, g1_ref, g2_ref, lg_ref, lb_ref, y_ref, o_ref,
                    buf_ref, sems, *, tc):
    i = pl.program_id(0)

    def gather(da_ref, db_ref, s):
        def issue(r, carry):
            for k, dk_ref in enumerate((da_ref, db_ref)):
                pltpu.make_async_copy(y_ref.at[pl.ds(dk_ref[0, r], 1), :],
                                      buf_ref.at[s, k, pl.ds(r, 1), :], sems.at[s]).start()
            return carry

        lax.fori_loop(0, tc, issue, 0, unroll=DMA_ISSUE_UNROLL)

    @pl.when(i == 0)
    def _():
        gather(d1_ref, d2_ref, 0)

    def tile(s):
        @pl.when(i + 1 < pl.num_programs(0))
        def _():
            gather(d1n_ref, d2n_ref, 1 - s)

        for k in range(TOP_K):
            pltpu.make_async_copy(y_ref.at[pl.ds(0, tc), :], buf_ref.at[s, k], sems.at[s]).wait()

        f = g1_ref[...] * buf_ref[s, 0] + g2_ref[...] * buf_ref[s, 1]
        o_ref[...] = _layer_norm(ALPHA * x_ref[...] + f, lg_ref[...], lb_ref[...])

    pl.when(i % 2 == 0)(functools.partial(tile, 0))
    pl.when(i % 2 == 1)(functools.partial(tile, 1))


def _combine_ln(x2d, y, d1, d2, g1, g2, lg, lb):
    n, d = x2d.shape
    tc = _tile(n, 256)
    nb = n // tc
    kern = functools.partial(_combine_kernel, tc=tc)
    cur = lambda: pl.BlockSpec((None, 1, tc), lambda i: (i, 0, 0), memory_space=pltpu.SMEM)
    nxt = lambda: pl.BlockSpec((None, 1, tc), lambda i: (jnp.minimum(i + 1, nb - 1), 0, 0),
                               memory_space=pltpu.SMEM)
    d1b, d2b = d1.reshape(nb, 1, tc), d2.reshape(nb, 1, tc)
    return pl.pallas_call(
        kern,
        grid=(nb,),
        in_specs=[cur(), cur(), nxt(), nxt(),
                  pl.BlockSpec((tc, d), lambda i: (i, 0)),
                  pl.BlockSpec((tc, 1), lambda i: (i, 0)),
                  pl.BlockSpec((tc, 1), lambda i: (i, 0)),
                  pl.BlockSpec((1, d), lambda i: (0, 0)),
                  pl.BlockSpec((1, d), lambda i: (0, 0)),
                  pl.BlockSpec(memory_space=pl.ANY)],
        out_specs=pl.BlockSpec((tc, d), lambda i: (i, 0)),
        out_shape=jax.ShapeDtypeStruct((n, d), F32),
        scratch_shapes=[pltpu.VMEM((2, TOP_K, tc, d), F32), pltpu.SemaphoreType.DMA((2,))],
        compiler_params=_params("arbitrary"),
        name="combine_ln",
    )(d1b, d2b, d1b, d2b, x2d, g1.reshape(n, 1), g2.reshape(n, 1), lg, lb, y)


def kernel(x, ev_w_in, ev_w_pool, ev_pool_scale, ev_lam_q1, ev_lam_k1, ev_lam_q2, ev_lam_k2, ev_subln_g, ev_w_out, ev_ln1_g, ev_ln1_b, ev_ffn_wg, ev_ffn_wu, ev_ffn_wd, ev_ln2_g, ev_ln2_b, od_w_in, od_conv_w, od_w_out, od_ln1_g, od_ln1_b, od_router, od_exp_wg, od_exp_wu, od_exp_wd, od_ln2_g, od_ln2_b):
    batch, seq, d = x.shape
    n = batch * seq
    assert ev_w_in.shape[0] == 1 and od_w_in.shape[0] == 1, "DEPTH == 2: one even and one odd layer"
    h = x.reshape(n, d)
    bf = lambda w: w.astype(BF16)
    row = lambda v: v.reshape(1, -1)

    pool_width = ev_w_pool.shape[1] * ev_w_pool.shape[2]
    u_pool, qkv = _even_inproj(h, bf(ev_w_in[0]), pool_width)
    a = _pool_mixer(u_pool, bf(ev_w_pool[0]), row(ev_pool_scale[0]), seq)
    o = _diff_attention(qkv, row(ev_lam_q1[0]), row(ev_lam_k1[0]), row(ev_lam_q2[0]), row(ev_lam_k2[0]),
                        row(ev_subln_g[0]), batch, seq)
    h = _outproj_ln([a, o], bf(ev_w_out[0]), h, row(ev_ln1_g[0]), row(ev_ln1_b[0]))
    h = _ffn_ln(h, bf(ev_ffn_wg[0]), bf(ev_ffn_wu[0]), bf(ev_ffn_wd[0]), row(ev_ln2_g[0]), row(ev_ln2_b[0]))

    gated = _odd_inproj(h, bf(od_w_in[0]), od_conv_w[0].reshape(CONV_K, -1), seq)
    n_exp = od_router.shape[-1]
    w_router = jnp.pad(od_router[0], ((0, 0), (0, ROUTER_LANES - n_exp)))
    h, logits = _outproj_ln([gated], bf(od_w_out[0]), h, row(od_ln1_g[0]), row(od_ln1_b[0]), w_router)

    tm = _tile(n, 1024)
    n_tiles = (TOP_K * n) // tm + n_exp
    d1, d2, g1, g2, te, nact, half2 = _route(logits[:, :n_exp].T, tm, n_tiles)
    src = _invert(d1, d2, n_tiles * tm)
    y = _moe_ffn(h, src, te.reshape(-1), nact.reshape(-1), half2.reshape(-1), od_exp_wg[0], od_exp_wu[0],
                 od_exp_wd[0], tm)
    h = _combine_ln(h, y, d1, d2, g1, g2, row(od_ln2_g[0]), row(od_ln2_b[0]))
    return h.reshape(batch, seq, d)
```

```python
import functools
import math

import jax
import jax.numpy as jnp
from jax import lax
from jax.experimental import pallas as pl
from jax.experimental.pallas import tpu as pltpu

F32 = jnp.float32
BF16 = jnp.bfloat16

DEPTH = 2
ALPHA = (2 * DEPTH) ** 0.25
LN_EPS = 1e-5
RMS_EPS = 1e-5
POOL_WINDOWS = (2, 4, 8, 16)
POOL_HALO = 16
DIFF_HEAD_DIM = 64
HEAD_WIDTH = 2 * DIFF_HEAD_DIM
LAM_INIT_LAYER0 = 0.8 - 0.6 * math.exp(-0.3 * 0)
CONV_K = 3
CONV_HALO = 8
TOP_K = 2
ROUTER_LANES = 128
CUMSUM_CHUNK = 256
ODD_COL_GROUPS = 2
OUTPROJ_ROW_GROUPS = 2
FFN_COL_GROUPS = 2
DMA_ISSUE_UNROLL = 8

VMEM_LIMIT_BYTES = 56 * 1024 * 1024
BIG_TILE_VMEM_LIMIT_BYTES = 61 * 1024 * 1024


def _tile(n, pref):
    t = min(n, pref)
    assert n % t == 0, (n, t)
    return t


def _params(*sem):
    return pltpu.CompilerParams(dimension_semantics=sem, vmem_limit_bytes=VMEM_LIMIT_BYTES)


def _layer_norm(v, g, b):
    mu = jnp.mean(v, axis=-1, keepdims=True)
    d = v - mu
    var = jnp.mean(d * d, axis=-1, keepdims=True)
    return d * lax.rsqrt(var + LN_EPS) * g + b


def _even_inproj_kernel(x_ref, w_ref, pool_ref, qkv_ref, xb_ref):
    j = pl.program_id(1)

    @pl.when(j == 0)
    def _():
        xb_ref[...] = x_ref[...].astype(BF16)

    acc = jnp.dot(xb_ref[...], w_ref[...], preferred_element_type=F32)

    @pl.when(j == 0)
    def _():
        pool_ref[...] = acc

    @pl.when(j > 0)
    def _():
        qkv_ref[...] = acc.astype(BF16)


def _even_inproj(x2d, w_bf16, pool_width):
    n, d = x2d.shape
    width = w_bf16.shape[1]
    tn = pool_width
    assert (width - pool_width) % tn == 0
    tm = _tile(n, 1024)
    return pl.pallas_call(
        _even_inproj_kernel,
        grid=(n // tm, width // tn),
        in_specs=[pl.BlockSpec((tm, d), lambda i, j: (i, 0)),
                  pl.BlockSpec((d, tn), lambda i, j: (0, j))],
        out_specs=[pl.BlockSpec((tm, tn), lambda i, j: (i, 0)),
                   pl.BlockSpec((tm, tn), lambda i, j: (i, jnp.maximum(j - 1, 0)))],
        out_shape=[jax.ShapeDtypeStruct((n, pool_width), F32),
                   jax.ShapeDtypeStruct((n, width - pool_width), BF16)],
        scratch_shapes=[pltpu.VMEM((tm, d), BF16)],
        compiler_params=_params("arbitrary", "arbitrary"),
        name="even_inproj",
    )(x2d, w_bf16)


def _pool_kernel(u_ref, halo_ref, w_ref, scale_ref, o_ref, buf_ref, *, seq, tp, cg):
    i = pl.program_id(0)
    first = (i * tp) % seq == 0
    pos = (i * tp) % seq + lax.broadcasted_iota(jnp.int32, (tp, 1), 0)
    for g, window in enumerate(POOL_WINDOWS):
        cols = slice(g * cg, (g + 1) * cg)
        u = u_ref[:, cols]
        buf_ref[0:POOL_HALO, :] = jnp.where(first, 0.0, halo_ref[:, cols])
        buf_ref[POOL_HALO:, :] = u
        total = u
        for back in range(1, window):
            total = total + buf_ref[pl.ds(POOL_HALO - back, tp), :]
        count = jnp.minimum(pos + 1, window).astype(F32)
        pooled = total / count - u
        mixed = jnp.dot(pooled.astype(BF16), w_ref[g], preferred_element_type=F32)
        o_ref[:, cols] = (mixed * scale_ref[:, cols]).astype(BF16)


def _pool_mixer(u, w_pool_bf16, scale, seq):
    n, pw = u.shape
    groups, cg, _ = w_pool_bf16.shape
    assert groups == len(POOL_WINDOWS) and groups * cg == pw
    tp = _tile(seq, 512)
    halo_blocks = tp // POOL_HALO
    kern = functools.partial(_pool_kernel, seq=seq, tp=tp, cg=cg)
    return pl.pallas_call(
        kern,
        grid=(n // tp,),
        in_specs=[pl.BlockSpec((tp, pw), lambda i: (i, 0)),
                  pl.BlockSpec((POOL_HALO, pw), lambda i: (jnp.maximum(i * halo_blocks - 1, 0), 0)),
                  pl.BlockSpec((groups, cg, cg), lambda i: (0, 0, 0)),
                  pl.BlockSpec((1, pw), lambda i: (0, 0))],
        out_specs=pl.BlockSpec((tp, pw), lambda i: (i, 0)),
        out_shape=jax.ShapeDtypeStruct((n, pw), BF16),
        scratch_shapes=[pltpu.VMEM((POOL_HALO + tp, cg), F32)],
        compiler_params=_params("arbitrary"),
        name="pool_mixer",
    )(u, u, w_pool_bf16, scale)


def _diff_attn_kernel(q_ref, k_ref, v_ref, lq1_ref, lk1_ref, lq2_ref, lk2_ref, g_ref, o_ref,
                      qs_ref, m_ref, acc_ref, sa_ref, sb_ref, *, tq):
    qi = pl.program_id(2)
    q = q_ref[...]
    lane = lax.broadcasted_iota(jnp.int32, q.shape, 1)
    zero = jnp.zeros_like(q)
    scale = jnp.asarray(DIFF_HEAD_DIM ** -0.5, BF16)
    qs_ref[0] = jnp.where(lane < DIFF_HEAD_DIM, q, zero) * scale
    qs_ref[1] = jnp.where(lane >= DIFF_HEAD_DIM, q, zero) * scale
    m_ref[...] = jnp.full(m_ref.shape, -jnp.inf, F32)
    acc_ref[...] = jnp.zeros(acc_ref.shape, F32)
    ones = jnp.ones((tq, HEAD_WIDTH), BF16)
    n_chunks = tq // HEAD_WIDTH

    def scores(kb, s_ref):
        k = k_ref[pl.ds(pl.multiple_of(kb * tq, tq), tq), :]
        for h in range(2):
            s_ref[h] = lax.dot_general(qs_ref[h], k, (((1,), (1,)), ((), ())), preferred_element_type=F32)

    def step(kb, s_ref, masked):
        vx = jnp.concatenate([v_ref[pl.ds(pl.multiple_of(kb * tq, tq), tq), :], ones], axis=1)
        for h in range(2):
            s = s_ref[h]
            if masked:
                r = lax.broadcasted_iota(jnp.int32, (tq, tq), 0)
                c = lax.broadcasted_iota(jnp.int32, (tq, tq), 1)
                s = jnp.where(c <= r, s, -jnp.inf)
            chunks = [s[:, i * HEAD_WIDTH:(i + 1) * HEAD_WIDTH] for i in range(n_chunks)]
            m_blk = functools.reduce(jnp.maximum, chunks)
            m_old = m_ref[h]
            m_new = jnp.maximum(m_old, jnp.max(m_blk, axis=-1, keepdims=True))
            scale_old = jnp.exp(m_old - m_new)
            p = jnp.concatenate([jnp.exp(ch - m_new) for ch in chunks], axis=1).astype(BF16)
            pv = jnp.dot(p, vx, preferred_element_type=F32)
            acc_ref[h] = jnp.concatenate([scale_old, scale_old], axis=1) * acc_ref[h] + pv
            m_ref[h] = m_new

    def pair(t, carry):
        scores(2 * t + 1, sb_ref)
        step(2 * t, sa_ref, False)
        scores(2 * t + 2, sa_ref)
        step(2 * t + 1, sb_ref, False)
        return carry

    scores(0, sa_ref)
    lax.fori_loop(0, qi // 2, pair, 0)

    @pl.when(qi % 2 == 0)
    def _():
        step(qi, sa_ref, True)

    @pl.when(qi % 2 == 1)
    def _():
        scores(qi, sb_ref)
        step(qi - 1, sa_ref, False)
        step(qi, sb_ref, True)

    lam = (jnp.exp(jnp.sum(lq1_ref[...] * lk1_ref[...], keepdims=True))
           - jnp.exp(jnp.sum(lq2_ref[...] * lk2_ref[...], keepdims=True)) + LAM_INIT_LAYER0)
    o = (acc_ref[0, :, 0:HEAD_WIDTH] / acc_ref[0, :, HEAD_WIDTH:]
         - lam * (acc_ref[1, :, 0:HEAD_WIDTH] / acc_ref[1, :, HEAD_WIDTH:]))
    o = o * lax.rsqrt(jnp.mean(o * o, axis=-1, keepdims=True) + RMS_EPS)
    o_ref[...] = (o * g_ref[...] * (1.0 - LAM_INIT_LAYER0)).astype(BF16)


def _diff_attention(qkv, lq1, lk1, lq2, lk2, subln_g, batch, seq):
    n, w3 = qkv.shape
    width = w3 // 3
    heads = width // HEAD_WIDTH
    tq = _tile(seq, 512)
    nq = seq // tq
    kern = functools.partial(_diff_attn_kernel, tq=tq)
    vec = lambda: pl.BlockSpec((1, DIFF_HEAD_DIM), lambda b, h, i: (0, 0))
    return pl.pallas_call(
        kern,
        grid=(batch, heads, nq),
        in_specs=[pl.BlockSpec((tq, HEAD_WIDTH), lambda b, h, i: (b * nq + i, h)),
                  pl.BlockSpec((seq, HEAD_WIDTH), lambda b, h, i: (b, heads + h)),
                  pl.BlockSpec((seq, HEAD_WIDTH), lambda b, h, i: (b, 2 * heads + h)),
                  vec(), vec(), vec(), vec(),
                  pl.BlockSpec((1, HEAD_WIDTH), lambda b, h, i: (0, 0))],
        out_specs=pl.BlockSpec((tq, HEAD_WIDTH), lambda b, h, i: (b * nq + i, h)),
        out_shape=jax.ShapeDtypeStruct((n, width), BF16),
        scratch_shapes=[pltpu.VMEM((2, tq, HEAD_WIDTH), BF16),
                        pltpu.VMEM((2, tq, HEAD_WIDTH), F32),
                        pltpu.VMEM((2, tq, 2 * HEAD_WIDTH), F32),
                        pltpu.VMEM((2, tq, tq), F32), pltpu.VMEM((2, tq, tq), F32)],
        compiler_params=_params("arbitrary", "arbitrary", "arbitrary"),
        name="diff_attention",
    )(qkv, qkv, qkv, lq1, lk1, lq2, lk2, subln_g)


def _outproj_ln_kernel(*refs, n_lhs, with_router):
    lhs_refs = refs[:n_lhs]
    w_ref, x_ref, g_ref, b_ref = refs[n_lhs:n_lhs + 4]
    rest = refs[n_lhs + 4:]
    if with_router:
        wr_ref, o_ref, logit_ref = rest
    else:
        (o_ref,) = rest
    tm = x_ref.shape[0]
    n_groups = 1 if with_router else OUTPROJ_ROW_GROUPS
    group = tm // n_groups
    for r in range(n_groups):
        rows = slice(r * group, (r + 1) * group)
        y = None
        off = 0
        for a_ref in lhs_refs:
            kp = a_ref.shape[1]
            part = jnp.dot(a_ref[rows, :], w_ref[off:off + kp, :], preferred_element_type=F32)
            y = part if y is None else y + part
            off += kp
        out = _layer_norm(ALPHA * x_ref[rows, :] + y, g_ref[...], b_ref[...])
        o_ref[rows, :] = out
        if with_router:
            out_hi = out.astype(BF16)
            out_lo = (out - out_hi.astype(F32)).astype(BF16)
            hi_both = jnp.dot(out_hi, wr_ref[...], preferred_element_type=F32)
            lo_hi = jnp.dot(out_lo, wr_ref[:, 0:ROUTER_LANES], preferred_element_type=F32)
            logit_ref[rows, :] = hi_both[:, 0:ROUTER_LANES] + (hi_both[:, ROUTER_LANES:] + lo_hi)


def _outproj_ln(lhs_list, w_bf16, x2d, g, b, w_router_padded=None):
    n, d = x2d.shape
    tm = _tile(n, 512)
    with_router = w_router_padded is not None
    in_specs = [pl.BlockSpec((tm, a.shape[1]), lambda i: (i, 0)) for a in lhs_list]
    in_specs += [pl.BlockSpec(w_bf16.shape, lambda i: (0, 0)),
                 pl.BlockSpec((tm, d), lambda i: (i, 0)),
                 pl.BlockSpec((1, d), lambda i: (0, 0)),
                 pl.BlockSpec((1, d), lambda i: (0, 0))]
    out_specs = [pl.BlockSpec((tm, d), lambda i: (i, 0))]
    out_shape = [jax.ShapeDtypeStruct((n, d), F32)]
    args = list(lhs_list) + [w_bf16, x2d, g, b]
    if with_router:
        wr_hi = w_router_padded.astype(BF16)
        wr_lo = (w_router_padded - wr_hi.astype(F32)).astype(BF16)
        wr = jnp.concatenate([wr_hi, wr_lo], axis=1)
        in_specs.append(pl.BlockSpec(wr.shape, lambda i: (0, 0)))
        out_specs.append(pl.BlockSpec((tm, ROUTER_LANES), lambda i: (i, 0)))
        out_shape.append(jax.ShapeDtypeStruct((n, ROUTER_LANES), F32))
        args.append(wr)
    kern = functools.partial(_outproj_ln_kernel, n_lhs=len(lhs_list), with_router=with_router)
    res = pl.pallas_call(
        kern,
        grid=(n // tm,),
        in_specs=in_specs,
        out_specs=out_specs,
        out_shape=out_shape,
        compiler_params=_params("arbitrary"),
        name="outproj_ln_router" if with_router else "outproj_ln",
    )(*args)
    return res if with_router else res[0]


def _swiglu_hidden(xb, wg, wu):
    gate = jnp.dot(xb, wg, preferred_element_type=F32)
    up = jnp.dot(xb, wu, preferred_element_type=F32)
    return (gate / (1.0 + jnp.exp(-gate)) * up).astype(BF16)


def _swiglu_block(xb, wg, wu, wd):
    return jnp.dot(_swiglu_hidden(xb, wg, wu), wd, preferred_element_type=F32)


def _ffn_ln_kernel(x_ref, wg_ref, wu_ref, wd_ref, g_ref, b_ref, o_ref, xb_ref):
    j = pl.program_id(1)

    @pl.when(j == 0)
    def _():
        xb_ref[...] = x_ref[...].astype(BF16)
        o_ref[...] = jnp.zeros_like(o_ref)

    xb = xb_ref[...]
    tf = wg_ref.shape[1]
    group = tf // FFN_COL_GROUPS
    hidden = [_swiglu_hidden(xb, wg_ref[:, c * group:(c + 1) * group], wu_ref[:, c * group:(c + 1) * group])
              for c in range(FFN_COL_GROUPS)]
    o_ref[...] += jnp.dot(jnp.concatenate(hidden, axis=1), wd_ref[...], preferred_element_type=F32)

    @pl.when(j == pl.num_programs(1) - 1)
    def _():
        o_ref[...] = _layer_norm(ALPHA * x_ref[...] + o_ref[...], g_ref[...], b_ref[...])


def _ffn_ln(x2d, wg, wu, wd, g, b):
    n, d = x2d.shape
    ff = wg.shape[1]
    tm = _tile(n, 1024)
    tf = _tile(ff, 512)
    return pl.pallas_call(
        _ffn_ln_kernel,
        grid=(n // tm, ff // tf),
        in_specs=[pl.BlockSpec((tm, d), lambda i, j: (i, 0)),
                  pl.BlockSpec((d, tf), lambda i, j: (0, j)),
                  pl.BlockSpec((d, tf), lambda i, j: (0, j)),
                  pl.BlockSpec((tf, d), lambda i, j: (j, 0)),
                  pl.BlockSpec((1, d), lambda i, j: (0, 0)),
                  pl.BlockSpec((1, d), lambda i, j: (0, 0))],
        out_specs=pl.BlockSpec((tm, d), lambda i, j: (i, 0)),
        out_shape=jax.ShapeDtypeStruct((n, d), F32),
        scratch_shapes=[pltpu.VMEM((tm, d), BF16)],
        compiler_params=pltpu.CompilerParams(dimension_semantics=("arbitrary", "arbitrary"),
                                             vmem_limit_bytes=BIG_TILE_VMEM_LIMIT_BYTES),
        name="ffn_ln",
    )(x2d, wg, wu, wd, g, b)


def _odd_inproj_kernel(x_ref, wb_ref, wc_ref, wh_ref, cw_ref, o_ref, xb_ref, zs_ref, carry_ref,
                       *, seq, tm):
    i = pl.program_id(0)
    j = pl.program_id(1)

    @pl.when(j == 0)
    def _():
        xb_ref[...] = x_ref[...].astype(BF16)

    xb = xb_ref[...]
    first = (i * tm) % seq == 0
    tn = o_ref.shape[1]
    group = tn // ODD_COL_GROUPS
    for c in range(ODD_COL_GROUPS):
        cols = slice(c * group, (c + 1) * group)
        b_gate = jnp.dot(xb, wb_ref[:, cols], preferred_element_type=F32)
        c_gate = jnp.dot(xb, wc_ref[:, cols], preferred_element_type=F32)
        hx = jnp.dot(xb, wh_ref[:, cols], preferred_element_type=F32)
        z = c_gate * hx
        zs_ref[0:CONV_HALO, cols] = jnp.where(first, 0.0, carry_ref[j, :, cols])
        zs_ref[CONV_HALO:, cols] = z
        carry_ref[j, :, cols] = z[tm - CONV_HALO:, :]
        y = (cw_ref[0:1, cols] * zs_ref[pl.ds(CONV_HALO - 2, tm), cols]
             + cw_ref[1:2, cols] * zs_ref[pl.ds(CONV_HALO - 1, tm), cols]
             + cw_ref[2:3, cols] * z)
        o_ref[:, cols] = (b_gate * y).astype(BF16)


def _odd_inproj(x2d, w_bf16, conv_w, seq):
    n, d = x2d.shape
    cw = w_bf16.shape[1] // 3
    tm = _tile(seq, 1024)
    tn = _tile(cw, 512)
    nj = cw // tn
    kern = functools.partial(_odd_inproj_kernel, seq=seq, tm=tm)
    return pl.pallas_call(
        kern,
        grid=(n // tm, nj),
        in_specs=[pl.BlockSpec((tm, d), lambda i, j: (i, 0)),
                  pl.BlockSpec((d, tn), lambda i, j: (0, j)),
                  pl.BlockSpec((d, tn), lambda i, j: (0, nj + j)),
                  pl.BlockSpec((d, tn), lambda i, j: (0, 2 * nj + j)),
                  pl.BlockSpec((CONV_K, tn), lambda i, j: (0, j))],
        out_specs=pl.BlockSpec((tm, tn), lambda i, j: (i, j)),
        out_shape=jax.ShapeDtypeStruct((n, cw), BF16),
        scratch_shapes=[pltpu.VMEM((tm, d), BF16),
                        pltpu.VMEM((CONV_HALO + tm, tn), F32),
                        pltpu.VMEM((nj, CONV_HALO, tn), F32)],
        compiler_params=_params("arbitrary", "arbitrary"),
        name="odd_inproj",
    )(x2d, w_bf16, w_bf16, w_bf16, conv_w)


def _route_kernel(lt_ref, d1_ref, d2_ref, g1_ref, g2_ref, te_ref, nact_ref, zfill_ref, half2_ref, rank_ref,
                  *, tm):
    e, n = lt_ref.shape
    lt = lt_ref[...]
    eidx = lax.broadcasted_iota(jnp.int32, (e, n), 0)
    v1 = jnp.max(lt, axis=0, keepdims=True)
    i1 = jnp.min(jnp.where(lt == v1, eidx, e), axis=0, keepdims=True)
    oh1 = eidx == i1
    lt2 = jnp.where(oh1, -jnp.inf, lt)
    v2 = jnp.max(lt2, axis=0, keepdims=True)
    i2 = jnp.min(jnp.where(lt2 == v2, eidx, e), axis=0, keepdims=True)
    oh2 = eidx == i2
    ex = jnp.exp(v2 - v1)
    g1_ref[...] = 1.0 / (1.0 + ex)
    g2_ref[...] = ex / (1.0 + ex)

    sel = jnp.where(oh1 | oh2, 1.0, 0.0)
    a = lax.broadcasted_iota(jnp.int32, (CUMSUM_CHUNK, CUMSUM_CHUNK), 0)
    b = lax.broadcasted_iota(jnp.int32, (CUMSUM_CHUNK, CUMSUM_CHUNK), 1)
    upper = jnp.where(a <= b, 1.0, 0.0).astype(BF16)
    running = jnp.zeros((e, 1), F32)
    for c in range(n // CUMSUM_CHUNK):
        cols = slice(c * CUMSUM_CHUNK, (c + 1) * CUMSUM_CHUNK)
        blk = sel[:, cols]
        incl = jnp.dot(blk.astype(BF16), upper, preferred_element_type=F32)
        rank_ref[:, cols] = incl - blk + running
        running = running + incl[:, CUMSUM_CHUNK - 1:CUMSUM_CHUNK]

    log2_tm = tm.bit_length() - 1
    assert tm == 1 << log2_tm
    counts = running.astype(jnp.int32)
    padded = ((counts + (tm - 1)) >> log2_tm) << log2_tm
    e8 = lax.broadcasted_iota(jnp.int32, (e, 1), 0)
    offs = jnp.zeros((e, 1), jnp.int32)
    total = jnp.zeros((1, 1), jnp.int32)
    for k in range(e):
        offs = jnp.where(e8 == k, total, offs)
        total = total + padded[k:k + 1, :]
    ends = offs + padded
    zfill_ref[...] = jnp.where(padded > 0, ends - tm, -1)

    dest = rank_ref[...].astype(jnp.int32) + offs
    d1_ref[...] = jnp.sum(jnp.where(oh1, dest, 0), axis=0, keepdims=True)
    d2_ref[...] = jnp.sum(jnp.where(oh2, dest, 0), axis=0, keepdims=True)

    t = te_ref.shape[1]
    starts = lax.broadcasted_iota(jnp.int32, (e, t), 1) * tm
    te = jnp.sum(jnp.where(ends <= starts, 1, 0), axis=0, keepdims=True)
    te_ref[...] = jnp.minimum(te, e - 1)
    nact_ref[...] = total >> log2_tm
    in_group = (offs <= starts) & (starts < ends)
    half2 = in_group & (starts + tm // 2 < offs + counts)
    half2_ref[...] = jnp.sum(jnp.where(half2, 1, 0), axis=0, keepdims=True)


def _route(logits_t, tm, n_tiles):
    e, n = logits_t.shape
    assert n % CUMSUM_CHUNK == 0
    kern = functools.partial(_route_kernel, tm=tm)
    row_i = jax.ShapeDtypeStruct((1, n), jnp.int32)
    row_f = jax.ShapeDtypeStruct((1, n), F32)
    return pl.pallas_call(
        kern,
        out_shape=[row_i, row_i, row_f, row_f,
                   jax.ShapeDtypeStruct((1, n_tiles), jnp.int32),
                   jax.ShapeDtypeStruct((1, 1), jnp.int32),
                   jax.ShapeDtypeStruct((e, 1), jnp.int32),
                   jax.ShapeDtypeStruct((1, n_tiles), jnp.int32)],
        scratch_shapes=[pltpu.VMEM((e, n), F32)],
        compiler_params=pltpu.CompilerParams(vmem_limit_bytes=VMEM_LIMIT_BYTES),
        name="route",
    )(logits_t)


def _dispatch_kernel(fill_ref, d1_ref, d2_ref, x_ref, xs_ref, zero_ref, sem, zsem, *, td, tm, n_exp, n_tiles):
    @pl.when(pl.program_id(0) == 0)
    def _():
        zero_ref[...] = jnp.zeros_like(zero_ref)

        def fill_copy(row0):
            return pltpu.make_async_copy(zero_ref, xs_ref.at[pl.ds(pl.multiple_of(row0, tm), tm), :], zsem)

        def start_fill(row0):
            fill_copy(row0).start()

        def wait_fill(row0):
            fill_copy(row0).wait()

        fills = [(fill_ref[e] >= 0, fill_ref[e]) for e in range(n_exp)]
        fills += [(t >= fill_ref[n_exp], t * tm) for t in range(n_tiles - n_exp, n_tiles)]
        for wanted, row0 in fills:
            pl.when(wanted)(functools.partial(start_fill, row0))
        for wanted, row0 in fills:
            pl.when(wanted)(functools.partial(wait_fill, row0))

    def row_copy(r, dst_row):
        return pltpu.make_async_copy(x_ref.at[pl.ds(r, 1), :], xs_ref.at[pl.ds(dst_row, 1), :], sem)

    def issue(r, carry):
        row_copy(r, d1_ref[0, r]).start()
        row_copy(r, d2_ref[0, r]).start()
        return carry

    lax.fori_loop(0, td, issue, 0, unroll=DMA_ISSUE_UNROLL)
    for _ in range(TOP_K):
        pltpu.make_async_copy(x_ref, xs_ref.at[pl.ds(0, td), :], sem).wait()


def _dispatch(x2d, d1, d2, fill, tm, n_tiles):
    n, d = x2d.shape
    n_exp = fill.shape[0] - 1
    td = _tile(n, 512)
    nb = n // td
    kern = functools.partial(_dispatch_kernel, td=td, tm=tm, n_exp=n_exp, n_tiles=n_tiles)
    smem_idx = lambda: pl.BlockSpec((None, 1, td), lambda i: (i, 0, 0), memory_space=pltpu.SMEM)
    return pl.pallas_call(
        kern,
        grid=(nb,),
        in_specs=[pl.BlockSpec(memory_space=pltpu.SMEM), smem_idx(), smem_idx(),
                  pl.BlockSpec((td, d), lambda i: (i, 0))],
        out_specs=pl.BlockSpec(memory_space=pl.ANY),
        out_shape=jax.ShapeDtypeStruct((n_tiles * tm, d), F32),
        scratch_shapes=[pltpu.VMEM((tm, d), F32), pltpu.SemaphoreType.DMA, pltpu.SemaphoreType.DMA],
        compiler_params=_params("arbitrary"),
        name="dispatch",
    )(fill, d1.reshape(nb, 1, td), d2.reshape(nb, 1, td), x2d)


def _moe_kernel(te_ref, nact_ref, half2_ref, xs_hbm, wg_ref, wu_ref, wd_ref, y_ref, stage_ref, xb_ref, sem,
                *, tm):
    i = pl.program_id(0)
    j = pl.program_id(1)
    nact = nact_ref[0]
    active = i < nact
    half = tm // 2

    def tile_copy(t):
        return pltpu.make_async_copy(xs_hbm.at[pl.ds(pl.multiple_of(t * tm, tm), tm), :], stage_ref, sem)

    @pl.when(j == 0)
    def _():
        y_ref[...] = jnp.zeros_like(y_ref)

    @pl.when((i == 0) & (j == 0))
    def _():
        tile_copy(0).start()

    @pl.when(active & (j == 0))
    def _():
        tile_copy(i).wait()
        xb_ref[...] = stage_ref[...].astype(BF16)

    @pl.when((j == 1) & (i + 1 < nact))
    def _():
        tile_copy(i + 1).start()

    def accumulate(rows):
        y_ref[rows, :] += _swiglu_block(xb_ref[rows, :], wg_ref[...].astype(BF16), wu_ref[...].astype(BF16),
                                        wd_ref[...].astype(BF16))

    @pl.when(active & (half2_ref[i] > 0))
    def _():
        accumulate(slice(None))

    @pl.when(active & (half2_ref[i] == 0))
    def _():
        accumulate(slice(0, half))


def _moe_ffn(xs, te, nact, half2, wg, wu, wd, tm):
    rows, d = xs.shape
    n_exp, _, ff = wg.shape
    tf = _tile(ff, 512)
    nj = ff // tf
    assert nj >= 2
    n_tiles = rows // tm

    def ff_blk(i, j, nact_ref):
        return jnp.where(i < nact_ref[0], j, nj - 1)

    def expert(i, te_ref, nact_ref):
        return te_ref[jnp.minimum(i, nact_ref[0] - 1)]

    grid_spec = pltpu.PrefetchScalarGridSpec(
        num_scalar_prefetch=3,
        grid=(n_tiles, nj),
        in_specs=[pl.BlockSpec(memory_space=pl.ANY),
                  pl.BlockSpec((None, d, tf), lambda i, j, t, a, h: (expert(i, t, a), 0, ff_blk(i, j, a))),
                  pl.BlockSpec((None, d, tf), lambda i, j, t, a, h: (expert(i, t, a), 0, ff_blk(i, j, a))),
                  pl.BlockSpec((None, tf, d), lambda i, j, t, a, h: (expert(i, t, a), ff_blk(i, j, a), 0))],
        out_specs=pl.BlockSpec((tm, d), lambda i, j, t, a, h: (i, 0)),
        scratch_shapes=[pltpu.VMEM((tm, d), F32), pltpu.VMEM((tm, d), BF16), pltpu.SemaphoreType.DMA],
    )
    return pl.pallas_call(
        functools.partial(_moe_kernel, tm=tm),
        grid_spec=grid_spec,
        out_shape=jax.ShapeDtypeStruct((rows, d), F32),
        compiler_params=pltpu.CompilerParams(dimension_semantics=("arbitrary", "arbitrary"),
                                             vmem_limit_bytes=BIG_TILE_VMEM_LIMIT_BYTES),
        name="moe_ffn",
    )(te, nact, half2, xs, wg, wu, wd)


def _combine_kernel(d1_ref, d2_ref, d1n_ref, d2n_ref, x_ref, g1_ref, g2_ref, lg_ref, lb_ref, y_ref, o_ref,
                    buf_ref, sems, *, tc):
    i = pl.program_id(0)

    def gather(da_ref, db_ref, s):
        def issue(r, carry):
            for k, dk_ref in enumerate((da_ref, db_ref)):
                pltpu.make_async_copy(y_ref.at[pl.ds(dk_ref[0, r], 1), :],
                                      buf_ref.at[s, k, pl.ds(r, 1), :], sems.at[s]).start()
            return carry

        lax.fori_loop(0, tc, issue, 0, unroll=DMA_ISSUE_UNROLL)

    @pl.when(i == 0)
    def _():
        gather(d1_ref, d2_ref, 0)

    def tile(s):
        @pl.when(i + 1 < pl.num_programs(0))
        def _():
            gather(d1n_ref, d2n_ref, 1 - s)

        for k in range(TOP_K):
            pltpu.make_async_copy(y_ref.at[pl.ds(0, tc), :], buf_ref.at[s, k], sems.at[s]).wait()

        f = g1_ref[...] * buf_ref[s, 0] + g2_ref[...] * buf_ref[s, 1]
        o_ref[...] = _layer_norm(ALPHA * x_ref[...] + f, lg_ref[...], lb_ref[...])

    pl.when(i % 2 == 0)(functools.partial(tile, 0))
    pl.when(i % 2 == 1)(functools.partial(tile, 1))


def _combine_ln(x2d, y, d1, d2, g1, g2, lg, lb):
    n, d = x2d.shape
    tc = _tile(n, 512)
    nb = n // tc
    kern = functools.partial(_combine_kernel, tc=tc)
    cur = lambda: pl.BlockSpec((None, 1, tc), lambda i: (i, 0, 0), memory_space=pltpu.SMEM)
    nxt = lambda: pl.BlockSpec((None, 1, tc), lambda i: (jnp.minimum(i + 1, nb - 1), 0, 0),
                               memory_space=pltpu.SMEM)
    d1b, d2b = d1.reshape(nb, 1, tc), d2.reshape(nb, 1, tc)
    return pl.pallas_call(
        kern,
        grid=(nb,),
        in_specs=[cur(), cur(), nxt(), nxt(),
                  pl.BlockSpec((tc, d), lambda i: (i, 0)),
                  pl.BlockSpec((tc, 1), lambda i: (i, 0)),
                  pl.BlockSpec((tc, 1), lambda i: (i, 0)),
                  pl.BlockSpec((1, d), lambda i: (0, 0)),
                  pl.BlockSpec((1, d), lambda i: (0, 0)),
                  pl.BlockSpec(memory_space=pl.ANY)],
        out_specs=pl.BlockSpec((tc, d), lambda i: (i, 0)),
        out_shape=jax.ShapeDtypeStruct((n, d), F32),
        scratch_shapes=[pltpu.VMEM((2, TOP_K, tc, d), F32), pltpu.SemaphoreType.DMA((2,))],
        compiler_params=_params("arbitrary"),
        name="combine_ln",
    )(d1b, d2b, d1b, d2b, x2d, g1.reshape(n, 1), g2.reshape(n, 1), lg, lb, y)


def kernel(x, ev_w_in, ev_w_pool, ev_pool_scale, ev_lam_q1, ev_lam_k1, ev_lam_q2, ev_lam_k2, ev_subln_g, ev_w_out, ev_ln1_g, ev_ln1_b, ev_ffn_wg, ev_ffn_wu, ev_ffn_wd, ev_ln2_g, ev_ln2_b, od_w_in, od_conv_w, od_w_out, od_ln1_g, od_ln1_b, od_router, od_exp_wg, od_exp_wu, od_exp_wd, od_ln2_g, od_ln2_b):
    batch, seq, d = x.shape
    n = batch * seq
    assert ev_w_in.shape[0] == 1 and od_w_in.shape[0] == 1, "DEPTH == 2: one even and one odd layer"
    h = x.reshape(n, d)
    bf = lambda w: w.astype(BF16)
    row = lambda v: v.reshape(1, -1)

    pool_width = ev_w_pool.shape[1] * ev_w_pool.shape[2]
    u_pool, qkv = _even_inproj(h, bf(ev_w_in[0]), pool_width)
    a = _pool_mixer(u_pool, bf(ev_w_pool[0]), row(ev_pool_scale[0]), seq)
    o = _diff_attention(qkv, row(ev_lam_q1[0]), row(ev_lam_k1[0]), row(ev_lam_q2[0]), row(ev_lam_k2[0]),
                        row(ev_subln_g[0]), batch, seq)
    h = _outproj_ln([a, o], bf(ev_w_out[0]), h, row(ev_ln1_g[0]), row(ev_ln1_b[0]))
    h = _ffn_ln(h, bf(ev_ffn_wg[0]), bf(ev_ffn_wu[0]), bf(ev_ffn_wd[0]), row(ev_ln2_g[0]), row(ev_ln2_b[0]))

    gated = _odd_inproj(h, bf(od_w_in[0]), od_conv_w[0].reshape(CONV_K, -1), seq)
    n_exp = od_router.shape[-1]
    w_router = jnp.pad(od_router[0], ((0, 0), (0, ROUTER_LANES - n_exp)))
    h, logits = _outproj_ln([gated], bf(od_w_out[0]), h, row(od_ln1_g[0]), row(od_ln1_b[0]), w_router)

    tm = _tile(n, 1024)
    n_tiles = (TOP_K * n) // tm + n_exp
    d1, d2, g1, g2, te, nact, zfill, half2 = _route(logits[:, :n_exp].T, tm, n_tiles)
    fill = jnp.concatenate([zfill.reshape(-1), nact.reshape(-1)])
    xs = _dispatch(h, d1, d2, fill, tm, n_tiles)
    y = _moe_ffn(xs, te.reshape(-1), nact.reshape(-1), half2.reshape(-1), od_exp_wg[0], od_exp_wu[0],
                 od_exp_wd[0], tm)
    h = _combine_ln(h, y, d1, d2, g1, g2, row(od_ln2_g[0]), row(od_ln2_b[0]))
    return h.reshape(batch, seq, d)
```

```python
import functools
import math

import jax
import jax.numpy as jnp
from jax import lax
from jax.experimental import pallas as pl
from jax.experimental.pallas import tpu as pltpu

F32 = jnp.float32
BF16 = jnp.bfloat16

DEPTH = 2
ALPHA = (2 * DEPTH) ** 0.25
LN_EPS = 1e-5
RMS_EPS = 1e-5
POOL_WINDOWS = (2, 4, 8, 16)
POOL_HALO = 16
DIFF_HEAD_DIM = 64
HEAD_WIDTH = 2 * DIFF_HEAD_DIM
LAM_INIT_LAYER0 = 0.8 - 0.6 * math.exp(-0.3 * 0)
CONV_K = 3
CONV_HALO = 8
TOP_K = 2
ROUTER_LANES = 128
CUMSUM_CHUNK = 256
ODD_COL_GROUPS = 2
OUTPROJ_ROW_GROUPS = 2
FFN_COL_GROUPS = 2
DMA_ISSUE_UNROLL = 8

VMEM_LIMIT_BYTES = 56 * 1024 * 1024
BIG_TILE_VMEM_LIMIT_BYTES = 61 * 1024 * 1024


def _tile(n, pref):
    t = min(n, pref)
    assert n % t == 0, (n, t)
    return t


def _params(*sem):
    return pltpu.CompilerParams(dimension_semantics=sem, vmem_limit_bytes=VMEM_LIMIT_BYTES)


def _layer_norm(v, g, b):
    mu = jnp.mean(v, axis=-1, keepdims=True)
    d = v - mu
    var = jnp.mean(d * d, axis=-1, keepdims=True)
    return d * lax.rsqrt(var + LN_EPS) * g + b


def _even_inproj_kernel(x_ref, w_ref, pool_ref, qkv_ref, xb_ref):
    j = pl.program_id(1)

    @pl.when(j == 0)
    def _():
        xb_ref[...] = x_ref[...].astype(BF16)

    acc = jnp.dot(xb_ref[...], w_ref[...], preferred_element_type=F32)

    @pl.when(j == 0)
    def _():
        pool_ref[...] = acc

    @pl.when(j > 0)
    def _():
        qkv_ref[...] = acc.astype(BF16)


def _even_inproj(x2d, w_bf16, pool_width):
    n, d = x2d.shape
    width = w_bf16.shape[1]
    tn = pool_width
    assert (width - pool_width) % tn == 0
    tm = _tile(n, 1024)
    return pl.pallas_call(
        _even_inproj_kernel,
        grid=(n // tm, width // tn),
        in_specs=[pl.BlockSpec((tm, d), lambda i, j: (i, 0)),
                  pl.BlockSpec((d, tn), lambda i, j: (0, j))],
        out_specs=[pl.BlockSpec((tm, tn), lambda i, j: (i, 0)),
                   pl.BlockSpec((tm, tn), lambda i, j: (i, jnp.maximum(j - 1, 0)))],
        out_shape=[jax.ShapeDtypeStruct((n, pool_width), F32),
                   jax.ShapeDtypeStruct((n, width - pool_width), BF16)],
        scratch_shapes=[pltpu.VMEM((tm, d), BF16)],
        compiler_params=_params("arbitrary", "arbitrary"),
        name="even_inproj",
    )(x2d, w_bf16)


def _pool_kernel(u_ref, halo_ref, w_ref, scale_ref, o_ref, buf_ref, *, seq, tp, cg):
    i = pl.program_id(0)
    first = (i * tp) % seq == 0
    pos = (i * tp) % seq + lax.broadcasted_iota(jnp.int32, (tp, 1), 0)
    for g, window in enumerate(POOL_WINDOWS):
        cols = slice(g * cg, (g + 1) * cg)
        u = u_ref[:, cols]
        buf_ref[0:POOL_HALO, :] = jnp.where(first, 0.0, halo_ref[:, cols])
        buf_ref[POOL_HALO:, :] = u
        total = u
        for back in range(1, window):
            total = total + buf_ref[pl.ds(POOL_HALO - back, tp), :]
        count = jnp.minimum(pos + 1, window).astype(F32)
        pooled = total / count - u
        mixed = jnp.dot(pooled.astype(BF16), w_ref[g], preferred_element_type=F32)
        o_ref[:, cols] = (mixed * scale_ref[:, cols]).astype(BF16)


def _pool_mixer(u, w_pool_bf16, scale, seq):
    n, pw = u.shape
    groups, cg, _ = w_pool_bf16.shape
    assert groups == len(POOL_WINDOWS) and groups * cg == pw
    tp = _tile(seq, 512)
    halo_blocks = tp // POOL_HALO
    kern = functools.partial(_pool_kernel, seq=seq, tp=tp, cg=cg)
    return pl.pallas_call(
        kern,
        grid=(n // tp,),
        in_specs=[pl.BlockSpec((tp, pw), lambda i: (i, 0)),
                  pl.BlockSpec((POOL_HALO, pw), lambda i: (jnp.maximum(i * halo_blocks - 1, 0), 0)),
                  pl.BlockSpec((groups, cg, cg), lambda i: (0, 0, 0)),
                  pl.BlockSpec((1, pw), lambda i: (0, 0))],
        out_specs=pl.BlockSpec((tp, pw), lambda i: (i, 0)),
        out_shape=jax.ShapeDtypeStruct((n, pw), BF16),
        scratch_shapes=[pltpu.VMEM((POOL_HALO + tp, cg), F32)],
        compiler_params=_params("arbitrary"),
        name="pool_mixer",
    )(u, u, w_pool_bf16, scale)


def _diff_attn_kernel(q_ref, k_ref, v_ref, lq1_ref, lk1_ref, lq2_ref, lk2_ref, g_ref, o_ref,
                      qs_ref, m_ref, acc_ref, sa_ref, sb_ref, *, tq):
    qi = pl.program_id(2)
    q = q_ref[...]
    lane = lax.broadcasted_iota(jnp.int32, q.shape, 1)
    zero = jnp.zeros_like(q)
    scale = jnp.asarray(DIFF_HEAD_DIM ** -0.5, BF16)
    qs_ref[0] = jnp.where(lane < DIFF_HEAD_DIM, q, zero) * scale
    qs_ref[1] = jnp.where(lane >= DIFF_HEAD_DIM, q, zero) * scale
    m_ref[...] = jnp.full(m_ref.shape, -jnp.inf, F32)
    acc_ref[...] = jnp.zeros(acc_ref.shape, F32)
    ones = jnp.ones((tq, HEAD_WIDTH), BF16)
    n_chunks = tq // HEAD_WIDTH

    def scores(kb, s_ref):
        k = k_ref[pl.ds(pl.multiple_of(kb * tq, tq), tq), :]
        for h in range(2):
            s_ref[h] = lax.dot_general(qs_ref[h], k, (((1,), (1,)), ((), ())), preferred_element_type=F32)

    def step(kb, s_ref, masked):
        vx = jnp.concatenate([v_ref[pl.ds(pl.multiple_of(kb * tq, tq), tq), :], ones], axis=1)
        for h in range(2):
            s = s_ref[h]
            if masked:
                r = lax.broadcasted_iota(jnp.int32, (tq, tq), 0)
                c = lax.broadcasted_iota(jnp.int32, (tq, tq), 1)
                s = jnp.where(c <= r, s, -jnp.inf)
            chunks = [s[:, i * HEAD_WIDTH:(i + 1) * HEAD_WIDTH] for i in range(n_chunks)]
            m_blk = functools.reduce(jnp.maximum, chunks)
            m_old = m_ref[h]
            m_new = jnp.maximum(m_old, jnp.max(m_blk, axis=-1, keepdims=True))
            scale_old = jnp.exp(m_old - m_new)
            p = jnp.concatenate([jnp.exp(ch - m_new) for ch in chunks], axis=1).astype(BF16)
            pv = jnp.dot(p, vx, preferred_element_type=F32)
            acc_ref[h] = jnp.concatenate([scale_old, scale_old], axis=1) * acc_ref[h] + pv
            m_ref[h] = m_new

    def pair(t, carry):
        scores(2 * t + 1, sb_ref)
        step(2 * t, sa_ref, False)
        scores(2 * t + 2, sa_ref)
        step(2 * t + 1, sb_ref, False)
        return carry

    scores(0, sa_ref)
    lax.fori_loop(0, qi // 2, pair, 0)

    @pl.when(qi % 2 == 0)
    def _():
        step(qi, sa_ref, True)

    @pl.when(qi % 2 == 1)
    def _():
        scores(qi, sb_ref)
        step(qi - 1, sa_ref, False)
        step(qi, sb_ref, True)

    lam = (jnp.exp(jnp.sum(lq1_ref[...] * lk1_ref[...], keepdims=True))
           - jnp.exp(jnp.sum(lq2_ref[...] * lk2_ref[...], keepdims=True)) + LAM_INIT_LAYER0)
    o = (acc_ref[0, :, 0:HEAD_WIDTH] / acc_ref[0, :, HEAD_WIDTH:]
         - lam * (acc_ref[1, :, 0:HEAD_WIDTH] / acc_ref[1, :, HEAD_WIDTH:]))
    o = o * lax.rsqrt(jnp.mean(o * o, axis=-1, keepdims=True) + RMS_EPS)
    o_ref[...] = (o * g_ref[...] * (1.0 - LAM_INIT_LAYER0)).astype(BF16)


def _diff_attention(qkv, lq1, lk1, lq2, lk2, subln_g, batch, seq):
    n, w3 = qkv.shape
    width = w3 // 3
    heads = width // HEAD_WIDTH
    tq = _tile(seq, 512)
    nq = seq // tq
    kern = functools.partial(_diff_attn_kernel, tq=tq)
    vec = lambda: pl.BlockSpec((1, DIFF_HEAD_DIM), lambda b, h, i: (0, 0))
    return pl.pallas_call(
        kern,
        grid=(batch, heads, nq),
        in_specs=[pl.BlockSpec((tq, HEAD_WIDTH), lambda b, h, i: (b * nq + i, h)),
                  pl.BlockSpec((seq, HEAD_WIDTH), lambda b, h, i: (b, heads + h)),
                  pl.BlockSpec((seq, HEAD_WIDTH), lambda b, h, i: (b, 2 * heads + h)),
                  vec(), vec(), vec(), vec(),
                  pl.BlockSpec((1, HEAD_WIDTH), lambda b, h, i: (0, 0))],
        out_specs=pl.BlockSpec((tq, HEAD_WIDTH), lambda b, h, i: (b * nq + i, h)),
        out_shape=jax.ShapeDtypeStruct((n, width), BF16),
        scratch_shapes=[pltpu.VMEM((2, tq, HEAD_WIDTH), BF16),
                        pltpu.VMEM((2, tq, HEAD_WIDTH), F32),
                        pltpu.VMEM((2, tq, 2 * HEAD_WIDTH), F32),
                        pltpu.VMEM((2, tq, tq), F32), pltpu.VMEM((2, tq, tq), F32)],
        compiler_params=_params("arbitrary", "arbitrary", "arbitrary"),
        name="diff_attention",
    )(qkv, qkv, qkv, lq1, lk1, lq2, lk2, subln_g)


def _outproj_ln_kernel(*refs, n_lhs, with_router):
    lhs_refs = refs[:n_lhs]
    w_ref, x_ref, g_ref, b_ref = refs[n_lhs:n_lhs + 4]
    rest = refs[n_lhs + 4:]
    if with_router:
        wr_ref, o_ref, logit_ref = rest
    else:
        (o_ref,) = rest
    tm = x_ref.shape[0]
    n_groups = 1 if with_router else OUTPROJ_ROW_GROUPS
    group = tm // n_groups
    for r in range(n_groups):
        rows = slice(r * group, (r + 1) * group)
        y = None
        off = 0
        for a_ref in lhs_refs:
            kp = a_ref.shape[1]
            part = jnp.dot(a_ref[rows, :], w_ref[off:off + kp, :], preferred_element_type=F32)
            y = part if y is None else y + part
            off += kp
        out = _layer_norm(ALPHA * x_ref[rows, :] + y, g_ref[...], b_ref[...])
        o_ref[rows, :] = out
        if with_router:
            out_hi = out.astype(BF16)
            out_lo = (out - out_hi.astype(F32)).astype(BF16)
            hi_both = jnp.dot(out_hi, wr_ref[...], preferred_element_type=F32)
            lo_hi = jnp.dot(out_lo, wr_ref[:, 0:ROUTER_LANES], preferred_element_type=F32)
            logit_ref[rows, :] = hi_both[:, 0:ROUTER_LANES] + (hi_both[:, ROUTER_LANES:] + lo_hi)


def _outproj_ln(lhs_list, w_bf16, x2d, g, b, w_router_padded=None):
    n, d = x2d.shape
    tm = _tile(n, 512)
    with_router = w_router_padded is not None
    in_specs = [pl.BlockSpec((tm, a.shape[1]), lambda i: (i, 0)) for a in lhs_list]
    in_specs += [pl.BlockSpec(w_bf16.shape, lambda i: (0, 0)),
                 pl.BlockSpec((tm, d), lambda i: (i, 0)),
                 pl.BlockSpec((1, d), lambda i: (0, 0)),
                 pl.BlockSpec((1, d), lambda i: (0, 0))]
    out_specs = [pl.BlockSpec((tm, d), lambda i: (i, 0))]
    out_shape = [jax.ShapeDtypeStruct((n, d), F32)]
    args = list(lhs_list) + [w_bf16, x2d, g, b]
    if with_router:
        wr_hi = w_router_padded.astype(BF16)
        wr_lo = (w_router_padded - wr_hi.astype(F32)).astype(BF16)
        wr = jnp.concatenate([wr_hi, wr_lo], axis=1)
        in_specs.append(pl.BlockSpec(wr.shape, lambda i: (0, 0)))
        out_specs.append(pl.BlockSpec((tm, ROUTER_LANES), lambda i: (i, 0)))
        out_shape.append(jax.ShapeDtypeStruct((n, ROUTER_LANES), F32))
        args.append(wr)
    kern = functools.partial(_outproj_ln_kernel, n_lhs=len(lhs_list), with_router=with_router)
    res = pl.pallas_call(
        kern,
        grid=(n // tm,),
        in_specs=in_specs,
        out_specs=out_specs,
        out_shape=out_shape,
        compiler_params=_params("arbitrary"),
        name="outproj_ln_router" if with_router else "outproj_ln",
    )(*args)
    return res if with_router else res[0]


def _swiglu_hidden(xb, wg, wu):
    gate = jnp.dot(xb, wg, preferred_element_type=F32)
    up = jnp.dot(xb, wu, preferred_element_type=F32)
    return (gate / (1.0 + jnp.exp(-gate)) * up).astype(BF16)


def _swiglu_block(xb, wg, wu, wd):
    return jnp.dot(_swiglu_hidden(xb, wg, wu), wd, preferred_element_type=F32)


def _ffn_ln_kernel(x_ref, wg_ref, wu_ref, wd_ref, g_ref, b_ref, o_ref, xb_ref):
    j = pl.program_id(1)

    @pl.when(j == 0)
    def _():
        xb_ref[...] = x_ref[...].astype(BF16)
        o_ref[...] = jnp.zeros_like(o_ref)

    xb = xb_ref[...]
    tf = wg_ref.shape[1]
    group = tf // FFN_COL_GROUPS
    hidden = [_swiglu_hidden(xb, wg_ref[:, c * group:(c + 1) * group], wu_ref[:, c * group:(c + 1) * group])
              for c in range(FFN_COL_GROUPS)]
    o_ref[...] += jnp.dot(jnp.concatenate(hidden, axis=1), wd_ref[...], preferred_element_type=F32)

    @pl.when(j == pl.num_programs(1) - 1)
    def _():
        o_ref[...] = _layer_norm(ALPHA * x_ref[...] + o_ref[...], g_ref[...], b_ref[...])


def _ffn_ln(x2d, wg, wu, wd, g, b):
    n, d = x2d.shape
    ff = wg.shape[1]
    tm = _tile(n, 1024)
    tf = _tile(ff, 512)
    return pl.pallas_call(
        _ffn_ln_kernel,
        grid=(n // tm, ff // tf),
        in_specs=[pl.BlockSpec((tm, d), lambda i, j: (i, 0)),
                  pl.BlockSpec((d, tf), lambda i, j: (0, j)),
                  pl.BlockSpec((d, tf), lambda i, j: (0, j)),
                  pl.BlockSpec((tf, d), lambda i, j: (j, 0)),
                  pl.BlockSpec((1, d), lambda i, j: (0, 0)),
                  pl.BlockSpec((1, d), lambda i, j: (0, 0))],
        out_specs=pl.BlockSpec((tm, d), lambda i, j: (i, 0)),
        out_shape=jax.ShapeDtypeStruct((n, d), F32),
        scratch_shapes=[pltpu.VMEM((tm, d), BF16)],
        compiler_params=pltpu.CompilerParams(dimension_semantics=("arbitrary", "arbitrary"),
                                             vmem_limit_bytes=BIG_TILE_VMEM_LIMIT_BYTES),
        name="ffn_ln",
    )(x2d, wg, wu, wd, g, b)


def _odd_inproj_kernel(x_ref, wb_ref, wc_ref, wh_ref, cw_ref, o_ref, xb_ref, zs_ref, carry_ref,
                       *, seq, tm):
    i = pl.program_id(0)
    j = pl.program_id(1)

    @pl.when(j == 0)
    def _():
        xb_ref[...] = x_ref[...].astype(BF16)

    xb = xb_ref[...]
    first = (i * tm) % seq == 0
    tn = o_ref.shape[1]
    group = tn // ODD_COL_GROUPS
    for c in range(ODD_COL_GROUPS):
        cols = slice(c * group, (c + 1) * group)
        b_gate = jnp.dot(xb, wb_ref[:, cols], preferred_element_type=F32)
        c_gate = jnp.dot(xb, wc_ref[:, cols], preferred_element_type=F32)
        hx = jnp.dot(xb, wh_ref[:, cols], preferred_element_type=F32)
        z = c_gate * hx
        zs_ref[0:CONV_HALO, cols] = jnp.where(first, 0.0, carry_ref[j, :, cols])
        zs_ref[CONV_HALO:, cols] = z
        carry_ref[j, :, cols] = z[tm - CONV_HALO:, :]
        y = (cw_ref[0:1, cols] * zs_ref[pl.ds(CONV_HALO - 2, tm), cols]
             + cw_ref[1:2, cols] * zs_ref[pl.ds(CONV_HALO - 1, tm), cols]
             + cw_ref[2:3, cols] * z)
        o_ref[:, cols] = (b_gate * y).astype(BF16)


def _odd_inproj(x2d, w_bf16, conv_w, seq):
    n, d = x2d.shape
    cw = w_bf16.shape[1] // 3
    tm = _tile(seq, 1024)
    tn = _tile(cw, 512)
    nj = cw // tn
    kern = functools.partial(_odd_inproj_kernel, seq=seq, tm=tm)
    return pl.pallas_call(
        kern,
        grid=(n // tm, nj),
        in_specs=[pl.BlockSpec((tm, d), lambda i, j: (i, 0)),
                  pl.BlockSpec((d, tn), lambda i, j: (0, j)),
                  pl.BlockSpec((d, tn), lambda i, j: (0, nj + j)),
                  pl.BlockSpec((d, tn), lambda i, j: (0, 2 * nj + j)),
                  pl.BlockSpec((CONV_K, tn), lambda i, j: (0, j))],
        out_specs=pl.BlockSpec((tm, tn), lambda i, j: (i, j)),
        out_shape=jax.ShapeDtypeStruct((n, cw), BF16),
        scratch_shapes=[pltpu.VMEM((tm, d), BF16),
                        pltpu.VMEM((CONV_HALO + tm, tn), F32),
                        pltpu.VMEM((nj, CONV_HALO, tn), F32)],
        compiler_params=_params("arbitrary", "arbitrary"),
        name="odd_inproj",
    )(x2d, w_bf16, w_bf16, w_bf16, conv_w)


def _route_kernel(lt_ref, d1_ref, d2_ref, g1_ref, g2_ref, te_ref, nact_ref, zfill_ref, half2_ref, rank_ref,
                  *, tm):
    e, n = lt_ref.shape
    lt = lt_ref[...]
    eidx = lax.broadcasted_iota(jnp.int32, (e, n), 0)
    v1 = jnp.max(lt, axis=0, keepdims=True)
    i1 = jnp.min(jnp.where(lt == v1, eidx, e), axis=0, keepdims=True)
    oh1 = eidx == i1
    lt2 = jnp.where(oh1, -jnp.inf, lt)
    v2 = jnp.max(lt2, axis=0, keepdims=True)
    i2 = jnp.min(jnp.where(lt2 == v2, eidx, e), axis=0, keepdims=True)
    oh2 = eidx == i2
    ex = jnp.exp(v2 - v1)
    g1_ref[...] = 1.0 / (1.0 + ex)
    g2_ref[...] = ex / (1.0 + ex)

    sel = jnp.where(oh1 | oh2, 1.0, 0.0)
    a = lax.broadcasted_iota(jnp.int32, (CUMSUM_CHUNK, CUMSUM_CHUNK), 0)
    b = lax.broadcasted_iota(jnp.int32, (CUMSUM_CHUNK, CUMSUM_CHUNK), 1)
    upper = jnp.where(a <= b, 1.0, 0.0).astype(BF16)
    running = jnp.zeros((e, 1), F32)
    for c in range(n // CUMSUM_CHUNK):
        cols = slice(c * CUMSUM_CHUNK, (c + 1) * CUMSUM_CHUNK)
        blk = sel[:, cols]
        incl = jnp.dot(blk.astype(BF16), upper, preferred_element_type=F32)
        rank_ref[:, cols] = incl - blk + running
        running = running + incl[:, CUMSUM_CHUNK - 1:CUMSUM_CHUNK]

    log2_tm = tm.bit_length() - 1
    assert tm == 1 << log2_tm
    counts = running.astype(jnp.int32)
    padded = ((counts + (tm - 1)) >> log2_tm) << log2_tm
    e8 = lax.broadcasted_iota(jnp.int32, (e, 1), 0)
    offs = jnp.zeros((e, 1), jnp.int32)
    total = jnp.zeros((1, 1), jnp.int32)
    for k in range(e):
        offs = jnp.where(e8 == k, total, offs)
        total = total + padded[k:k + 1, :]
    ends = offs + padded
    zfill_ref[...] = jnp.where(padded > 0, ends - tm, -1)

    dest = rank_ref[...].astype(jnp.int32) + offs
    d1_ref[...] = jnp.sum(jnp.where(oh1, dest, 0), axis=0, keepdims=True)
    d2_ref[...] = jnp.sum(jnp.where(oh2, dest, 0), axis=0, keepdims=True)

    t = te_ref.shape[1]
    starts = lax.broadcasted_iota(jnp.int32, (e, t), 1) * tm
    te = jnp.sum(jnp.where(ends <= starts, 1, 0), axis=0, keepdims=True)
    te_ref[...] = jnp.minimum(te, e - 1)
    nact_ref[...] = total >> log2_tm
    in_group = (offs <= starts) & (starts < ends)
    half2 = in_group & (starts + tm // 2 < offs + counts)
    half2_ref[...] = jnp.sum(jnp.where(half2, 1, 0), axis=0, keepdims=True)


def _route(logits_t, tm, n_tiles):
    e, n = logits_t.shape
    assert n % CUMSUM_CHUNK == 0
    kern = functools.partial(_route_kernel, tm=tm)
    row_i = jax.ShapeDtypeStruct((1, n), jnp.int32)
    row_f = jax.ShapeDtypeStruct((1, n), F32)
    return pl.pallas_call(
        kern,
        out_shape=[row_i, row_i, row_f, row_f,
                   jax.ShapeDtypeStruct((1, n_tiles), jnp.int32),
                   jax.ShapeDtypeStruct((1, 1), jnp.int32),
                   jax.ShapeDtypeStruct((e, 1), jnp.int32),
                   jax.ShapeDtypeStruct((1, n_tiles), jnp.int32)],
        scratch_shapes=[pltpu.VMEM((e, n), F32)],
        compiler_params=pltpu.CompilerParams(vmem_limit_bytes=VMEM_LIMIT_BYTES),
        name="route",
    )(logits_t)


def _dispatch_kernel(fill_ref, d1_ref, d2_ref, x_ref, xs_ref, zero_ref, sem, zsem, *, td, tm, n_exp, n_tiles):
    @pl.when(pl.program_id(0) == 0)
    def _():
        zero_ref[...] = jnp.zeros_like(zero_ref)

        def fill_copy(row0):
            return pltpu.make_async_copy(zero_ref, xs_ref.at[pl.ds(pl.multiple_of(row0, tm), tm), :], zsem)

        def start_fill(row0):
            fill_copy(row0).start()

        def wait_fill(row0):
            fill_copy(row0).wait()

        fills = [(fill_ref[e] >= 0, fill_ref[e]) for e in range(n_exp)]
        fills += [(t >= fill_ref[n_exp], t * tm) for t in range(n_tiles - n_exp, n_tiles)]
        for wanted, row0 in fills:
            pl.when(wanted)(functools.partial(start_fill, row0))
        for wanted, row0 in fills:
            pl.when(wanted)(functools.partial(wait_fill, row0))

    def row_copy(r, dst_row):
        return pltpu.make_async_copy(x_ref.at[pl.ds(r, 1), :], xs_ref.at[pl.ds(dst_row, 1), :], sem)

    def issue(r, carry):
        row_copy(r, d1_ref[0, r]).start()
        row_copy(r, d2_ref[0, r]).start()
        return carry

    lax.fori_loop(0, td, issue, 0, unroll=DMA_ISSUE_UNROLL)
    for _ in range(TOP_K):
        pltpu.make_async_copy(x_ref, xs_ref.at[pl.ds(0, td), :], sem).wait()


def _dispatch(x2d, d1, d2, fill, tm, n_tiles):
    n, d = x2d.shape
    n_exp = fill.shape[0] - 1
    td = _tile(n, 512)
    nb = n // td
    kern = functools.partial(_dispatch_kernel, td=td, tm=tm, n_exp=n_exp, n_tiles=n_tiles)
    smem_idx = lambda: pl.BlockSpec((None, 1, td), lambda i: (i, 0, 0), memory_space=pltpu.SMEM)
    return pl.pallas_call(
        kern,
        grid=(nb,),
        in_specs=[pl.BlockSpec(memory_space=pltpu.SMEM), smem_idx(), smem_idx(),
                  pl.BlockSpec((td, d), lambda i: (i, 0))],
        out_specs=pl.BlockSpec(memory_space=pl.ANY),
        out_shape=jax.ShapeDtypeStruct((n_tiles * tm, d), F32),
        scratch_shapes=[pltpu.VMEM((tm, d), F32), pltpu.SemaphoreType.DMA, pltpu.SemaphoreType.DMA],
        compiler_params=_params("arbitrary"),
        name="dispatch",
    )(fill, d1.reshape(nb, 1, td), d2.reshape(nb, 1, td), x2d)


def _moe_kernel(te_ref, nact_ref, half2_ref, xs_hbm, wg_ref, wu_ref, wd_ref, y_ref, stage_ref, xb_ref, sem,
                *, tm):
    i = pl.program_id(0)
    j = pl.program_id(1)
    nact = nact_ref[0]
    active = i < nact
    half = tm // 2

    def tile_copy(t):
        return pltpu.make_async_copy(xs_hbm.at[pl.ds(pl.multiple_of(t * tm, tm), tm), :], stage_ref, sem)

    @pl.when(j == 0)
    def _():
        y_ref[...] = jnp.zeros_like(y_ref)

    @pl.when((i == 0) & (j == 0))
    def _():
        tile_copy(0).start()

    @pl.when(active & (j == 0))
    def _():
        tile_copy(i).wait()
        xb_ref[...] = stage_ref[...].astype(BF16)

    @pl.when((j == 1) & (i + 1 < nact))
    def _():
        tile_copy(i + 1).start()

    def accumulate(rows):
        xb = xb_ref[rows, :]
        group = wg_ref.shape[1] // FFN_COL_GROUPS
        hidden = []
        for c in range(FFN_COL_GROUPS):
            cols = slice(c * group, (c + 1) * group)
            hidden.append(_swiglu_hidden(xb, wg_ref[:, cols].astype(BF16), wu_ref[:, cols].astype(BF16)))
        y_ref[rows, :] += jnp.dot(jnp.concatenate(hidden, axis=1), wd_ref[...].astype(BF16),
                                  preferred_element_type=F32)

    @pl.when(active & (half2_ref[i] > 0))
    def _():
        accumulate(slice(None))

    @pl.when(active & (half2_ref[i] == 0))
    def _():
        accumulate(slice(0, half))


def _moe_ffn(xs, te, nact, half2, wg, wu, wd, tm):
    rows, d = xs.shape
    n_exp, _, ff = wg.shape
    tf = _tile(ff, 512)
    nj = ff // tf
    assert nj >= 2
    n_tiles = rows // tm

    def ff_blk(i, j, nact_ref):
        return jnp.where(i < nact_ref[0], j, nj - 1)

    def expert(i, te_ref, nact_ref):
        return te_ref[jnp.minimum(i, nact_ref[0] - 1)]

    grid_spec = pltpu.PrefetchScalarGridSpec(
        num_scalar_prefetch=3,
        grid=(n_tiles, nj),
        in_specs=[pl.BlockSpec(memory_space=pl.ANY),
                  pl.BlockSpec((None, d, tf), lambda i, j, t, a, h: (expert(i, t, a), 0, ff_blk(i, j, a))),
                  pl.BlockSpec((None, d, tf), lambda i, j, t, a, h: (expert(i, t, a), 0, ff_blk(i, j, a))),
                  pl.BlockSpec((None, tf, d), lambda i, j, t, a, h: (expert(i, t, a), ff_blk(i, j, a), 0))],
        out_specs=pl.BlockSpec((tm, d), lambda i, j, t, a, h: (i, 0)),
        scratch_shapes=[pltpu.VMEM((tm, d), F32), pltpu.VMEM((tm, d), BF16), pltpu.SemaphoreType.DMA],
    )
    return pl.pallas_call(
        functools.partial(_moe_kernel, tm=tm),
        grid_spec=grid_spec,
        out_shape=jax.ShapeDtypeStruct((rows, d), F32),
        compiler_params=pltpu.CompilerParams(dimension_semantics=("arbitrary", "arbitrary"),
                                             vmem_limit_bytes=BIG_TILE_VMEM_LIMIT_BYTES),
        name="moe_ffn",
    )(te, nact, half2, xs, wg, wu, wd)


def _combine_kernel(d1_ref, d2_ref, d1n_ref, d2n_ref, x_ref, g1_ref, g2_ref, lg_ref, lb_ref, y_ref, o_ref,
                    buf_ref, sems, *, tc):
    i = pl.program_id(0)

    def gather(da_ref, db_ref, s):
        def issue(r, carry):
            for k, dk_ref in enumerate((da_ref, db_ref)):
                pltpu.make_async_copy(y_ref.at[pl.ds(dk_ref[0, r], 1), :],
                                      buf_ref.at[s, k, pl.ds(r, 1), :], sems.at[s]).start()
            return carry

        lax.fori_loop(0, tc, issue, 0, unroll=DMA_ISSUE_UNROLL)

    @pl.when(i == 0)
    def _():
        gather(d1_ref, d2_ref, 0)

    def tile(s):
        @pl.when(i + 1 < pl.num_programs(0))
        def _():
            gather(d1n_ref, d2n_ref, 1 - s)

        for k in range(TOP_K):
            pltpu.make_async_copy(y_ref.at[pl.ds(0, tc), :], buf_ref.at[s, k], sems.at[s]).wait()

        f = g1_ref[...] * buf_ref[s, 0] + g2_ref[...] * buf_ref[s, 1]
        o_ref[...] = _layer_norm(ALPHA * x_ref[...] + f, lg_ref[...], lb_ref[...])

    pl.when(i % 2 == 0)(functools.partial(tile, 0))
    pl.when(i % 2 == 1)(functools.partial(tile, 1))


def _combine_ln(x2d, y, d1, d2, g1, g2, lg, lb):
    n, d = x2d.shape
    tc = _tile(n, 256)
    nb = n // tc
    kern = functools.partial(_combine_kernel, tc=tc)
    cur = lambda: pl.BlockSpec((None, 1, tc), lambda i: (i, 0, 0), memory_space=pltpu.SMEM)
    nxt = lambda: pl.BlockSpec((None, 1, tc), lambda i: (jnp.minimum(i + 1, nb - 1), 0, 0),
                               memory_space=pltpu.SMEM)
    d1b, d2b = d1.reshape(nb, 1, tc), d2.reshape(nb, 1, tc)
    return pl.pallas_call(
        kern,
        grid=(nb,),
        in_specs=[cur(), cur(), nxt(), nxt(),
                  pl.BlockSpec((tc, d), lambda i: (i, 0)),
                  pl.BlockSpec((tc, 1), lambda i: (i, 0)),
                  pl.BlockSpec((tc, 1), lambda i: (i, 0)),
                  pl.BlockSpec((1, d), lambda i: (0, 0)),
                  pl.BlockSpec((1, d), lambda i: (0, 0)),
                  pl.BlockSpec(memory_space=pl.ANY)],
        out_specs=pl.BlockSpec((tc, d), lambda i: (i, 0)),
        out_shape=jax.ShapeDtypeStruct((n, d), F32),
        scratch_shapes=[pltpu.VMEM((2, TOP_K, tc, d), F32), pltpu.SemaphoreType.DMA((2,))],
        compiler_params=_params("arbitrary"),
        name="combine_ln",
    )(d1b, d2b, d1b, d2b, x2d, g1.reshape(n, 1), g2.reshape(n, 1), lg, lb, y)


def kernel(x, ev_w_in, ev_w_pool, ev_pool_scale, ev_lam_q1, ev_lam_k1, ev_lam_q2, ev_lam_k2, ev_subln_g, ev_w_out, ev_ln1_g, ev_ln1_b, ev_ffn_wg, ev_ffn_wu, ev_ffn_wd, ev_ln2_g, ev_ln2_b, od_w_in, od_conv_w, od_w_out, od_ln1_g, od_ln1_b, od_router, od_exp_wg, od_exp_wu, od_exp_wd, od_ln2_g, od_ln2_b):
    batch, seq, d = x.shape
    n = batch * seq
    assert ev_w_in.shape[0] == 1 and od_w_in.shape[0] == 1, "DEPTH == 2: one even and one odd layer"
    h = x.reshape(n, d)
    bf = lambda w: w.astype(BF16)
    row = lambda v: v.reshape(1, -1)

    pool_width = ev_w_pool.shape[1] * ev_w_pool.shape[2]
    u_pool, qkv = _even_inproj(h, bf(ev_w_in[0]), pool_width)
    a = _pool_mixer(u_pool, bf(ev_w_pool[0]), row(ev_pool_scale[0]), seq)
    o = _diff_attention(qkv, row(ev_lam_q1[0]), row(ev_lam_k1[0]), row(ev_lam_q2[0]), row(ev_lam_k2[0]),
                        row(ev_subln_g[0]), batch, seq)
    h = _outproj_ln([a, o], bf(ev_w_out[0]), h, row(ev_ln1_g[0]), row(ev_ln1_b[0]))
    h = _ffn_ln(h, bf(ev_ffn_wg[0]), bf(ev_ffn_wu[0]), bf(ev_ffn_wd[0]), row(ev_ln2_g[0]), row(ev_ln2_b[0]))

    gated = _odd_inproj(h, bf(od_w_in[0]), od_conv_w[0].reshape(CONV_K, -1), seq)
    n_exp = od_router.shape[-1]
    w_router = jnp.pad(od_router[0], ((0, 0), (0, ROUTER_LANES - n_exp)))
    h, logits = _outproj_ln([gated], bf(od_w_out[0]), h, row(od_ln1_g[0]), row(od_ln1_b[0]), w_router)

    tm = _tile(n, 1024)
    n_tiles = (TOP_K * n) // tm + n_exp
    d1, d2, g1, g2, te, nact, zfill, half2 = _route(logits[:, :n_exp].T, tm, n_tiles)
    fill = jnp.concatenate([zfill.reshape(-1), nact.reshape(-1)])
    xs = _dispatch(h, d1, d2, fill, tm, n_tiles)
    y = _moe_ffn(xs, te.reshape(-1), nact.reshape(-1), half2.reshape(-1), od_exp_wg[0], od_exp_wu[0],
                 od_exp_wd[0], tm)
    h = _combine_ln(h, y, d1, d2, g1, g2, row(od_ln2_g[0]), row(od_ln2_b[0]))
    return h.reshape(batch, seq, d)
```

```python
import functools
import math

import jax
import jax.numpy as jnp
from jax import lax
from jax.experimental import pallas as pl
from jax.experimental.pallas import tpu as pltpu

F32 = jnp.float32
BF16 = jnp.bfloat16

DEPTH = 2
ALPHA = (2 * DEPTH) ** 0.25
LN_EPS = 1e-5
RMS_EPS = 1e-5
POOL_WINDOWS = (2, 4, 8, 16)
POOL_HALO = 16
DIFF_HEAD_DIM = 64
HEAD_WIDTH = 2 * DIFF_HEAD_DIM
LAM_INIT_LAYER0 = 0.8 - 0.6 * math.exp(-0.3 * 0)
CONV_K = 3
CONV_HALO = 8
TOP_K = 2
ROUTER_LANES = 128
CUMSUM_CHUNK = 256
ODD_COL_GROUPS = 2
OUTPROJ_ROW_GROUPS = 2
FFN_COL_GROUPS = 2
DMA_ISSUE_UNROLL = 8

VMEM_LIMIT_BYTES = 56 * 1024 * 1024
BIG_TILE_VMEM_LIMIT_BYTES = 61 * 1024 * 1024


def _tile(n, pref):
    t = min(n, pref)
    assert n % t == 0, (n, t)
    return t


def _params(*sem):
    return pltpu.CompilerParams(dimension_semantics=sem, vmem_limit_bytes=VMEM_LIMIT_BYTES)


def _layer_norm(v, g, b):
    mu = jnp.mean(v, axis=-1, keepdims=True)
    d = v - mu
    var = jnp.mean(d * d, axis=-1, keepdims=True)
    return d * lax.rsqrt(var + LN_EPS) * g + b


def _even_inproj_kernel(x_ref, w_ref, pool_ref, qkv_ref, xb_ref):
    j = pl.program_id(1)

    @pl.when(j == 0)
    def _():
        xb_ref[...] = x_ref[...].astype(BF16)

    acc = jnp.dot(xb_ref[...], w_ref[...], preferred_element_type=F32)

    @pl.when(j == 0)
    def _():
        pool_ref[...] = acc

    @pl.when(j > 0)
    def _():
        qkv_ref[...] = acc.astype(BF16)


def _even_inproj(x2d, w_bf16, pool_width):
    n, d = x2d.shape
    width = w_bf16.shape[1]
    tn = pool_width
    assert (width - pool_width) % tn == 0
    tm = _tile(n, 1024)
    return pl.pallas_call(
        _even_inproj_kernel,
        grid=(n // tm, width // tn),
        in_specs=[pl.BlockSpec((tm, d), lambda i, j: (i, 0)),
                  pl.BlockSpec((d, tn), lambda i, j: (0, j))],
        out_specs=[pl.BlockSpec((tm, tn), lambda i, j: (i, 0)),
                   pl.BlockSpec((tm, tn), lambda i, j: (i, jnp.maximum(j - 1, 0)))],
        out_shape=[jax.ShapeDtypeStruct((n, pool_width), F32),
                   jax.ShapeDtypeStruct((n, width - pool_width), BF16)],
        scratch_shapes=[pltpu.VMEM((tm, d), BF16)],
        compiler_params=_params("arbitrary", "arbitrary"),
        name="even_inproj",
    )(x2d, w_bf16)


def _pool_kernel(u_ref, halo_ref, w_ref, scale_ref, o_ref, buf_ref, *, seq, tp, cg):
    i = pl.program_id(0)
    first = (i * tp) % seq == 0
    pos = (i * tp) % seq + lax.broadcasted_iota(jnp.int32, (tp, 1), 0)
    for g, window in enumerate(POOL_WINDOWS):
        cols = slice(g * cg, (g + 1) * cg)
        u = u_ref[:, cols]
        buf_ref[0:POOL_HALO, :] = jnp.where(first, 0.0, halo_ref[:, cols])
        buf_ref[POOL_HALO:, :] = u
        total = u
        for back in range(1, window):
            total = total + buf_ref[pl.ds(POOL_HALO - back, tp), :]
        count = jnp.minimum(pos + 1, window).astype(F32)
        pooled = total / count - u
        mixed = jnp.dot(pooled.astype(BF16), w_ref[g], preferred_element_type=F32)
        o_ref[:, cols] = (mixed * scale_ref[:, cols]).astype(BF16)


def _pool_mixer(u, w_pool_bf16, scale, seq):
    n, pw = u.shape
    groups, cg, _ = w_pool_bf16.shape
    assert groups == len(POOL_WINDOWS) and groups * cg == pw
    tp = _tile(seq, 512)
    halo_blocks = tp // POOL_HALO
    kern = functools.partial(_pool_kernel, seq=seq, tp=tp, cg=cg)
    return pl.pallas_call(
        kern,
        grid=(n // tp,),
        in_specs=[pl.BlockSpec((tp, pw), lambda i: (i, 0)),
                  pl.BlockSpec((POOL_HALO, pw), lambda i: (jnp.maximum(i * halo_blocks - 1, 0), 0)),
                  pl.BlockSpec((groups, cg, cg), lambda i: (0, 0, 0)),
                  pl.BlockSpec((1, pw), lambda i: (0, 0))],
        out_specs=pl.BlockSpec((tp, pw), lambda i: (i, 0)),
        out_shape=jax.ShapeDtypeStruct((n, pw), BF16),
        scratch_shapes=[pltpu.VMEM((POOL_HALO + tp, cg), F32)],
        compiler_params=_params("arbitrary"),
        name="pool_mixer",
    )(u, u, w_pool_bf16, scale)


def _diff_attn_kernel(q_ref, k_ref, v_ref, lq1_ref, lk1_ref, lq2_ref, lk2_ref, g_ref, o_ref,
                      qs_ref, m_ref, acc_ref, sa_ref, sb_ref, *, tq):
    qi = pl.program_id(2)
    q = q_ref[...]
    lane = lax.broadcasted_iota(jnp.int32, q.shape, 1)
    zero = jnp.zeros_like(q)
    scale = jnp.asarray(DIFF_HEAD_DIM ** -0.5, BF16)
    qs_ref[0] = jnp.where(lane < DIFF_HEAD_DIM, q, zero) * scale
    qs_ref[1] = jnp.where(lane >= DIFF_HEAD_DIM, q, zero) * scale
    m_ref[...] = jnp.full(m_ref.shape, -jnp.inf, F32)
    acc_ref[...] = jnp.zeros(acc_ref.shape, F32)
    ones = jnp.ones((tq, HEAD_WIDTH), BF16)
    n_chunks = tq // HEAD_WIDTH

    def scores(kb, s_ref):
        k = k_ref[pl.ds(pl.multiple_of(kb * tq, tq), tq), :]
        for h in range(2):
            s_ref[h] = lax.dot_general(qs_ref[h], k, (((1,), (1,)), ((), ())), preferred_element_type=F32)

    def step(kb, s_ref, masked):
        vx = jnp.concatenate([v_ref[pl.ds(pl.multiple_of(kb * tq, tq), tq), :], ones], axis=1)
        for h in range(2):
            s = s_ref[h]
            if masked:
                r = lax.broadcasted_iota(jnp.int32, (tq, tq), 0)
                c = lax.broadcasted_iota(jnp.int32, (tq, tq), 1)
                s = jnp.where(c <= r, s, -jnp.inf)
            chunks = [s[:, i * HEAD_WIDTH:(i + 1) * HEAD_WIDTH] for i in range(n_chunks)]
            m_blk = functools.reduce(jnp.maximum, chunks)
            m_old = m_ref[h]
            m_new = jnp.maximum(m_old, jnp.max(m_blk, axis=-1, keepdims=True))
            scale_old = jnp.exp(m_old - m_new)
            p = jnp.concatenate([jnp.exp(ch - m_new) for ch in chunks], axis=1).astype(BF16)
            pv = jnp.dot(p, vx, preferred_element_type=F32)
            acc_ref[h] = jnp.concatenate([scale_old, scale_old], axis=1) * acc_ref[h] + pv
            m_ref[h] = m_new

    def pair(t, carry):
        scores(2 * t + 1, sb_ref)
        step(2 * t, sa_ref, False)
        scores(2 * t + 2, sa_ref)
        step(2 * t + 1, sb_ref, False)
        return carry

    scores(0, sa_ref)
    lax.fori_loop(0, qi // 2, pair, 0)

    @pl.when(qi % 2 == 0)
    def _():
        step(qi, sa_ref, True)

    @pl.when(qi % 2 == 1)
    def _():
        scores(qi, sb_ref)
        step(qi - 1, sa_ref, False)
        step(qi, sb_ref, True)

    lam = (jnp.exp(jnp.sum(lq1_ref[...] * lk1_ref[...], keepdims=True))
           - jnp.exp(jnp.sum(lq2_ref[...] * lk2_ref[...], keepdims=True)) + LAM_INIT_LAYER0)
    o = (acc_ref[0, :, 0:HEAD_WIDTH] / acc_ref[0, :, HEAD_WIDTH:]
         - lam * (acc_ref[1, :, 0:HEAD_WIDTH] / acc_ref[1, :, HEAD_WIDTH:]))
    o = o * lax.rsqrt(jnp.mean(o * o, axis=-1, keepdims=True) + RMS_EPS)
    o_ref[...] = (o * g_ref[...] * (1.0 - LAM_INIT_LAYER0)).astype(BF16)


def _diff_attention(qkv, lq1, lk1, lq2, lk2, subln_g, batch, seq):
    n, w3 = qkv.shape
    width = w3 // 3
    heads = width // HEAD_WIDTH
    tq = _tile(seq, 512)
    nq = seq // tq
    kern = functools.partial(_diff_attn_kernel, tq=tq)
    vec = lambda: pl.BlockSpec((1, DIFF_HEAD_DIM), lambda b, h, i: (0, 0))
    return pl.pallas_call(
        kern,
        grid=(batch, heads, nq),
        in_specs=[pl.BlockSpec((tq, HEAD_WIDTH), lambda b, h, i: (b * nq + i, h)),
                  pl.BlockSpec((seq, HEAD_WIDTH), lambda b, h, i: (b, heads + h)),
                  pl.BlockSpec((seq, HEAD_WIDTH), lambda b, h, i: (b, 2 * heads + h)),
                  vec(), vec(), vec(), vec(),
                  pl.BlockSpec((1, HEAD_WIDTH), lambda b, h, i: (0, 0))],
        out_specs=pl.BlockSpec((tq, HEAD_WIDTH), lambda b, h, i: (b * nq + i, h)),
        out_shape=jax.ShapeDtypeStruct((n, width), BF16),
        scratch_shapes=[pltpu.VMEM((2, tq, HEAD_WIDTH), BF16),
                        pltpu.VMEM((2, tq, HEAD_WIDTH), F32),
                        pltpu.VMEM((2, tq, 2 * HEAD_WIDTH), F32),
                        pltpu.VMEM((2, tq, tq), F32), pltpu.VMEM((2, tq, tq), F32)],
        compiler_params=_params("arbitrary", "arbitrary", "arbitrary"),
        name="diff_attention",
    )(qkv, qkv, qkv, lq1, lk1, lq2, lk2, subln_g)


def _outproj_ln_kernel(*refs, n_lhs, with_router):
    lhs_refs = refs[:n_lhs]
    w_ref, x_ref, g_ref, b_ref = refs[n_lhs:n_lhs + 4]
    rest = refs[n_lhs + 4:]
    if with_router:
        wr_ref, o_ref, logit_ref = rest
    else:
        (o_ref,) = rest
    tm = x_ref.shape[0]
    n_groups = 1 if with_router else OUTPROJ_ROW_GROUPS
    group = tm // n_groups
    for r in range(n_groups):
        rows = slice(r * group, (r + 1) * group)
        y = None
        off = 0
        for a_ref in lhs_refs:
            kp = a_ref.shape[1]
            part = jnp.dot(a_ref[rows, :], w_ref[off:off + kp, :], preferred_element_type=F32)
            y = part if y is None else y + part
            off += kp
        out = _layer_norm(ALPHA * x_ref[rows, :] + y, g_ref[...], b_ref[...])
        o_ref[rows, :] = out
        if with_router:
            out_hi = out.astype(BF16)
            out_lo = (out - out_hi.astype(F32)).astype(BF16)
            hi_both = jnp.dot(out_hi, wr_ref[...], preferred_element_type=F32)
            lo_hi = jnp.dot(out_lo, wr_ref[:, 0:ROUTER_LANES], preferred_element_type=F32)
            logit_ref[rows, :] = hi_both[:, 0:ROUTER_LANES] + (hi_both[:, ROUTER_LANES:] + lo_hi)


def _outproj_ln(lhs_list, w_bf16, x2d, g, b, w_router_padded=None):
    n, d = x2d.shape
    tm = _tile(n, 512)
    with_router = w_router_padded is not None
    in_specs = [pl.BlockSpec((tm, a.shape[1]), lambda i: (i, 0)) for a in lhs_list]
    in_specs += [pl.BlockSpec(w_bf16.shape, lambda i: (0, 0)),
                 pl.BlockSpec((tm, d), lambda i: (i, 0)),
                 pl.BlockSpec((1, d), lambda i: (0, 0)),
                 pl.BlockSpec((1, d), lambda i: (0, 0))]
    out_specs = [pl.BlockSpec((tm, d), lambda i: (i, 0))]
    out_shape = [jax.ShapeDtypeStruct((n, d), F32)]
    args = list(lhs_list) + [w_bf16, x2d, g, b]
    if with_router:
        wr_hi = w_router_padded.astype(BF16)
        wr_lo = (w_router_padded - wr_hi.astype(F32)).astype(BF16)
        wr = jnp.concatenate([wr_hi, wr_lo], axis=1)
        in_specs.append(pl.BlockSpec(wr.shape, lambda i: (0, 0)))
        out_specs.append(pl.BlockSpec((tm, ROUTER_LANES), lambda i: (i, 0)))
        out_shape.append(jax.ShapeDtypeStruct((n, ROUTER_LANES), F32))
        args.append(wr)
    kern = functools.partial(_outproj_ln_kernel, n_lhs=len(lhs_list), with_router=with_router)
    res = pl.pallas_call(
        kern,
        grid=(n // tm,),
        in_specs=in_specs,
        out_specs=out_specs,
        out_shape=out_shape,
        compiler_params=_params("arbitrary"),
        name="outproj_ln_router" if with_router else "outproj_ln",
    )(*args)
    return res if with_router else res[0]


def _swiglu_hidden(xb, wg, wu):
    gate = jnp.dot(xb, wg, preferred_element_type=F32)
    up = jnp.dot(xb, wu, preferred_element_type=F32)
    return (gate / (1.0 + jnp.exp(-gate)) * up).astype(BF16)


def _swiglu_block(xb, wg, wu, wd):
    return jnp.dot(_swiglu_hidden(xb, wg, wu), wd, preferred_element_type=F32)


def _ffn_ln_kernel(x_ref, wg_ref, wu_ref, wd_ref, g_ref, b_ref, o_ref, xb_ref):
    j = pl.program_id(1)

    @pl.when(j == 0)
    def _():
        xb_ref[...] = x_ref[...].astype(BF16)
        o_ref[...] = jnp.zeros_like(o_ref)

    xb = xb_ref[...]
    tf = wg_ref.shape[1]
    group = tf // FFN_COL_GROUPS
    hidden = [_swiglu_hidden(xb, wg_ref[:, c * group:(c + 1) * group], wu_ref[:, c * group:(c + 1) * group])
              for c in range(FFN_COL_GROUPS)]
    o_ref[...] += jnp.dot(jnp.concatenate(hidden, axis=1), wd_ref[...], preferred_element_type=F32)

    @pl.when(j == pl.num_programs(1) - 1)
    def _():
        o_ref[...] = _layer_norm(ALPHA * x_ref[...] + o_ref[...], g_ref[...], b_ref[...])


def _ffn_ln(x2d, wg, wu, wd, g, b):
    n, d = x2d.shape
    ff = wg.shape[1]
    tm = _tile(n, 1024)
    tf = _tile(ff, 512)
    return pl.pallas_call(
        _ffn_ln_kernel,
        grid=(n // tm, ff // tf),
        in_specs=[pl.BlockSpec((tm, d), lambda i, j: (i, 0)),
                  pl.BlockSpec((d, tf), lambda i, j: (0, j)),
                  pl.BlockSpec((d, tf), lambda i, j: (0, j)),
                  pl.BlockSpec((tf, d), lambda i, j: (j, 0)),
                  pl.BlockSpec((1, d), lambda i, j: (0, 0)),
                  pl.BlockSpec((1, d), lambda i, j: (0, 0))],
        out_specs=pl.BlockSpec((tm, d), lambda i, j: (i, 0)),
        out_shape=jax.ShapeDtypeStruct((n, d), F32),
        scratch_shapes=[pltpu.VMEM((tm, d), BF16)],
        compiler_params=pltpu.CompilerParams(dimension_semantics=("arbitrary", "arbitrary"),
                                             vmem_limit_bytes=BIG_TILE_VMEM_LIMIT_BYTES),
        name="ffn_ln",
    )(x2d, wg, wu, wd, g, b)


def _odd_inproj_kernel(x_ref, wb_ref, wc_ref, wh_ref, cw_ref, o_ref, xb_ref, zs_ref, carry_ref,
                       *, seq, tm):
    i = pl.program_id(0)
    j = pl.program_id(1)

    @pl.when(j == 0)
    def _():
        xb_ref[...] = x_ref[...].astype(BF16)

    xb = xb_ref[...]
    first = (i * tm) % seq == 0
    tn = o_ref.shape[1]
    group = tn // ODD_COL_GROUPS
    for c in range(ODD_COL_GROUPS):
        cols = slice(c * group, (c + 1) * group)
        b_gate = jnp.dot(xb, wb_ref[:, cols], preferred_element_type=F32)
        c_gate = jnp.dot(xb, wc_ref[:, cols], preferred_element_type=F32)
        hx = jnp.dot(xb, wh_ref[:, cols], preferred_element_type=F32)
        z = c_gate * hx
        zs_ref[0:CONV_HALO, cols] = jnp.where(first, 0.0, carry_ref[j, :, cols])
        zs_ref[CONV_HALO:, cols] = z
        carry_ref[j, :, cols] = z[tm - CONV_HALO:, :]
        y = (cw_ref[0:1, cols] * zs_ref[pl.ds(CONV_HALO - 2, tm), cols]
             + cw_ref[1:2, cols] * zs_ref[pl.ds(CONV_HALO - 1, tm), cols]
             + cw_ref[2:3, cols] * z)
        o_ref[:, cols] = (b_gate * y).astype(BF16)


def _odd_inproj(x2d, w_bf16, conv_w, seq):
    n, d = x2d.shape
    cw = w_bf16.shape[1] // 3
    tm = _tile(seq, 1024)
    tn = _tile(cw, 512)
    nj = cw // tn
    kern = functools.partial(_odd_inproj_kernel, seq=seq, tm=tm)
    return pl.pallas_call(
        kern,
        grid=(n // tm, nj),
        in_specs=[pl.BlockSpec((tm, d), lambda i, j: (i, 0)),
                  pl.BlockSpec((d, tn), lambda i, j: (0, j)),
                  pl.BlockSpec((d, tn), lambda i, j: (0, nj + j)),
                  pl.BlockSpec((d, tn), lambda i, j: (0, 2 * nj + j)),
                  pl.BlockSpec((CONV_K, tn), lambda i, j: (0, j))],
        out_specs=pl.BlockSpec((tm, tn), lambda i, j: (i, j)),
        out_shape=jax.ShapeDtypeStruct((n, cw), BF16),
        scratch_shapes=[pltpu.VMEM((tm, d), BF16),
                        pltpu.VMEM((CONV_HALO + tm, tn), F32),
                        pltpu.VMEM((nj, CONV_HALO, tn), F32)],
        compiler_params=_params("arbitrary", "arbitrary"),
        name="odd_inproj",
    )(x2d, w_bf16, w_bf16, w_bf16, conv_w)


def _route_kernel(lt_ref, d1_ref, d2_ref, g1_ref, g2_ref, te_ref, nact_ref, zfill_ref, half2_ref, rank_ref,
                  *, tm):
    e, n = lt_ref.shape
    lt = lt_ref[...]
    eidx = lax.broadcasted_iota(jnp.int32, (e, n), 0)
    v1 = jnp.max(lt, axis=0, keepdims=True)
    i1 = jnp.min(jnp.where(lt == v1, eidx, e), axis=0, keepdims=True)
    oh1 = eidx == i1
    lt2 = jnp.where(oh1, -jnp.inf, lt)
    v2 = jnp.max(lt2, axis=0, keepdims=True)
    i2 = jnp.min(jnp.where(lt2 == v2, eidx, e), axis=0, keepdims=True)
    oh2 = eidx == i2
    ex = jnp.exp(v2 - v1)
    g1_ref[...] = 1.0 / (1.0 + ex)
    g2_ref[...] = ex / (1.0 + ex)

    sel = jnp.where(oh1 | oh2, 1.0, 0.0)
    a = lax.broadcasted_iota(jnp.int32, (CUMSUM_CHUNK, CUMSUM_CHUNK), 0)
    b = lax.broadcasted_iota(jnp.int32, (CUMSUM_CHUNK, CUMSUM_CHUNK), 1)
    upper = jnp.where(a <= b, 1.0, 0.0).astype(BF16)
    running = jnp.zeros((e, 1), F32)
    for c in range(n // CUMSUM_CHUNK):
        cols = slice(c * CUMSUM_CHUNK, (c + 1) * CUMSUM_CHUNK)
        blk = sel[:, cols]
        incl = jnp.dot(blk.astype(BF16), upper, preferred_element_type=F32)
        rank_ref[:, cols] = incl - blk + running
        running = running + incl[:, CUMSUM_CHUNK - 1:CUMSUM_CHUNK]

    log2_tm = tm.bit_length() - 1
    assert tm == 1 << log2_tm
    counts = running.astype(jnp.int32)
    padded = ((counts + (tm - 1)) >> log2_tm) << log2_tm
    e8 = lax.broadcasted_iota(jnp.int32, (e, 1), 0)
    offs = jnp.zeros((e, 1), jnp.int32)
    total = jnp.zeros((1, 1), jnp.int32)
    for k in range(e):
        offs = jnp.where(e8 == k, total, offs)
        total = total + padded[k:k + 1, :]
    ends = offs + padded
    zfill_ref[...] = jnp.where(padded > 0, ends - tm, -1)

    dest = rank_ref[...].astype(jnp.int32) + offs
    d1_ref[...] = jnp.sum(jnp.where(oh1, dest, 0), axis=0, keepdims=True)
    d2_ref[...] = jnp.sum(jnp.where(oh2, dest, 0), axis=0, keepdims=True)

    t = te_ref.shape[1]
    starts = lax.broadcasted_iota(jnp.int32, (e, t), 1) * tm
    te = jnp.sum(jnp.where(ends <= starts, 1, 0), axis=0, keepdims=True)
    te_ref[...] = jnp.minimum(te, e - 1)
    nact_ref[...] = total >> log2_tm
    in_group = (offs <= starts) & (starts < ends)
    half2 = in_group & (starts + tm // 2 < offs + counts)
    half2_ref[...] = jnp.sum(jnp.where(half2, 1, 0), axis=0, keepdims=True)


def _route(logits_t, tm, n_tiles):
    e, n = logits_t.shape
    assert n % CUMSUM_CHUNK == 0
    kern = functools.partial(_route_kernel, tm=tm)
    row_i = jax.ShapeDtypeStruct((1, n), jnp.int32)
    row_f = jax.ShapeDtypeStruct((1, n), F32)
    return pl.pallas_call(
        kern,
        out_shape=[row_i, row_i, row_f, row_f,
                   jax.ShapeDtypeStruct((1, n_tiles), jnp.int32),
                   jax.ShapeDtypeStruct((1, 1), jnp.int32),
                   jax.ShapeDtypeStruct((e, 1), jnp.int32),
                   jax.ShapeDtypeStruct((1, n_tiles), jnp.int32)],
        scratch_shapes=[pltpu.VMEM((e, n), F32)],
        compiler_params=pltpu.CompilerParams(vmem_limit_bytes=VMEM_LIMIT_BYTES),
        name="route",
    )(logits_t)


def _dispatch_kernel(fill_ref, d1_ref, d2_ref, x_ref, xs_ref, zero_ref, sem, zsem, *, td, tm, n_exp, n_tiles):
    @pl.when(pl.program_id(0) == 0)
    def _():
        zero_ref[...] = jnp.zeros_like(zero_ref)

        def fill_copy(row0):
            return pltpu.make_async_copy(zero_ref, xs_ref.at[pl.ds(pl.multiple_of(row0, tm), tm), :], zsem)

        def start_fill(row0):
            fill_copy(row0).start()

        def wait_fill(row0):
            fill_copy(row0).wait()

        fills = [(fill_ref[e] >= 0, fill_ref[e]) for e in range(n_exp)]
        fills += [(t >= fill_ref[n_exp], t * tm) for t in range(n_tiles - n_exp, n_tiles)]
        for wanted, row0 in fills:
            pl.when(wanted)(functools.partial(start_fill, row0))
        for wanted, row0 in fills:
            pl.when(wanted)(functools.partial(wait_fill, row0))

    def row_copy(r, dst_row):
        return pltpu.make_async_copy(x_ref.at[pl.ds(r, 1), :], xs_ref.at[pl.ds(dst_row, 1), :], sem)

    def issue(r, carry):
        row_copy(r, d1_ref[0, r]).start()
        row_copy(r, d2_ref[0, r]).start()
        return carry

    lax.fori_loop(0, td, issue, 0, unroll=DMA_ISSUE_UNROLL)
    for _ in range(TOP_K):
        pltpu.make_async_copy(x_ref, xs_ref.at[pl.ds(0, td), :], sem).wait()


def _dispatch(x2d, d1, d2, fill, tm, n_tiles):
    n, d = x2d.shape
    n_exp = fill.shape[0] - 1
    td = _tile(n, 1024)
    nb = n // td
    kern = functools.partial(_dispatch_kernel, td=td, tm=tm, n_exp=n_exp, n_tiles=n_tiles)
    smem_idx = lambda: pl.BlockSpec((None, 1, td), lambda i: (i, 0, 0), memory_space=pltpu.SMEM)
    return pl.pallas_call(
        kern,
        grid=(nb,),
        in_specs=[pl.BlockSpec(memory_space=pltpu.SMEM), smem_idx(), smem_idx(),
                  pl.BlockSpec((td, d), lambda i: (i, 0))],
        out_specs=pl.BlockSpec(memory_space=pl.ANY),
        out_shape=jax.ShapeDtypeStruct((n_tiles * tm, d), F32),
        scratch_shapes=[pltpu.VMEM((tm, d), F32), pltpu.SemaphoreType.DMA, pltpu.SemaphoreType.DMA],
        compiler_params=_params("arbitrary"),
        name="dispatch",
    )(fill, d1.reshape(nb, 1, td), d2.reshape(nb, 1, td), x2d)


def _moe_kernel(te_ref, nact_ref, half2_ref, xs_hbm, wg_ref, wu_ref, wd_ref, y_ref, stage_ref, xb_ref, sem,
                *, tm):
    i = pl.program_id(0)
    j = pl.program_id(1)
    nact = nact_ref[0]
    active = i < nact
    half = tm // 2

    def tile_copy(t):
        return pltpu.make_async_copy(xs_hbm.at[pl.ds(pl.multiple_of(t * tm, tm), tm), :], stage_ref, sem)

    @pl.when(j == 0)
    def _():
        y_ref[...] = jnp.zeros_like(y_ref)

    @pl.when((i == 0) & (j == 0))
    def _():
        tile_copy(0).start()

    @pl.when(active & (j == 0))
    def _():
        tile_copy(i).wait()
        xb_ref[...] = stage_ref[...].astype(BF16)

    @pl.when((j == pl.num_programs(1) // 2) & (i + 1 < nact))
    def _():
        tile_copy(i + 1).start()

    def accumulate(rows):
        xb = xb_ref[rows, :]
        group = wg_ref.shape[1] // FFN_COL_GROUPS
        hidden = []
        for c in range(FFN_COL_GROUPS):
            cols = slice(c * group, (c + 1) * group)
            hidden.append(_swiglu_hidden(xb, wg_ref[:, cols].astype(BF16), wu_ref[:, cols].astype(BF16)))
        y_ref[rows, :] += jnp.dot(jnp.concatenate(hidden, axis=1), wd_ref[...].astype(BF16),
                                  preferred_element_type=F32)

    @pl.when(active & (half2_ref[i] > 0))
    def _():
        accumulate(slice(None))

    @pl.when(active & (half2_ref[i] == 0))
    def _():
        accumulate(slice(0, half))


def _moe_ffn(xs, te, nact, half2, wg, wu, wd, tm):
    rows, d = xs.shape
    n_exp, _, ff = wg.shape
    tf = _tile(ff, 512)
    nj = ff // tf
    assert nj >= 2
    n_tiles = rows // tm

    def ff_blk(i, j, nact_ref):
        return jnp.where(i < nact_ref[0], j, nj - 1)

    def expert(i, te_ref, nact_ref):
        return te_ref[jnp.minimum(i, nact_ref[0] - 1)]

    grid_spec = pltpu.PrefetchScalarGridSpec(
        num_scalar_prefetch=3,
        grid=(n_tiles, nj),
        in_specs=[pl.BlockSpec(memory_space=pl.ANY),
                  pl.BlockSpec((None, d, tf), lambda i, j, t, a, h: (expert(i, t, a), 0, ff_blk(i, j, a))),
                  pl.BlockSpec((None, d, tf), lambda i, j, t, a, h: (expert(i, t, a), 0, ff_blk(i, j, a))),
                  pl.BlockSpec((None, tf, d), lambda i, j, t, a, h: (expert(i, t, a), ff_blk(i, j, a), 0))],
        out_specs=pl.BlockSpec((tm, d), lambda i, j, t, a, h: (i, 0)),
        scratch_shapes=[pltpu.VMEM((tm, d), F32), pltpu.VMEM((tm, d), BF16), pltpu.SemaphoreType.DMA],
    )
    return pl.pallas_call(
        functools.partial(_moe_kernel, tm=tm),
        grid_spec=grid_spec,
        out_shape=jax.ShapeDtypeStruct((rows, d), F32),
        compiler_params=pltpu.CompilerParams(dimension_semantics=("arbitrary", "arbitrary"),
                                             vmem_limit_bytes=BIG_TILE_VMEM_LIMIT_BYTES),
        name="moe_ffn",
    )(te, nact, half2, xs, wg, wu, wd)


def _combine_kernel(d1_ref, d2_ref, d1n_ref, d2n_ref, x_ref, g1_ref, g2_ref, lg_ref, lb_ref, y_ref, o_ref,
                    buf_ref, sems, *, tc):
    i = pl.program_id(0)

    def gather(da_ref, db_ref, s):
        def issue(r, carry):
            for k, dk_ref in enumerate((da_ref, db_ref)):
                pltpu.make_async_copy(y_ref.at[pl.ds(dk_ref[0, r], 1), :],
                                      buf_ref.at[s, k, pl.ds(r, 1), :], sems.at[s]).start()
            return carry

        lax.fori_loop(0, tc, issue, 0, unroll=DMA_ISSUE_UNROLL)

    @pl.when(i == 0)
    def _():
        gather(d1_ref, d2_ref, 0)

    def tile(s):
        @pl.when(i + 1 < pl.num_programs(0))
        def _():
            gather(d1n_ref, d2n_ref, 1 - s)

        for k in range(TOP_K):
            pltpu.make_async_copy(y_ref.at[pl.ds(0, tc), :], buf_ref.at[s, k], sems.at[s]).wait()

        f = g1_ref[...] * buf_ref[s, 0] + g2_ref[...] * buf_ref[s, 1]
        o_ref[...] = _layer_norm(ALPHA * x_ref[...] + f, lg_ref[...], lb_ref[...])

    pl.when(i % 2 == 0)(functools.partial(tile, 0))
    pl.when(i % 2 == 1)(functools.partial(tile, 1))


def _combine_ln(x2d, y, d1, d2, g1, g2, lg, lb):
    n, d = x2d.shape
    tc = _tile(n, 256)
    nb = n // tc
    kern = functools.partial(_combine_kernel, tc=tc)
    cur = lambda: pl.BlockSpec((None, 1, tc), lambda i: (i, 0, 0), memory_space=pltpu.SMEM)
    nxt = lambda: pl.BlockSpec((None, 1, tc), lambda i: (jnp.minimum(i + 1, nb - 1), 0, 0),
                               memory_space=pltpu.SMEM)
    d1b, d2b = d1.reshape(nb, 1, tc), d2.reshape(nb, 1, tc)
    return pl.pallas_call(
        kern,
        grid=(nb,),
        in_specs=[cur(), cur(), nxt(), nxt(),
                  pl.BlockSpec((tc, d), lambda i: (i, 0)),
                  pl.BlockSpec((tc, 1), lambda i: (i, 0)),
                  pl.BlockSpec((tc, 1), lambda i: (i, 0)),
                  pl.BlockSpec((1, d), lambda i: (0, 0)),
                  pl.BlockSpec((1, d), lambda i: (0, 0)),
                  pl.BlockSpec(memory_space=pl.ANY)],
        out_specs=pl.BlockSpec((tc, d), lambda i: (i, 0)),
        out_shape=jax.ShapeDtypeStruct((n, d), F32),
        scratch_shapes=[pltpu.VMEM((2, TOP_K, tc, d), F32), pltpu.SemaphoreType.DMA((2,))],
        compiler_params=_params("arbitrary"),
        name="combine_ln",
    )(d1b, d2b, d1b, d2b, x2d, g1.reshape(n, 1), g2.reshape(n, 1), lg, lb, y)


def kernel(x, ev_w_in, ev_w_pool, ev_pool_scale, ev_lam_q1, ev_lam_k1, ev_lam_q2, ev_lam_k2, ev_subln_g, ev_w_out, ev_ln1_g, ev_ln1_b, ev_ffn_wg, ev_ffn_wu, ev_ffn_wd, ev_ln2_g, ev_ln2_b, od_w_in, od_conv_w, od_w_out, od_ln1_g, od_ln1_b, od_router, od_exp_wg, od_exp_wu, od_exp_wd, od_ln2_g, od_ln2_b):
    batch, seq, d = x.shape
    n = batch * seq
    assert ev_w_in.shape[0] == 1 and od_w_in.shape[0] == 1, "DEPTH == 2: one even and one odd layer"
    h = x.reshape(n, d)
    bf = lambda w: w.astype(BF16)
    row = lambda v: v.reshape(1, -1)

    pool_width = ev_w_pool.shape[1] * ev_w_pool.shape[2]
    u_pool, qkv = _even_inproj(h, bf(ev_w_in[0]), pool_width)
    a = _pool_mixer(u_pool, bf(ev_w_pool[0]), row(ev_pool_scale[0]), seq)
    o = _diff_attention(qkv, row(ev_lam_q1[0]), row(ev_lam_k1[0]), row(ev_lam_q2[0]), row(ev_lam_k2[0]),
                        row(ev_subln_g[0]), batch, seq)
    h = _outproj_ln([a, o], bf(ev_w_out[0]), h, row(ev_ln1_g[0]), row(ev_ln1_b[0]))
    h = _ffn_ln(h, bf(ev_ffn_wg[0]), bf(ev_ffn_wu[0]), bf(ev_ffn_wd[0]), row(ev_ln2_g[0]), row(ev_ln2_b[0]))

    gated = _odd_inproj(h, bf(od_w_in[0]), od_conv_w[0].reshape(CONV_K, -1), seq)
    n_exp = od_router.shape[-1]
    w_router = jnp.pad(od_router[0], ((0, 0), (0, ROUTER_LANES - n_exp)))
    h, logits = _outproj_ln([gated], bf(od_w_out[0]), h, row(od_ln1_g[0]), row(od_ln1_b[0]), w_router)

    tm = _tile(n, 1024)
    n_tiles = (TOP_K * n) // tm + n_exp
    d1, d2, g1, g2, te, nact, zfill, half2 = _route(logits[:, :n_exp].T, tm, n_tiles)
    fill = jnp.concatenate([zfill.reshape(-1), nact.reshape(-1)])
    xs = _dispatch(h, d1, d2, fill, tm, n_tiles)
    y = _moe_ffn(xs, te.reshape(-1), nact.reshape(-1), half2.reshape(-1), od_exp_wg[0], od_exp_wu[0],
                 od_exp_wd[0], tm)
    h = _combine_ln(h, y, d1, d2, g1, g2, row(od_ln2_g[0]), row(od_ln2_b[0]))
    return h.reshape(batch, seq, d)
```

```python
import functools
import math

import jax
import jax.numpy as jnp
from jax import lax
from jax.experimental import pallas as pl
from jax.experimental.pallas import tpu as pltpu

F32 = jnp.float32
BF16 = jnp.bfloat16

DEPTH = 2
ALPHA = (2 * DEPTH) ** 0.25
LN_EPS = 1e-5
RMS_EPS = 1e-5
POOL_WINDOWS = (2, 4, 8, 16)
POOL_HALO = 16
DIFF_HEAD_DIM = 64
HEAD_WIDTH = 2 * DIFF_HEAD_DIM
LAM_INIT_LAYER0 = 0.8 - 0.6 * math.exp(-0.3 * 0)
CONV_K = 3
CONV_HALO = 8
TOP_K = 2
ROUTER_LANES = 128
CUMSUM_CHUNK = 256
ODD_COL_GROUPS = 2
OUTPROJ_ROW_GROUPS = 2
FFN_COL_GROUPS = 2
DMA_ISSUE_UNROLL = 8

VMEM_LIMIT_BYTES = 56 * 1024 * 1024
BIG_TILE_VMEM_LIMIT_BYTES = 61 * 1024 * 1024


def _tile(n, pref):
    t = min(n, pref)
    assert n % t == 0, (n, t)
    return t


def _params(*sem):
    return pltpu.CompilerParams(dimension_semantics=sem, vmem_limit_bytes=VMEM_LIMIT_BYTES)


def _layer_norm(v, g, b):
    mu = jnp.mean(v, axis=-1, keepdims=True)
    d = v - mu
    var = jnp.mean(d * d, axis=-1, keepdims=True)
    return d * lax.rsqrt(var + LN_EPS) * g + b


def _even_inproj_kernel(x_ref, w_ref, pool_ref, qkv_ref, xb_ref):
    j = pl.program_id(1)

    @pl.when(j == 0)
    def _():
        xb_ref[...] = x_ref[...].astype(BF16)

    acc = jnp.dot(xb_ref[...], w_ref[...], preferred_element_type=F32)

    @pl.when(j == 0)
    def _():
        pool_ref[...] = acc

    @pl.when(j > 0)
    def _():
        qkv_ref[...] = acc.astype(BF16)


def _even_inproj(x2d, w_bf16, pool_width):
    n, d = x2d.shape
    width = w_bf16.shape[1]
    tn = pool_width
    assert (width - pool_width) % tn == 0
    tm = _tile(n, 1024)
    return pl.pallas_call(
        _even_inproj_kernel,
        grid=(n // tm, width // tn),
        in_specs=[pl.BlockSpec((tm, d), lambda i, j: (i, 0)),
                  pl.BlockSpec((d, tn), lambda i, j: (0, j))],
        out_specs=[pl.BlockSpec((tm, tn), lambda i, j: (i, 0)),
                   pl.BlockSpec((tm, tn), lambda i, j: (i, jnp.maximum(j - 1, 0)))],
        out_shape=[jax.ShapeDtypeStruct((n, pool_width), F32),
                   jax.ShapeDtypeStruct((n, width - pool_width), BF16)],
        scratch_shapes=[pltpu.VMEM((tm, d), BF16)],
        compiler_params=_params("arbitrary", "arbitrary"),
        name="even_inproj",
    )(x2d, w_bf16)


def _pool_kernel(u_ref, halo_ref, w_ref, scale_ref, o_ref, buf_ref, *, seq, tp, cg):
    i = pl.program_id(0)
    first = (i * tp) % seq == 0
    pos = (i * tp) % seq + lax.broadcasted_iota(jnp.int32, (tp, 1), 0)
    for g, window in enumerate(POOL_WINDOWS):
        cols = slice(g * cg, (g + 1) * cg)
        u = u_ref[:, cols]
        buf_ref[0:POOL_HALO, :] = jnp.where(first, 0.0, halo_ref[:, cols])
        buf_ref[POOL_HALO:, :] = u
        total = u
        for back in range(1, window):
            total = total + buf_ref[pl.ds(POOL_HALO - back, tp), :]
        count = jnp.minimum(pos + 1, window).astype(F32)
        pooled = total / count - u
        mixed = jnp.dot(pooled.astype(BF16), w_ref[g], preferred_element_type=F32)
        o_ref[:, cols] = (mixed * scale_ref[:, cols]).astype(BF16)


def _pool_mixer(u, w_pool_bf16, scale, seq):
    n, pw = u.shape
    groups, cg, _ = w_pool_bf16.shape
    assert groups == len(POOL_WINDOWS) and groups * cg == pw
    tp = _tile(seq, 512)
    halo_blocks = tp // POOL_HALO
    kern = functools.partial(_pool_kernel, seq=seq, tp=tp, cg=cg)
    return pl.pallas_call(
        kern,
        grid=(n // tp,),
        in_specs=[pl.BlockSpec((tp, pw), lambda i: (i, 0)),
                  pl.BlockSpec((POOL_HALO, pw), lambda i: (jnp.maximum(i * halo_blocks - 1, 0), 0)),
                  pl.BlockSpec((groups, cg, cg), lambda i: (0, 0, 0)),
                  pl.BlockSpec((1, pw), lambda i: (0, 0))],
        out_specs=pl.BlockSpec((tp, pw), lambda i: (i, 0)),
        out_shape=jax.ShapeDtypeStruct((n, pw), BF16),
        scratch_shapes=[pltpu.VMEM((POOL_HALO + tp, cg), F32)],
        compiler_params=_params("arbitrary"),
        name="pool_mixer",
    )(u, u, w_pool_bf16, scale)


def _diff_attn_kernel(q_ref, k_ref, v_ref, lq1_ref, lk1_ref, lq2_ref, lk2_ref, g_ref, o_ref,
                      qs_ref, m_ref, acc_ref, sa_ref, sb_ref, *, tq):
    qi = pl.program_id(2)
    q = q_ref[...]
    lane = lax.broadcasted_iota(jnp.int32, q.shape, 1)
    zero = jnp.zeros_like(q)
    scale = jnp.asarray(DIFF_HEAD_DIM ** -0.5, BF16)
    qs_ref[0] = jnp.where(lane < DIFF_HEAD_DIM, q, zero) * scale
    qs_ref[1] = jnp.where(lane >= DIFF_HEAD_DIM, q, zero) * scale
    m_ref[...] = jnp.full(m_ref.shape, -jnp.inf, F32)
    acc_ref[...] = jnp.zeros(acc_ref.shape, F32)
    ones = jnp.ones((tq, HEAD_WIDTH), BF16)
    n_chunks = tq // HEAD_WIDTH

    def scores(kb, s_ref):
        k = k_ref[pl.ds(pl.multiple_of(kb * tq, tq), tq), :]
        for h in range(2):
            s_ref[h] = lax.dot_general(qs_ref[h], k, (((1,), (1,)), ((), ())), preferred_element_type=F32)

    def step(kb, s_ref, masked):
        vx = jnp.concatenate([v_ref[pl.ds(pl.multiple_of(kb * tq, tq), tq), :], ones], axis=1)
        for h in range(2):
            s = s_ref[h]
            if masked:
                r = lax.broadcasted_iota(jnp.int32, (tq, tq), 0)
                c = lax.broadcasted_iota(jnp.int32, (tq, tq), 1)
                s = jnp.where(c <= r, s, -jnp.inf)
            chunks = [s[:, i * HEAD_WIDTH:(i + 1) * HEAD_WIDTH] for i in range(n_chunks)]
            m_blk = functools.reduce(jnp.maximum, chunks)
            m_old = m_ref[h]
            m_new = jnp.maximum(m_old, jnp.max(m_blk, axis=-1, keepdims=True))
            scale_old = jnp.exp(m_old - m_new)
            p = jnp.concatenate([jnp.exp(ch - m_new) for ch in chunks], axis=1).astype(BF16)
            pv = jnp.dot(p, vx, preferred_element_type=F32)
            acc_ref[h] = jnp.concatenate([scale_old, scale_old], axis=1) * acc_ref[h] + pv
            m_ref[h] = m_new

    def pair(t, carry):
        scores(2 * t + 1, sb_ref)
        step(2 * t, sa_ref, False)
        scores(2 * t + 2, sa_ref)
        step(2 * t + 1, sb_ref, False)
        return carry

    scores(0, sa_ref)
    lax.fori_loop(0, qi // 2, pair, 0)

    @pl.when(qi % 2 == 0)
    def _():
        step(qi, sa_ref, True)

    @pl.when(qi % 2 == 1)
    def _():
        scores(qi, sb_ref)
        step(qi - 1, sa_ref, False)
        step(qi, sb_ref, True)

    lam = (jnp.exp(jnp.sum(lq1_ref[...] * lk1_ref[...], keepdims=True))
           - jnp.exp(jnp.sum(lq2_ref[...] * lk2_ref[...], keepdims=True)) + LAM_INIT_LAYER0)
    o = (acc_ref[0, :, 0:HEAD_WIDTH] / acc_ref[0, :, HEAD_WIDTH:]
         - lam * (acc_ref[1, :, 0:HEAD_WIDTH] / acc_ref[1, :, HEAD_WIDTH:]))
    o = o * lax.rsqrt(jnp.mean(o * o, axis=-1, keepdims=True) + RMS_EPS)
    o_ref[...] = (o * g_ref[...] * (1.0 - LAM_INIT_LAYER0)).astype(BF16)


def _diff_attention(qkv, lq1, lk1, lq2, lk2, subln_g, batch, seq):
    n, w3 = qkv.shape
    width = w3 // 3
    heads = width // HEAD_WIDTH
    tq = _tile(seq, 512)
    nq = seq // tq
    kern = functools.partial(_diff_attn_kernel, tq=tq)
    vec = lambda: pl.BlockSpec((1, DIFF_HEAD_DIM), lambda b, h, i: (0, 0))
    return pl.pallas_call(
        kern,
        grid=(batch, heads, nq),
        in_specs=[pl.BlockSpec((tq, HEAD_WIDTH), lambda b, h, i: (b * nq + i, h)),
                  pl.BlockSpec((seq, HEAD_WIDTH), lambda b, h, i: (b, heads + h)),
                  pl.BlockSpec((seq, HEAD_WIDTH), lambda b, h, i: (b, 2 * heads + h)),
                  vec(), vec(), vec(), vec(),
                  pl.BlockSpec((1, HEAD_WIDTH), lambda b, h, i: (0, 0))],
        out_specs=pl.BlockSpec((tq, HEAD_WIDTH), lambda b, h, i: (b * nq + i, h)),
        out_shape=jax.ShapeDtypeStruct((n, width), BF16),
        scratch_shapes=[pltpu.VMEM((2, tq, HEAD_WIDTH), BF16),
                        pltpu.VMEM((2, tq, HEAD_WIDTH), F32),
                        pltpu.VMEM((2, tq, 2 * HEAD_WIDTH), F32),
                        pltpu.VMEM((2, tq, tq), F32), pltpu.VMEM((2, tq, tq), F32)],
        compiler_params=_params("arbitrary", "arbitrary", "arbitrary"),
        name="diff_attention",
    )(qkv, qkv, qkv, lq1, lk1, lq2, lk2, subln_g)


def _outproj_ln_kernel(*refs, n_lhs, with_router):
    lhs_refs = refs[:n_lhs]
    w_ref, x_ref, g_ref, b_ref = refs[n_lhs:n_lhs + 4]
    rest = refs[n_lhs + 4:]
    if with_router:
        wr_ref, o_ref, logit_ref = rest
    else:
        (o_ref,) = rest
    tm = x_ref.shape[0]
    n_groups = 1 if with_router else OUTPROJ_ROW_GROUPS
    group = tm // n_groups
    for r in range(n_groups):
        rows = slice(r * group, (r + 1) * group)
        y = None
        off = 0
        for a_ref in lhs_refs:
            kp = a_ref.shape[1]
            part = jnp.dot(a_ref[rows, :], w_ref[off:off + kp, :], preferred_element_type=F32)
            y = part if y is None else y + part
            off += kp
        out = _layer_norm(ALPHA * x_ref[rows, :] + y, g_ref[...], b_ref[...])
        o_ref[rows, :] = out
        if with_router:
            out_hi = out.astype(BF16)
            out_lo = (out - out_hi.astype(F32)).astype(BF16)
            hi_both = jnp.dot(out_hi, wr_ref[...], preferred_element_type=F32)
            lo_hi = jnp.dot(out_lo, wr_ref[:, 0:ROUTER_LANES], preferred_element_type=F32)
            logit_ref[rows, :] = hi_both[:, 0:ROUTER_LANES] + (hi_both[:, ROUTER_LANES:] + lo_hi)


def _outproj_ln(lhs_list, w_bf16, x2d, g, b, w_router_padded=None):
    n, d = x2d.shape
    tm = _tile(n, 512)
    with_router = w_router_padded is not None
    in_specs = [pl.BlockSpec((tm, a.shape[1]), lambda i: (i, 0)) for a in lhs_list]
    in_specs += [pl.BlockSpec(w_bf16.shape, lambda i: (0, 0)),
                 pl.BlockSpec((tm, d), lambda i: (i, 0)),
                 pl.BlockSpec((1, d), lambda i: (0, 0)),
                 pl.BlockSpec((1, d), lambda i: (0, 0))]
    out_specs = [pl.BlockSpec((tm, d), lambda i: (i, 0))]
    out_shape = [jax.ShapeDtypeStruct((n, d), F32)]
    args = list(lhs_list) + [w_bf16, x2d, g, b]
    if with_router:
        wr_hi = w_router_padded.astype(BF16)
        wr_lo = (w_router_padded - wr_hi.astype(F32)).astype(BF16)
        wr = jnp.concatenate([wr_hi, wr_lo], axis=1)
        in_specs.append(pl.BlockSpec(wr.shape, lambda i: (0, 0)))
        out_specs.append(pl.BlockSpec((tm, ROUTER_LANES), lambda i: (i, 0)))
        out_shape.append(jax.ShapeDtypeStruct((n, ROUTER_LANES), F32))
        args.append(wr)
    kern = functools.partial(_outproj_ln_kernel, n_lhs=len(lhs_list), with_router=with_router)
    res = pl.pallas_call(
        kern,
        grid=(n // tm,),
        in_specs=in_specs,
        out_specs=out_specs,
        out_shape=out_shape,
        compiler_params=_params("arbitrary"),
        name="outproj_ln_router" if with_router else "outproj_ln",
    )(*args)
    return res if with_router else res[0]


def _swiglu_hidden(xb, wg, wu):
    gate = jnp.dot(xb, wg, preferred_element_type=F32)
    up = jnp.dot(xb, wu, preferred_element_type=F32)
    return (gate / (1.0 + jnp.exp(-gate)) * up).astype(BF16)


def _swiglu_block(xb, wg, wu, wd):
    return jnp.dot(_swiglu_hidden(xb, wg, wu), wd, preferred_element_type=F32)


def _ffn_ln_kernel(x_ref, wg_ref, wu_ref, wd_ref, g_ref, b_ref, o_ref, xb_ref):
    j = pl.program_id(1)

    @pl.when(j == 0)
    def _():
        xb_ref[...] = x_ref[...].astype(BF16)
        o_ref[...] = jnp.zeros_like(o_ref)

    xb = xb_ref[...]
    tf = wg_ref.shape[1]
    group = tf // FFN_COL_GROUPS
    hidden = [_swiglu_hidden(xb, wg_ref[:, c * group:(c + 1) * group], wu_ref[:, c * group:(c + 1) * group])
              for c in range(FFN_COL_GROUPS)]
    o_ref[...] += jnp.dot(jnp.concatenate(hidden, axis=1), wd_ref[...], preferred_element_type=F32)

    @pl.when(j == pl.num_programs(1) - 1)
    def _():
        o_ref[...] = _layer_norm(ALPHA * x_ref[...] + o_ref[...], g_ref[...], b_ref[...])


def _ffn_ln(x2d, wg, wu, wd, g, b):
    n, d = x2d.shape
    ff = wg.shape[1]
    tm = _tile(n, 1024)
    tf = _tile(ff, 512)
    return pl.pallas_call(
        _ffn_ln_kernel,
        grid=(n // tm, ff // tf),
        in_specs=[pl.BlockSpec((tm, d), lambda i, j: (i, 0)),
                  pl.BlockSpec((d, tf), lambda i, j: (0, j)),
                  pl.BlockSpec((d, tf), lambda i, j: (0, j)),
                  pl.BlockSpec((tf, d), lambda i, j: (j, 0)),
                  pl.BlockSpec((1, d), lambda i, j: (0, 0)),
                  pl.BlockSpec((1, d), lambda i, j: (0, 0))],
        out_specs=pl.BlockSpec((tm, d), lambda i, j: (i, 0)),
        out_shape=jax.ShapeDtypeStruct((n, d), F32),
        scratch_shapes=[pltpu.VMEM((tm, d), BF16)],
        compiler_params=pltpu.CompilerParams(dimension_semantics=("arbitrary", "arbitrary"),
                                             vmem_limit_bytes=BIG_TILE_VMEM_LIMIT_BYTES),
        name="ffn_ln",
    )(x2d, wg, wu, wd, g, b)


def _odd_inproj_kernel(x_ref, wb_ref, wc_ref, wh_ref, cw_ref, o_ref, xb_ref, zs_ref, carry_ref,
                       *, seq, tm):
    i = pl.program_id(0)
    j = pl.program_id(1)

    @pl.when(j == 0)
    def _():
        xb_ref[...] = x_ref[...].astype(BF16)

    xb = xb_ref[...]
    first = (i * tm) % seq == 0
    tn = o_ref.shape[1]
    group = tn // ODD_COL_GROUPS
    for c in range(ODD_COL_GROUPS):
        cols = slice(c * group, (c + 1) * group)
        b_gate = jnp.dot(xb, wb_ref[:, cols], preferred_element_type=F32)
        c_gate = jnp.dot(xb, wc_ref[:, cols], preferred_element_type=F32)
        hx = jnp.dot(xb, wh_ref[:, cols], preferred_element_type=F32)
        z = c_gate * hx
        zs_ref[0:CONV_HALO, cols] = jnp.where(first, 0.0, carry_ref[j, :, cols])
        zs_ref[CONV_HALO:, cols] = z
        carry_ref[j, :, cols] = z[tm - CONV_HALO:, :]
        y = (cw_ref[0:1, cols] * zs_ref[pl.ds(CONV_HALO - 2, tm), cols]
             + cw_ref[1:2, cols] * zs_ref[pl.ds(CONV_HALO - 1, tm), cols]
             + cw_ref[2:3, cols] * z)
        o_ref[:, cols] = (b_gate * y).astype(BF16)


def _odd_inproj(x2d, w_bf16, conv_w, seq):
    n, d = x2d.shape
    cw = w_bf16.shape[1] // 3
    tm = _tile(seq, 1024)
    tn = _tile(cw, 512)
    nj = cw // tn
    kern = functools.partial(_odd_inproj_kernel, seq=seq, tm=tm)
    return pl.pallas_call(
        kern,
        grid=(n // tm, nj),
        in_specs=[pl.BlockSpec((tm, d), lambda i, j: (i, 0)),
                  pl.BlockSpec((d, tn), lambda i, j: (0, j)),
                  pl.BlockSpec((d, tn), lambda i, j: (0, nj + j)),
                  pl.BlockSpec((d, tn), lambda i, j: (0, 2 * nj + j)),
                  pl.BlockSpec((CONV_K, tn), lambda i, j: (0, j))],
        out_specs=pl.BlockSpec((tm, tn), lambda i, j: (i, j)),
        out_shape=jax.ShapeDtypeStruct((n, cw), BF16),
        scratch_shapes=[pltpu.VMEM((tm, d), BF16),
                        pltpu.VMEM((CONV_HALO + tm, tn), F32),
                        pltpu.VMEM((nj, CONV_HALO, tn), F32)],
        compiler_params=_params("arbitrary", "arbitrary"),
        name="odd_inproj",
    )(x2d, w_bf16, w_bf16, w_bf16, conv_w)


def _route_kernel(lt_ref, d1_ref, d2_ref, g1_ref, g2_ref, te_ref, nact_ref, zfill_ref, half2_ref, rank_ref,
                  *, tm):
    e, n = lt_ref.shape
    lt = lt_ref[...]
    eidx = lax.broadcasted_iota(jnp.int32, (e, n), 0)
    v1 = jnp.max(lt, axis=0, keepdims=True)
    i1 = jnp.min(jnp.where(lt == v1, eidx, e), axis=0, keepdims=True)
    oh1 = eidx == i1
    lt2 = jnp.where(oh1, -jnp.inf, lt)
    v2 = jnp.max(lt2, axis=0, keepdims=True)
    i2 = jnp.min(jnp.where(lt2 == v2, eidx, e), axis=0, keepdims=True)
    oh2 = eidx == i2
    ex = jnp.exp(v2 - v1)
    g1_ref[...] = 1.0 / (1.0 + ex)
    g2_ref[...] = ex / (1.0 + ex)

    sel = jnp.where(oh1 | oh2, 1.0, 0.0)
    a = lax.broadcasted_iota(jnp.int32, (CUMSUM_CHUNK, CUMSUM_CHUNK), 0)
    b = lax.broadcasted_iota(jnp.int32, (CUMSUM_CHUNK, CUMSUM_CHUNK), 1)
    upper = jnp.where(a <= b, 1.0, 0.0).astype(BF16)
    running = jnp.zeros((e, 1), F32)
    for c in range(n // CUMSUM_CHUNK):
        cols = slice(c * CUMSUM_CHUNK, (c + 1) * CUMSUM_CHUNK)
        blk = sel[:, cols]
        incl = jnp.dot(blk.astype(BF16), upper, preferred_element_type=F32)
        rank_ref[:, cols] = incl - blk + running
        running = running + incl[:, CUMSUM_CHUNK - 1:CUMSUM_CHUNK]

    log2_tm = tm.bit_length() - 1
    assert tm == 1 << log2_tm
    counts = running.astype(jnp.int32)
    padded = ((counts + (tm - 1)) >> log2_tm) << log2_tm
    e8 = lax.broadcasted_iota(jnp.int32, (e, 1), 0)
    offs = jnp.zeros((e, 1), jnp.int32)
    total = jnp.zeros((1, 1), jnp.int32)
    for k in range(e):
        offs = jnp.where(e8 == k, total, offs)
        total = total + padded[k:k + 1, :]
    ends = offs + padded
    zfill_ref[...] = jnp.where(padded > 0, ends - tm, -1)

    dest = rank_ref[...].astype(jnp.int32) + offs
    d1_ref[...] = jnp.sum(jnp.where(oh1, dest, 0), axis=0, keepdims=True)
    d2_ref[...] = jnp.sum(jnp.where(oh2, dest, 0), axis=0, keepdims=True)

    t = te_ref.shape[1]
    starts = lax.broadcasted_iota(jnp.int32, (e, t), 1) * tm
    te = jnp.sum(jnp.where(ends <= starts, 1, 0), axis=0, keepdims=True)
    te_ref[...] = jnp.minimum(te, e - 1)
    nact_ref[...] = total >> log2_tm
    in_group = (offs <= starts) & (starts < ends)
    half2 = in_group & (starts + tm // 2 < offs + counts)
    half2_ref[...] = jnp.sum(jnp.where(half2, 1, 0), axis=0, keepdims=True)


def _route(logits_t, tm, n_tiles):
    e, n = logits_t.shape
    assert n % CUMSUM_CHUNK == 0
    kern = functools.partial(_route_kernel, tm=tm)
    row_i = jax.ShapeDtypeStruct((1, n), jnp.int32)
    row_f = jax.ShapeDtypeStruct((1, n), F32)
    return pl.pallas_call(
        kern,
        out_shape=[row_i, row_i, row_f, row_f,
                   jax.ShapeDtypeStruct((1, n_tiles), jnp.int32),
                   jax.ShapeDtypeStruct((1, 1), jnp.int32),
                   jax.ShapeDtypeStruct((e, 1), jnp.int32),
                   jax.ShapeDtypeStruct((1, n_tiles), jnp.int32)],
        scratch_shapes=[pltpu.VMEM((e, n), F32)],
        compiler_params=pltpu.CompilerParams(vmem_limit_bytes=VMEM_LIMIT_BYTES),
        name="route",
    )(logits_t)


def _dispatch_kernel(fill_ref, d1_ref, d2_ref, x_ref, xs_ref, zero_ref, sem, zsem, *, td, tm, n_exp, n_tiles):
    @pl.when(pl.program_id(0) == 0)
    def _():
        zero_ref[...] = jnp.zeros_like(zero_ref)

        def fill_copy(row0):
            return pltpu.make_async_copy(zero_ref, xs_ref.at[pl.ds(pl.multiple_of(row0, tm), tm), :], zsem)

        def start_fill(row0):
            fill_copy(row0).start()

        def wait_fill(row0):
            fill_copy(row0).wait()

        fills = [(fill_ref[e] >= 0, fill_ref[e]) for e in range(n_exp)]
        fills += [(t >= fill_ref[n_exp], t * tm) for t in range(n_tiles - n_exp, n_tiles)]
        for wanted, row0 in fills:
            pl.when(wanted)(functools.partial(start_fill, row0))
        for wanted, row0 in fills:
            pl.when(wanted)(functools.partial(wait_fill, row0))

    def row_copy(r, dst_row):
        return pltpu.make_async_copy(x_ref.at[pl.ds(r, 1), :], xs_ref.at[pl.ds(dst_row, 1), :], sem)

    def issue(r, carry):
        row_copy(r, d1_ref[0, r]).start()
        row_copy(r, d2_ref[0, r]).start()
        return carry

    lax.fori_loop(0, td, issue, 0, unroll=True)
    for _ in range(TOP_K):
        pltpu.make_async_copy(x_ref, xs_ref.at[pl.ds(0, td), :], sem).wait()


def _dispatch(x2d, d1, d2, fill, tm, n_tiles):
    n, d = x2d.shape
    n_exp = fill.shape[0] - 1
    td = _tile(n, 1024)
    nb = n // td
    kern = functools.partial(_dispatch_kernel, td=td, tm=tm, n_exp=n_exp, n_tiles=n_tiles)
    smem_idx = lambda: pl.BlockSpec((None, 1, td), lambda i: (i, 0, 0), memory_space=pltpu.SMEM)
    return pl.pallas_call(
        kern,
        grid=(nb,),
        in_specs=[pl.BlockSpec(memory_space=pltpu.SMEM), smem_idx(), smem_idx(),
                  pl.BlockSpec((td, d), lambda i: (i, 0))],
        out_specs=pl.BlockSpec(memory_space=pl.ANY),
        out_shape=jax.ShapeDtypeStruct((n_tiles * tm, d), F32),
        scratch_shapes=[pltpu.VMEM((tm, d), F32), pltpu.SemaphoreType.DMA, pltpu.SemaphoreType.DMA],
        compiler_params=_params("arbitrary"),
        name="dispatch",
    )(fill, d1.reshape(nb, 1, td), d2.reshape(nb, 1, td), x2d)


def _moe_kernel(te_ref, nact_ref, half2_ref, xs_hbm, wg_ref, wu_ref, wd_ref, y_ref, stage_ref, xb_ref, sem,
                *, tm):
    i = pl.program_id(0)
    j = pl.program_id(1)
    nact = nact_ref[0]
    active = i < nact
    half = tm // 2

    def tile_copy(t):
        return pltpu.make_async_copy(xs_hbm.at[pl.ds(pl.multiple_of(t * tm, tm), tm), :], stage_ref, sem)

    @pl.when(j == 0)
    def _():
        y_ref[...] = jnp.zeros_like(y_ref)

    @pl.when((i == 0) & (j == 0))
    def _():
        tile_copy(0).start()

    @pl.when(active & (j == 0))
    def _():
        tile_copy(i).wait()
        xb_ref[...] = stage_ref[...].astype(BF16)

    @pl.when((j == pl.num_programs(1) // 2) & (i + 1 < nact))
    def _():
        tile_copy(i + 1).start()

    def accumulate(rows):
        xb = xb_ref[rows, :]
        group = wg_ref.shape[1] // FFN_COL_GROUPS
        hidden = []
        for c in range(FFN_COL_GROUPS):
            cols = slice(c * group, (c + 1) * group)
            hidden.append(_swiglu_hidden(xb, wg_ref[:, cols].astype(BF16), wu_ref[:, cols].astype(BF16)))
        y_ref[rows, :] += jnp.dot(jnp.concatenate(hidden, axis=1), wd_ref[...].astype(BF16),
                                  preferred_element_type=F32)

    @pl.when(active & (half2_ref[i] > 0))
    def _():
        accumulate(slice(None))

    @pl.when(active & (half2_ref[i] == 0))
    def _():
        accumulate(slice(0, half))


def _moe_ffn(xs, te, nact, half2, wg, wu, wd, tm):
    rows, d = xs.shape
    n_exp, _, ff = wg.shape
    tf = _tile(ff, 512)
    nj = ff // tf
    assert nj >= 2
    n_tiles = rows // tm

    def ff_blk(i, j, nact_ref):
        return jnp.where(i < nact_ref[0], j, nj - 1)

    def expert(i, te_ref, nact_ref):
        return te_ref[jnp.minimum(i, nact_ref[0] - 1)]

    grid_spec = pltpu.PrefetchScalarGridSpec(
        num_scalar_prefetch=3,
        grid=(n_tiles, nj),
        in_specs=[pl.BlockSpec(memory_space=pl.ANY),
                  pl.BlockSpec((None, d, tf), lambda i, j, t, a, h: (expert(i, t, a), 0, ff_blk(i, j, a))),
                  pl.BlockSpec((None, d, tf), lambda i, j, t, a, h: (expert(i, t, a), 0, ff_blk(i, j, a))),
                  pl.BlockSpec((None, tf, d), lambda i, j, t, a, h: (expert(i, t, a), ff_blk(i, j, a), 0))],
        out_specs=pl.BlockSpec((tm, d), lambda i, j, t, a, h: (i, 0)),
        scratch_shapes=[pltpu.VMEM((tm, d), F32), pltpu.VMEM((tm, d), BF16), pltpu.SemaphoreType.DMA],
    )
    return pl.pallas_call(
        functools.partial(_moe_kernel, tm=tm),
        grid_spec=grid_spec,
        out_shape=jax.ShapeDtypeStruct((rows, d), F32),
        compiler_params=pltpu.CompilerParams(dimension_semantics=("arbitrary", "arbitrary"),
                                             vmem_limit_bytes=BIG_TILE_VMEM_LIMIT_BYTES),
        name="moe_ffn",
    )(te, nact, half2, xs, wg, wu, wd)


def _combine_kernel(d1_ref, d2_ref, d1n_ref, d2n_ref, x_ref, g1_ref, g2_ref, lg_ref, lb_ref, y_ref, o_ref,
                    buf_ref, sems, *, tc):
    i = pl.program_id(0)

    def gather(da_ref, db_ref, s, unroll=DMA_ISSUE_UNROLL):
        def issue(r, carry):
            for k, dk_ref in enumerate((da_ref, db_ref)):
                pltpu.make_async_copy(y_ref.at[pl.ds(dk_ref[0, r], 1), :],
                                      buf_ref.at[s, k, pl.ds(r, 1), :], sems.at[s]).start()
            return carry

        lax.fori_loop(0, tc, issue, 0, unroll=unroll)

    @pl.when(i == 0)
    def _():
        gather(d1_ref, d2_ref, 0)

    def tile(s):
        @pl.when(i + 1 < pl.num_programs(0))
        def _():
            gather(d1n_ref, d2n_ref, 1 - s, unroll=True)

        for k in range(TOP_K):
            pltpu.make_async_copy(y_ref.at[pl.ds(0, tc), :], buf_ref.at[s, k], sems.at[s]).wait()

        f = g1_ref[...] * buf_ref[s, 0] + g2_ref[...] * buf_ref[s, 1]
        o_ref[...] = _layer_norm(ALPHA * x_ref[...] + f, lg_ref[...], lb_ref[...])

    pl.when(i % 2 == 0)(functools.partial(tile, 0))
    pl.when(i % 2 == 1)(functools.partial(tile, 1))


def _combine_ln(x2d, y, d1, d2, g1, g2, lg, lb):
    n, d = x2d.shape
    tc = _tile(n, 256)
    nb = n // tc
    kern = functools.partial(_combine_kernel, tc=tc)
    cur = lambda: pl.BlockSpec((None, 1, tc), lambda i: (i, 0, 0), memory_space=pltpu.SMEM)
    nxt = lambda: pl.BlockSpec((None, 1, tc), lambda i: (jnp.minimum(i + 1, nb - 1), 0, 0),
                               memory_space=pltpu.SMEM)
    d1b, d2b = d1.reshape(nb, 1, tc), d2.reshape(nb, 1, tc)
    return pl.pallas_call(
        kern,
        grid=(nb,),
        in_specs=[cur(), cur(), nxt(), nxt(),
                  pl.BlockSpec((tc, d), lambda i: (i, 0)),
                  pl.BlockSpec((tc, 1), lambda i: (i, 0)),
                  pl.BlockSpec((tc, 1), lambda i: (i, 0)),
                  pl.BlockSpec((1, d), lambda i: (0, 0)),
                  pl.BlockSpec((1, d), lambda i: (0, 0)),
                  pl.BlockSpec(memory_space=pl.ANY)],
        out_specs=pl.BlockSpec((tc, d), lambda i: (i, 0)),
        out_shape=jax.ShapeDtypeStruct((n, d), F32),
        scratch_shapes=[pltpu.VMEM((2, TOP_K, tc, d), F32), pltpu.SemaphoreType.DMA((2,))],
        compiler_params=_params("arbitrary"),
        name="combine_ln",
    )(d1b, d2b, d1b, d2b, x2d, g1.reshape(n, 1), g2.reshape(n, 1), lg, lb, y)


def kernel(x, ev_w_in, ev_w_pool, ev_pool_scale, ev_lam_q1, ev_lam_k1, ev_lam_q2, ev_lam_k2, ev_subln_g, ev_w_out, ev_ln1_g, ev_ln1_b, ev_ffn_wg, ev_ffn_wu, ev_ffn_wd, ev_ln2_g, ev_ln2_b, od_w_in, od_conv_w, od_w_out, od_ln1_g, od_ln1_b, od_router, od_exp_wg, od_exp_wu, od_exp_wd, od_ln2_g, od_ln2_b):
    batch, seq, d = x.shape
    n = batch * seq
    assert ev_w_in.shape[0] == 1 and od_w_in.shape[0] == 1, "DEPTH == 2: one even and one odd layer"
    h = x.reshape(n, d)
    bf = lambda w: w.astype(BF16)
    row = lambda v: v.reshape(1, -1)

    pool_width = ev_w_pool.shape[1] * ev_w_pool.shape[2]
    u_pool, qkv = _even_inproj(h, bf(ev_w_in[0]), pool_width)
    a = _pool_mixer(u_pool, bf(ev_w_pool[0]), row(ev_pool_scale[0]), seq)
    o = _diff_attention(qkv, row(ev_lam_q1[0]), row(ev_lam_k1[0]), row(ev_lam_q2[0]), row(ev_lam_k2[0]),
                        row(ev_subln_g[0]), batch, seq)
    h = _outproj_ln([a, o], bf(ev_w_out[0]), h, row(ev_ln1_g[0]), row(ev_ln1_b[0]))
    h = _ffn_ln(h, bf(ev_ffn_wg[0]), bf(ev_ffn_wu[0]), bf(ev_ffn_wd[0]), row(ev_ln2_g[0]), row(ev_ln2_b[0]))

    gated = _odd_inproj(h, bf(od_w_in[0]), od_conv_w[0].reshape(CONV_K, -1), seq)
    n_exp = od_router.shape[-1]
    w_router = jnp.pad(od_router[0], ((0, 0), (0, ROUTER_LANES - n_exp)))
    h, logits = _outproj_ln([gated], bf(od_w_out[0]), h, row(od_ln1_g[0]), row(od_ln1_b[0]), w_router)

    tm = _tile(n, 1024)
    n_tiles = (TOP_K * n) // tm + n_exp
    d1, d2, g1, g2, te, nact, zfill, half2 = _route(logits[:, :n_exp].T, tm, n_tiles)
    fill = jnp.concatenate([zfill.reshape(-1), nact.reshape(-1)])
    xs = _dispatch(h, d1, d2, fill, tm, n_tiles)
    y = _moe_ffn(xs, te.reshape(-1), nact.reshape(-1), half2.reshape(-1), od_exp_wg[0], od_exp_wu[0],
                 od_exp_wd[0], tm)
    h = _combine_ln(h, y, d1, d2, g1, g2, row(od_ln2_g[0]), row(od_ln2_b[0]))
    return h.reshape(batch, seq, d)
```
